```python
import jax
import jax.numpy as jnp
from jax import lax
import numpy as np


D_MODEL = 2048
BATCH = 1
SEQ = 8192
DEPTH = 4

GRID_W = 64
CTX_LEN = 256
N_EVEN = (DEPTH + 1) // 2
N_ODD = DEPTH // 2
EPS = 1e-6
ROPE_THETA = 10000.0
ROPE_DIM = 64
QBLOCK = 128

POOL_WINDOWS = (2, 4, 8, 16)
POOL_GROUPS = 4
POOL_W = D_MODEL // 4
POOL_GC = POOL_W // POOL_GROUPS
MLA_HEADS = 12
MLA_NOPE = 128
MLA_VDIM = 128
MLA_QK = MLA_NOPE + ROPE_DIM
MLA_W = MLA_HEADS * MLA_VDIM
Q_RANK = D_MODEL // 4
KV_RANK = D_MODEL // 4
M_HEADS = 4
M_DQK = 128
M_DV = 256
M_QK_W = M_HEADS * M_DQK
M_V_W = M_HEADS * M_DV
M_CHUNK = 64
A_HEADS = 16
A_KV_HEADS = 2
A_HD = 64
A_GROUP = A_HEADS // A_KV_HEADS
A_W = A_HEADS * A_HD
A_KV_W = A_KV_HEADS * A_HD
WINDOW = 128

EV_SIZES = (POOL_W, POOL_W, Q_RANK, KV_RANK, ROPE_DIM, MLA_W)
EV_IN = POOL_W + POOL_W + Q_RANK + KV_RANK + ROPE_DIM + MLA_W
OD_SIZES = (M_QK_W, M_QK_W, M_V_W, M_V_W, 4 * M_HEADS, M_V_W, A_W, A_KV_W, A_KV_W, A_W)
OD_IN = 2 * M_QK_W + 3 * M_V_W + 4 * M_HEADS + 2 * A_W + 2 * A_KV_W
EV_OUT = POOL_W + MLA_W
OD_OUT = M_V_W + A_W

kernel_name = 'hybrid_pool_mla_mlstm_swa_prefix_dit'

F32 = jnp.float32


def rmsnorm(x, w):
    x32 = x.astype(F32)
    y = x32 * lax.rsqrt(jnp.mean(x32 * x32, axis=-1, keepdims=True) + EPS)
    return y.astype(x.dtype) * w


def split_cols(u, sizes):
    idx = np.cumsum(np.array(sizes))[:-1].tolist()
    return jnp.split(u, idx, axis=-1)


def axial_angles(seq_len):
    rows = seq_len // GRID_W
    row, col = jnp.meshgrid(jnp.arange(rows), jnp.arange(GRID_W), indexing='ij')
    row = row.reshape(-1).astype(F32)
    col = col.reshape(-1).astype(F32)
    half = ROPE_DIM // 2
    inv = ROPE_THETA ** (-jnp.arange(0, half, 2, dtype=F32) / half)
    return row[:, None, None] * inv, col[:, None, None] * inv


def rope_axis(x, ang):
    x1, x2 = jnp.split(x, 2, axis=-1)
    cos = jnp.cos(ang).astype(x.dtype)
    sin = jnp.sin(ang).astype(x.dtype)
    return jnp.concatenate([x1 * cos - x2 * sin, x1 * sin + x2 * cos], axis=-1)


def rope_2d(x, ang_r, ang_c):
    xr, xc = jnp.split(x, 2, axis=-1)
    return jnp.concatenate([rope_axis(xr, ang_r), rope_axis(xc, ang_c)], axis=-1)


def pool_branch(u, w_pool, pool_scale):
    B, L, _ = u.shape
    u32 = u.astype(F32)
    csum = jnp.pad(jnp.cumsum(u32, axis=1), ((0, 0), (1, 0), (0, 0)))
    t = jnp.arange(L)
    groups = []
    for g, w in enumerate(POOL_WINDOWS):
        lo = jnp.clip(t - w // 2, 0, L - 1)
        hi = jnp.clip(t + w // 2 - 1, 0, L - 1)
        cs = csum[:, :, g * POOL_GC:(g + 1) * POOL_GC]
        mean = (cs[:, hi + 1] - cs[:, lo]) / (hi - lo + 1).astype(F32)[None, :, None]
        groups.append(mean - u32[:, :, g * POOL_GC:(g + 1) * POOL_GC])
    z = jnp.stack(groups, axis=2).astype(u.dtype)
    z = jnp.einsum('blgc,gcd->blgd', z, w_pool).reshape(B, L, POOL_W)
    return z * pool_scale


def mla_q(c_q, q_norm, w_uq, ang):
    B, L, _ = c_q.shape
    q = (rmsnorm(c_q, q_norm) @ w_uq).reshape(B, L, MLA_HEADS, MLA_QK)
    q_nope, q_rope = q[..., :MLA_NOPE], q[..., MLA_NOPE:]
    if ang is not None:
        q_rope = rope_2d(q_rope, *ang)
    return jnp.concatenate([q_nope, q_rope], axis=-1)


def mla_kv(c_kv, k_r, kv_norm, w_ukv, ang):
    B, L, _ = c_kv.shape
    kv = (rmsnorm(c_kv, kv_norm) @ w_ukv).reshape(B, L, MLA_HEADS, MLA_NOPE + MLA_VDIM)
    k_nope, v = kv[..., :MLA_NOPE], kv[..., MLA_NOPE:]
    k_rope = k_r[:, :, None, :]
    if ang is not None:
        k_rope = rope_2d(k_rope, *ang)
    k = jnp.concatenate([k_nope, jnp.broadcast_to(k_rope, (B, L, MLA_HEADS, ROPE_DIM))], axis=-1)
    return k, v


def block_dense_attention(q, k, v):
    B, S, H, d = q.shape
    scale = d ** -0.5
    qb = q.reshape(B, S // QBLOCK, QBLOCK, H, d).transpose(1, 0, 2, 3, 4)

    def one_block(qi):
        s = jnp.einsum('bqhd,bkhd->bhqk', qi, k).astype(F32) * scale
        p = jax.nn.softmax(s, axis=-1).astype(v.dtype)
        return jnp.einsum('bhqk,bkhd->bqhd', p, v)

    o = lax.map(one_block, qb)
    return o.transpose(1, 0, 2, 3, 4).reshape(B, S, H, v.shape[-1])


def even_mix(h, hc, w_in, q_norm, kv_norm, w_uq, w_ukv, w_pool, pool_scale, w_out, ang, need_ctx):
    B, L, _ = h.shape
    Lc = hc.shape[1]
    p_in, p_gate, c_q, c_kv, k_r, a_gate = split_cols(h @ w_in, EV_SIZES)
    pc_in, pc_gate, cc_q, cc_kv, kc_r, ac_gate = split_cols(hc @ w_in, EV_SIZES)
    q = mla_q(c_q, q_norm, w_uq, ang)
    k, v = mla_kv(c_kv, k_r, kv_norm, w_ukv, ang)
    kc, vc = mla_kv(cc_kv, kc_r, kv_norm, w_ukv, None)
    o = block_dense_attention(q, jnp.concatenate([k, kc], axis=1), jnp.concatenate([v, vc], axis=1))
    y = jnp.concatenate([pool_branch(p_in, w_pool, pool_scale) * jax.nn.silu(p_gate),
                         o.reshape(B, L, MLA_W) * jax.nn.silu(a_gate)], axis=-1) @ w_out
    if not need_ctx:
        return y, None
    qc = mla_q(cc_q, q_norm, w_uq, None)
    oc = block_dense_attention(qc, kc, vc)
    yc = jnp.concatenate([pool_branch(pc_in, w_pool, pool_scale) * jax.nn.silu(pc_gate),
                          oc.reshape(B, Lc, MLA_W) * jax.nn.silu(ac_gate)], axis=-1) @ w_out
    return y, yc


def mlstm_init_state(B):
    return (jnp.zeros((B, M_HEADS, M_DV, M_DQK), F32),
            jnp.zeros((B, M_HEADS, M_DQK), F32),
            jnp.zeros((B, M_HEADS), F32))


def mlstm_scan(q, k, v, ig, lf, state, return_h):
    B, H, L, _ = q.shape
    dv = v.shape[-1]
    nc = L // M_CHUNK

    def chunks(a):
        return jnp.moveaxis(a.reshape((B, H, nc, M_CHUNK) + a.shape[3:]), 2, 0)

    tril = jnp.tril(jnp.ones((M_CHUNK, M_CHUNK), dtype=bool))

    def step(carry, inp):
        C, n, m = carry
        qc, kc, vc, ic, fc = inp
        b = jnp.cumsum(fc, axis=-1)
        b_last = b[..., -1]
        a = b_last[..., None] - b + ic
        m_new = jnp.maximum(b_last + m, a.max(axis=-1))
        decay = jnp.exp(b_last + m - m_new)
        w = jnp.exp(a - m_new[..., None])
        C_new = decay[..., None, None] * C + jnp.einsum('bhs,bhsv,bhsk->bhvk', w, vc, kc)
        n_new = decay[..., None] * n + jnp.einsum('bhs,bhsk->bhk', w, kc)
        if not return_h:
            return (C_new, n_new, m_new), None
        dlog = jnp.where(tril, b[..., :, None] - b[..., None, :] + ic[..., None, :], -jnp.inf)
        inter = b + m[..., None]
        m_t = jnp.maximum(inter, dlog.max(axis=-1))
        iw = jnp.exp(inter - m_t)
        s = jnp.einsum('bhtk,bhsk->bhts', qc, kc) * jnp.exp(dlog - m_t[..., None])
        num = iw[..., None] * jnp.einsum('bhvk,bhtk->bhtv', C, qc) + jnp.einsum('bhts,bhsv->bhtv', s, vc)
        den = iw * jnp.einsum('bhk,bhtk->bht', n, qc) + s.sum(axis=-1)
        hout = num / jnp.maximum(jnp.abs(den), jnp.exp(-m_t))[..., None]
        return (C_new, n_new, m_new), hout

    final, hs = lax.scan(step, state, (chunks(q), chunks(k), chunks(v), chunks(ig), chunks(lf)))
    if not return_h:
        return None, final
    return jnp.moveaxis(hs, 0, 2).reshape(B, H, L, dv), final


def mlstm_prep(mq, mk, mv, mg, gate_b):
    B, L, _ = mq.shape
    q = mq.reshape(B, L, M_HEADS, M_DQK).transpose(0, 2, 1, 3).astype(F32) * (M_DQK ** -0.5)
    k = mk.reshape(B, L, M_HEADS, M_DQK).transpose(0, 2, 1, 3).astype(F32)
    v = mv.reshape(B, L, M_HEADS, M_DV).transpose(0, 2, 1, 3).astype(F32)
    g = (mg.astype(F32) + gate_b.astype(F32)).reshape(B, L, 4, M_HEADS).transpose(2, 0, 3, 1)
    return q, k, v, g[0], jax.nn.log_sigmoid(g[1]), g[2], jax.nn.log_sigmoid(g[3])


def bidir_mlstm(lat, ctxp, need_ctx):
    q, k, v, i_f, l_f, i_b, l_b = lat
    cq, ck, cv, ci_f, cl_f, ci_b, cl_b = ctxp
    flip = lambda a: jnp.flip(a, axis=2)
    zero = mlstm_init_state(q.shape[0])
    hcf, s_f = mlstm_scan(cq, ck, cv, ci_f, cl_f, zero, need_ctx)
    hcb, s_b = mlstm_scan(flip(cq), flip(ck), flip(cv), flip(ci_b), flip(cl_b), zero, need_ctx)
    hf, _ = mlstm_scan(q, k, v, i_f, l_f, s_f, True)
    hb, _ = mlstm_scan(flip(q), flip(k), flip(v), flip(i_b), flip(l_b), s_b, True)
    h = hf + flip(hb)
    hc = hcf + flip(hcb) if need_ctx else None
    return h, hc


def mlstm_out(hsum, o, z, head_norm):
    B, H, L, dv = hsum.shape
    hn = rmsnorm(hsum.transpose(0, 2, 1, 3), head_norm.reshape(M_HEADS, M_DV)).reshape(B, L, M_V_W)
    return hn * jax.nn.sigmoid(o) * jax.nn.silu(z)


def window_attention(q, k, v, kc, vc, sink):
    B, L = q.shape[:2]
    Lc = kc.shape[1]
    nb = L // WINDOW
    scale = A_HD ** -0.5
    qb = q.reshape(B, nb, WINDOW, A_KV_HEADS, A_GROUP, A_HD)

    def band(a):
        ap = jnp.pad(a, ((0, 0), (WINDOW, WINDOW), (0, 0), (0, 0))).reshape(B, nb + 2, WINDOW, A_KV_HEADS, A_HD)
        return jnp.concatenate([ap[:, :-2], ap[:, 1:-1], ap[:, 2:]], axis=2)

    kb, vb = band(k), band(v)
    qi = jnp.arange(WINDOW)[:, None]
    kj = jnp.arange(3 * WINDOW)[None, :]
    kpos = jnp.arange(nb)[:, None, None] * WINDOW + kj[None] - WINDOW
    mask = (jnp.abs(kj - WINDOW - qi) <= WINDOW)[None] & (kpos >= 0) & (kpos < L)
    s_loc = jnp.einsum('bnqhgd,bnkhd->bnhgqk', qb, kb).astype(F32) * scale
    s_loc = jnp.where(mask[None, :, None, None], s_loc, -jnp.inf)
    s_ctx = jnp.einsum('bnqhgd,bchd->bnhgqc', qb, kc).astype(F32) * scale
    s_sink = jnp.broadcast_to(sink.astype(F32)[None, None, :, :, None, None], s_loc.shape[:-1] + (1,))
    p = jax.nn.softmax(jnp.concatenate([s_loc, s_ctx, s_sink], axis=-1), axis=-1).astype(v.dtype)
    p_loc = p[..., :3 * WINDOW]
    p_ctx = p[..., 3 * WINDOW:3 * WINDOW + Lc]
    o = jnp.einsum('bnhgqk,bnkhd->bnqhgd', p_loc, vb) + jnp.einsum('bnhgqc,bchd->bnqhgd', p_ctx, vc)
    return o.reshape(B, L, A_W)


def ctx_gqa(q, k, v, sink):
    B, Lc = q.shape[:2]
    s = jnp.einsum('bqhgd,bkhd->bhgqk', q, k).astype(F32) * (A_HD ** -0.5)
    s_sink = jnp.broadcast_to(sink.astype(F32)[None, :, :, None, None], s.shape[:-1] + (1,))
    p = jax.nn.softmax(jnp.concatenate([s, s_sink], axis=-1), axis=-1)[..., :-1].astype(v.dtype)
    return jnp.einsum('bhgqk,bkhd->bqhgd', p, v).reshape(B, Lc, A_W)


def odd_mix(h, hc, w_in, gate_b, head_norm, sink, w_out, ang, need_ctx):
    B, L, _ = h.shape
    Lc = hc.shape[1]
    mq, mk, mv, mo, mg, mz, aq, ak, av, az = split_cols(h @ w_in, OD_SIZES)
    cmq, cmk, cmv, cmo, cmg, cmz, caq, cak, cav, caz = split_cols(hc @ w_in, OD_SIZES)
    hm, hmc = bidir_mlstm(mlstm_prep(mq, mk, mv, mg, gate_b), mlstm_prep(cmq, cmk, cmv, cmg, gate_b), need_ctx)
    ym = mlstm_out(hm, mo, mz, head_norm)
    sink_g = sink.reshape(A_KV_HEADS, A_GROUP)
    q = rope_2d(aq.reshape(B, L, A_HEADS, A_HD), *ang).reshape(B, L, A_KV_HEADS, A_GROUP, A_HD)
    k = rope_2d(ak.reshape(B, L, A_KV_HEADS, A_HD), *ang)
    v = av.reshape(B, L, A_KV_HEADS, A_HD)
    kc = cak.reshape(B, Lc, A_KV_HEADS, A_HD)
    vc = cav.reshape(B, Lc, A_KV_HEADS, A_HD)
    ya = window_attention(q, k, v, kc, vc, sink_g) * jax.nn.silu(az)
    y = jnp.concatenate([ym, ya], axis=-1) @ w_out
    if not need_ctx:
        return y, None
    ymc = mlstm_out(hmc, cmo, cmz, head_norm)
    qc = caq.reshape(B, Lc, A_KV_HEADS, A_GROUP, A_HD)
    yac = ctx_gqa(qc, kc, vc, sink_g) * jax.nn.silu(caz)
    yc = jnp.concatenate([ymc, yac], axis=-1) @ w_out
    return y, yc


def setup_inputs(seed: int = 0) -> dict:
    key = jax.random.key(seed)
    ks = jax.random.split(key, 24)
    nrm = lambda k, shape, s: jax.random.normal(k, shape, F32) * s
    gain = lambda k, shape: 1.0 + 0.02 * jax.random.normal(k, shape, F32)
    f_bias = jnp.linspace(3.0, 6.0, M_HEADS, dtype=F32)
    gate_off = jnp.concatenate([jnp.zeros((M_HEADS,), F32), f_bias, jnp.zeros((M_HEADS,), F32), f_bias])
    return {
        'x': nrm(ks[0], (BATCH, SEQ, D_MODEL), 1.0),
        'c': nrm(ks[1], (BATCH, D_MODEL), 1.0),
        'ctx': nrm(ks[2], (BATCH, CTX_LEN, D_MODEL), 1.0),
        'c_ctx': nrm(ks[3], (D_MODEL,), 1.0),
        'ada_w': nrm(ks[4], (DEPTH, D_MODEL, 3 * D_MODEL), 0.5 * D_MODEL ** -0.5),
        'ada_b': nrm(ks[5], (DEPTH, 3 * D_MODEL), 0.02),
        'norm_w': gain(ks[6], (DEPTH, D_MODEL)),
        'ev_w_in': nrm(ks[7], (N_EVEN, D_MODEL, EV_IN), D_MODEL ** -0.5),
        'ev_q_norm': gain(ks[8], (N_EVEN, Q_RANK)),
        'ev_kv_norm': gain(ks[9], (N_EVEN, KV_RANK)),
        'ev_w_uq': nrm(ks[10], (N_EVEN, Q_RANK, MLA_HEADS * MLA_QK), Q_RANK ** -0.5),
        'ev_w_ukv': nrm(ks[11], (N_EVEN, KV_RANK, MLA_HEADS * (MLA_NOPE + MLA_VDIM)), KV_RANK ** -0.5),
        'ev_w_pool': nrm(ks[12], (N_EVEN, POOL_GROUPS, POOL_GC, POOL_GC), POOL_GC ** -0.5),
        'ev_pool_scale': gain(ks[13], (N_EVEN, POOL_W)),
        'ev_w_out': nrm(ks[14], (N_EVEN, EV_OUT, D_MODEL), EV_OUT ** -0.5),
        'od_w_in': nrm(ks[15], (N_ODD, D_MODEL, OD_IN), D_MODEL ** -0.5),
        'od_gate_b': gate_off + nrm(ks[16], (N_ODD, 4 * M_HEADS), 0.1),
        'od_head_norm': gain(ks[17], (N_ODD, M_V_W)),
        'od_sink': nrm(ks[18], (N_ODD, A_HEADS), 0.5),
        'od_w_out': nrm(ks[19], (N_ODD, OD_OUT, D_MODEL), OD_OUT ** -0.5),
        'final_norm': gain(ks[20], (D_MODEL,)),
    }


def reference(x, c, ctx, c_ctx, ada_w, ada_b, norm_w, ev_w_in, ev_q_norm, ev_kv_norm, ev_w_uq, ev_w_ukv,
              ev_w_pool, ev_pool_scale, ev_w_out, od_w_in, od_gate_b, od_head_norm, od_sink, od_w_out, final_norm):
    L = x.shape[1]
    ang = axial_angles(L)
    xc = ctx
    sc = jax.nn.silu(c)
    scc = jax.nn.silu(c_ctx)
    for i in range(DEPTH):
        need_ctx = i < DEPTH - 1
        shift, scale, gate = jnp.split((sc @ ada_w[i] + ada_b[i])[:, None, :], 3, axis=-1)
        shift_c, scale_c, gate_c = jnp.split(scc @ ada_w[i] + ada_b[i], 3, axis=-1)
        h = rmsnorm(x, norm_w[i]) * (1.0 + scale) + shift
        hc = rmsnorm(xc, norm_w[i]) * (1.0 + scale_c) + shift_c
        j = i // 2
        if i % 2 == 0:
            y, yc = even_mix(h, hc, ev_w_in[j], ev_q_norm[j], ev_kv_norm[j], ev_w_uq[j], ev_w_ukv[j],
                             ev_w_pool[j], ev_pool_scale[j], ev_w_out[j], ang, need_ctx)
        else:
            y, yc = odd_mix(h, hc, od_w_in[j], od_gate_b[j], od_head_norm[j], od_sink[j], od_w_out[j],
                            ang, need_ctx)
        x = x + gate * y
        if need_ctx:
            xc = xc + gate_c * yc
    return rmsnorm(x, final_norm)
```

```python
import functools
import math

import numpy as np
import jax
import jax.numpy as jnp
from jax import lax
from jax.experimental import pallas as pl
from jax.experimental.pallas import tpu as pltpu

F32 = jnp.float32
BF16 = jnp.bfloat16

D_MODEL = 2048
SEQ = 8192
CTX_LEN = 256
ROWS = SEQ + CTX_LEN
DEPTH = 4
GRID_W = 64
EPS = 1e-6
ROPE_THETA = 10000.0
ROPE_DIM = 64
LOG2E = math.log2(math.e)

LANES = 128
ROW_BLK = 256
N_LAT_BLK = SEQ // ROW_BLK
N_BLK = ROWS // ROW_BLK
MM_ROWS = 768

POOL_WINDOWS = (2, 4, 8, 16)
POOL_W = 512
POOL_GC = 128
POOL_HALO = 16

MLA_HEADS = 12
MLA_NOPE = 128
MLA_VDIM = 128
MLA_QK = MLA_NOPE + ROPE_DIM
MLA_W = MLA_HEADS * MLA_VDIM
Q_RANK = 512
KV_RANK = 512
HEAD_PAD = 256

M_HEADS = 4
M_DQK = 128
M_DV = 256
M_QK_W = 512
M_V_W = 1024
M_CHUNK = 256

A_HEADS = 16
A_KV_HEADS = 2
A_HD = 64
A_W = 1024
WINDOW = 128

NEG_BIG = -1e30

EV_N = 3840
EV_COL_CQ = 1024
EV_COL_CKV = 1536
EV_COL_AGATE = 2048
EV_COL_KR = 3584
EV_TN = 768

OD_N = 8064
OD_COL_MO = 2048
OD_COL_MZ = 3072
OD_COL_AQ = 4096
OD_COL_AQR = 5120
OD_COL_AZ = 6144
OD_COL_KV = 7168
OD_KV_W = 896
OD_COL_MG = 7936
OD_TN = 896


def _rot64(w):
    return jnp.concatenate([-w[..., 16:32], w[..., 0:16], -w[..., 48:64], w[..., 32:48]], axis=-1)


def _ev_layout(w):
    kr = w[:, 2048:2112]
    krr = _rot64(kr)
    return jnp.concatenate([w[:, :2048], w[:, 2112:3648], kr, kr, krr, krr], axis=1).astype(BF16)


def _od_layout(w):
    d = w.shape[0]
    mg, mz, aq, ak, av, az = 3072, 3088, 4112, 5136, 5264, 5392
    aq_w = w[:, aq:aq + A_W]
    aq_r = _rot64(aq_w.reshape(d, A_HEADS, A_HD)).reshape(d, A_W)
    k = w[:, ak:ak + 2 * A_HD].reshape(d, A_KV_HEADS, A_HD)
    kr = _rot64(k)
    v = w[:, av:av + 2 * A_HD].reshape(d, A_KV_HEADS, A_HD)
    dup = lambda t: jnp.concatenate([t[:, 0], t[:, 0], t[:, 1], t[:, 1]], axis=1)
    pad = jnp.zeros((d, LANES - 4 * M_HEADS), w.dtype)
    return jnp.concatenate([w[:, :mg], w[:, mz:aq], aq_w, aq_r, w[:, az:az + A_W], dup(k), dup(kr), dup(v),
                            w[:, mg:mz], pad], axis=1).astype(BF16)


def _rope_tables():
    t = jnp.arange(SEQ)
    row = (t // GRID_W).astype(F32)
    col = (t % GRID_W).astype(F32)
    half = ROPE_DIM // 2
    inv = ROPE_THETA ** (-jnp.arange(0, half, 2, dtype=F32) / half)
    ar = row[:, None] * inv
    ac = col[:, None] * inv
    cos64 = jnp.concatenate([jnp.cos(ar), jnp.cos(ar), jnp.cos(ac), jnp.cos(ac)], axis=-1)
    sin64 = jnp.concatenate([jnp.sin(ar), jnp.sin(ar), jnp.sin(ac), jnp.sin(ac)], axis=-1)
    cos = jnp.concatenate([jnp.tile(cos64, (1, 2)), jnp.ones((CTX_LEN, LANES), F32)], axis=0)
    sin = jnp.concatenate([jnp.tile(sin64, (1, 2)), jnp.zeros((CTX_LEN, LANES), F32)], axis=0)
    return cos, sin


def _dot(a, b):
    return jnp.dot(a, b, preferred_element_type=F32)


def _dot_nt(a, b):
    return lax.dot_general(a, b, (((1,), (1,)), ((), ())), preferred_element_type=F32)


def _dot_tn(a, b):
    return lax.dot_general(a, b, (((0,), (0,)), ((), ())), preferred_element_type=F32)


def _rms(x):
    return x * lax.rsqrt(jnp.mean(x * x, axis=-1, keepdims=True) + EPS)


def _modnorm(x, nw, mod):
    return (_rms(x) * nw) * (1.0 + mod[1:2, :]) + mod[0:1, :]


def _silu(x):
    return x * (1.0 / (1.0 + jnp.exp(-x)))


def _sigmoid(x):
    return 1.0 / (1.0 + jnp.exp(-x))


def _split3(x):
    hi = x.astype(BF16)
    r1 = x - hi.astype(F32)
    mid = r1.astype(BF16)
    lo = (r1 - mid.astype(F32)).astype(BF16)
    return hi, mid, lo


def _is_ctx_blk(i):
    return jnp.where(i < N_LAT_BLK, 0, 1)


def _ada_kernel(s_ref, w_ref, b_ref, o_ref):
    s = _silu(s_ref[...])
    hi = s.astype(BF16)
    lo = (s - hi.astype(F32)).astype(BF16)
    lhs = jnp.concatenate([hi, lo], axis=0)
    r = _dot(lhs, w_ref[0].astype(BF16))
    o_ref[0] = r[0:8] + r[8:16] + b_ref[0]


def _ada(c, c_ctx, ada_w, ada_b):
    tn = 1536
    s = jnp.concatenate([c.reshape(1, D_MODEL), c_ctx.reshape(1, D_MODEL), jnp.zeros((6, D_MODEL), F32)], axis=0)
    out = pl.pallas_call(
        _ada_kernel,
        grid=(DEPTH, 3 * D_MODEL // tn),
        in_specs=[pl.BlockSpec((8, D_MODEL), lambda l, j: (0, 0)),
                  pl.BlockSpec((1, D_MODEL, tn), lambda l, j: (l, 0, j)),
                  pl.BlockSpec((1, 1, tn), lambda l, j: (l, 0, j))],
        out_specs=pl.BlockSpec((1, 8, tn), lambda l, j: (l, 0, j)),
        out_shape=jax.ShapeDtypeStruct((DEPTH, 8, 3 * D_MODEL), F32),
        name="ada_mod",
    )(s, ada_w, ada_b.reshape(DEPTH, 1, 3 * D_MODEL))
    m = out[:, :2, :].reshape(DEPTH, 2, 3, D_MODEL)
    return jnp.concatenate([m, jnp.zeros((DEPTH, 2, 5, D_MODEL), F32)], axis=2)


def _prenorm_kernel(x_ref, c_ref, nw_ref, mod_ref, xo_ref, h_ref):
    i = pl.program_id(0)

    def emit(src):
        x = src[...]
        xo_ref[...] = x
        h_ref[...] = _modnorm(x, nw_ref[...], mod_ref[0]).astype(BF16)

    @pl.when(i < N_LAT_BLK)
    def _():
        emit(x_ref)

    @pl.when(i == N_LAT_BLK)
    def _():
        emit(c_ref)


def _prenorm(x, ctx, nw, mod):
    return pl.pallas_call(
        _prenorm_kernel,
        grid=(N_BLK,),
        in_specs=[pl.BlockSpec((ROW_BLK, D_MODEL), lambda i: (jnp.minimum(i, N_LAT_BLK - 1), 0)),
                  pl.BlockSpec((CTX_LEN, D_MODEL), lambda i: (0, 0)),
                  pl.BlockSpec((1, D_MODEL), lambda i: (0, 0)),
                  pl.BlockSpec((1, 8, D_MODEL), lambda i: (_is_ctx_blk(i), 0, 0))],
        out_specs=[pl.BlockSpec((ROW_BLK, D_MODEL), lambda i: (i, 0)),
                   pl.BlockSpec((ROW_BLK, D_MODEL), lambda i: (i, 0))],
        out_shape=[jax.ShapeDtypeStruct((ROWS, D_MODEL), F32),
                   jax.ShapeDtypeStruct((ROWS, D_MODEL), BF16)],
        name="prenorm",
    )(x, ctx, nw.reshape(1, D_MODEL), mod)


def _mm_kernel(h_ref, w_ref, o_ref):
    o_ref[...] = _dot(h_ref[...], w_ref[...])


def _in_proj(h, w, tn):
    n = w.shape[1]
    return pl.pallas_call(
        _mm_kernel,
        grid=(ROWS // MM_ROWS, n // tn),
        in_specs=[pl.BlockSpec((MM_ROWS, D_MODEL), lambda i, j: (i, 0)),
                  pl.BlockSpec((D_MODEL, tn), lambda i, j: (0, j))],
        out_specs=pl.BlockSpec((MM_ROWS, tn), lambda i, j: (i, j)),
        out_shape=jax.ShapeDtypeStruct((ROWS, n), F32),
        name="in_proj",
    )(h, w)


Q_SCALE = (MLA_QK ** -0.5) * LOG2E


def _qproj_kernel(cq_ref, nw_ref, wn_ref, wr_ref, wrr_ref, cos_ref, sin_ref, q_ref):
    xn = (_rms(cq_ref[...]) * nw_ref[...]).astype(BF16)
    qn = _dot(xn, wn_ref[...]) * Q_SCALE
    cos = cos_ref[...]
    sin = sin_ref[...]
    for h in range(MLA_HEADS):
        q_ref[:, h * HEAD_PAD:h * HEAD_PAD + LANES] = qn[:, h * LANES:(h + 1) * LANES].astype(BF16)
    qr = _dot(xn, wr_ref[...])
    qrr = _dot(xn, wrr_ref[...])
    for p in range(MLA_HEADS // 2):
        sl = slice(p * LANES, (p + 1) * LANES)
        rp = ((qr[:, sl] * cos + qrr[:, sl] * sin) * Q_SCALE).astype(BF16)
        for h in (2 * p, 2 * p + 1):
            q_ref[:, h * HEAD_PAD + LANES:(h + 1) * HEAD_PAD] = rp


def _q_proj(u, nw, wn, wr, wrr, cos, sin):
    t = MM_ROWS
    return pl.pallas_call(
        _qproj_kernel,
        grid=(ROWS // t,),
        in_specs=[pl.BlockSpec((t, Q_RANK), lambda i: (i, EV_COL_CQ // Q_RANK)),
                  pl.BlockSpec((1, Q_RANK), lambda i: (0, 0)),
                  pl.BlockSpec(wn.shape, lambda i: (0, 0)),
                  pl.BlockSpec(wr.shape, lambda i: (0, 0)),
                  pl.BlockSpec(wrr.shape, lambda i: (0, 0)),
                  pl.BlockSpec((t, LANES), lambda i: (i, 0)),
                  pl.BlockSpec((t, LANES), lambda i: (i, 0))],
        out_specs=pl.BlockSpec((t, MLA_HEADS * HEAD_PAD), lambda i: (i, 0)),
        out_shape=jax.ShapeDtypeStruct((ROWS, MLA_HEADS * HEAD_PAD), BF16),
        name="mla_q_proj",
    )(u, nw.reshape(1, Q_RANK), wn, wr, wrr, cos, sin)


def _kvproj_kernel(ckv_ref, kr_ref, nw_ref, wk_ref, wv_ref, cos_ref, sin_ref, k_ref, v_ref):
    xn = (_rms(ckv_ref[...]) * nw_ref[...]).astype(BF16)
    kn = _dot(xn, wk_ref[...])
    vv = _dot(xn, wv_ref[...])
    kr = kr_ref[...]
    kr2 = kr[:, :LANES] * cos_ref[...] + kr[:, LANES:] * sin_ref[...]
    lane = lax.broadcasted_iota(jnp.int32, kr2.shape, 1)
    k_lo = jnp.where(lane < ROPE_DIM, kr2, 0.0).astype(BF16)
    k_hi = jnp.where(lane >= ROPE_DIM, kr2, 0.0).astype(BF16)
    ones_col = jnp.where(lane == 0, 1.0, 0.0).astype(BF16)
    for h in range(MLA_HEADS):
        sl = slice(h * LANES, (h + 1) * LANES)
        k_ref[:, h * HEAD_PAD:h * HEAD_PAD + LANES] = kn[:, sl].astype(BF16)
        k_ref[:, h * HEAD_PAD + LANES:(h + 1) * HEAD_PAD] = k_lo if h % 2 == 0 else k_hi
        v_ref[:, h * HEAD_PAD:h * HEAD_PAD + LANES] = vv[:, sl].astype(BF16)
        v_ref[:, h * HEAD_PAD + LANES:(h + 1) * HEAD_PAD] = ones_col


def _kv_proj(u, nw, wk, wv, cos, sin):
    t = MM_ROWS
    wide = MLA_HEADS * HEAD_PAD
    return pl.pallas_call(
        _kvproj_kernel,
        grid=(ROWS // t,),
        in_specs=[pl.BlockSpec((t, KV_RANK), lambda i: (i, EV_COL_CKV // KV_RANK)),
                  pl.BlockSpec((t, 2 * LANES), lambda i: (i, EV_COL_KR // (2 * LANES))),
                  pl.BlockSpec((1, KV_RANK), lambda i: (0, 0)),
                  pl.BlockSpec(wk.shape, lambda i: (0, 0)),
                  pl.BlockSpec(wv.shape, lambda i: (0, 0)),
                  pl.BlockSpec((t, LANES), lambda i: (i, 0)),
                  pl.BlockSpec((t, LANES), lambda i: (i, 0))],
        out_specs=[pl.BlockSpec((t, wide), lambda i: (i, 0)),
                   pl.BlockSpec((t, wide), lambda i: (i, 0))],
        out_shape=[jax.ShapeDtypeStruct((ROWS, wide), BF16),
                   jax.ShapeDtypeStruct((ROWS, wide), BF16)],
        name="mla_kv_proj",
    )(u, u, nw.reshape(1, KV_RANK), wk, wv, cos, sin)


FL_TQ = 768
FL_TK = 768
FL_LAST = ROWS // FL_TQ - 1
FL_LAT_IN_LAST = SEQ - FL_LAST * FL_TQ


def _attend(q, k_ref, v_ref, k_start, n_chunks, tk):
    rows = q.shape[0]

    def body(j, carry):
        m_prev, acc = carry
        off = pl.multiple_of(k_start + j * tk, LANES)
        k = k_ref[pl.ds(off, tk), :]
        v = v_ref[pl.ds(off, tk), :]
        s = _dot_nt(q, k)
        m_new = jnp.maximum(m_prev, jnp.max(s, axis=1, keepdims=True))
        alpha = jnp.exp2(m_prev - m_new)
        p = jnp.exp2(s - m_new).astype(BF16)
        return m_new, alpha * acc + _dot(p, v)

    m0 = jnp.full((rows, 1), NEG_BIG, F32)
    acc0 = jnp.zeros((rows, HEAD_PAD), F32)
    _, acc = lax.fori_loop(0, n_chunks, body, (m0, acc0))
    return acc[:, :MLA_VDIM] / acc[:, MLA_VDIM:MLA_VDIM + 1]


def _flash_kernel(q_ref, k_ref, v_ref, g_ref, o_ref):
    i = pl.program_id(1)

    @pl.when(i < FL_LAST)
    def _():
        o = _attend(q_ref[...], k_ref, v_ref, 0, ROWS // FL_TK, FL_TK)
        o_ref[...] = (o * _silu(g_ref[...])).astype(BF16)

    @pl.when(i == FL_LAST)
    def _():
        nl = FL_LAT_IN_LAST
        o = _attend(q_ref[0:nl, :], k_ref, v_ref, 0, ROWS // FL_TK, FL_TK)
        o_ref[0:nl, :] = (o * _silu(g_ref[0:nl, :])).astype(BF16)
        oc = _attend(q_ref[nl:, :], k_ref, v_ref, SEQ, 1, CTX_LEN)
        o_ref[nl:, :] = (oc * _silu(g_ref[nl:, :])).astype(BF16)


def _flash(q, k, v, u):
    return pl.pallas_call(
        _flash_kernel,
        grid=(MLA_HEADS, ROWS // FL_TQ),
        in_specs=[pl.BlockSpec((FL_TQ, HEAD_PAD), lambda h, i: (i, h)),
                  pl.BlockSpec((ROWS, HEAD_PAD), lambda h, i: (0, h)),
                  pl.BlockSpec((ROWS, HEAD_PAD), lambda h, i: (0, h)),
                  pl.BlockSpec((FL_TQ, MLA_VDIM), lambda h, i: (i, EV_COL_AGATE // MLA_VDIM + h))],
        out_specs=pl.BlockSpec((FL_TQ, MLA_VDIM), lambda h, i: (i, h)),
        out_shape=jax.ShapeDtypeStruct((ROWS, MLA_W), BF16),
        name="mla_attention",
    )(q, k, v, u)


def _pool_kernel(prev_ref, cur_ref, next_ref, gate_ref, w_ref, sc_ref, o_ref):
    i = pl.program_id(0)
    first = jnp.logical_or(i == 0, i == N_LAT_BLK)
    last = i >= N_LAT_BLK - 1
    lo_lim = jnp.where(first, 0, -POOL_HALO)
    hi_lim = jnp.where(last, ROW_BLK - 1, ROW_BLK - 1 + POOL_HALO)
    src_rows = ROW_BLK + 2 * POOL_HALO
    t = lax.broadcasted_iota(jnp.int32, (ROW_BLK, src_rows), 0)
    j = lax.broadcasted_iota(jnp.int32, (ROW_BLK, src_rows), 1) - POOL_HALO
    t1 = lax.broadcasted_iota(jnp.int32, (ROW_BLK, 1), 0)
    for g, w in enumerate(POOL_WINDOWS):
        sl = slice(g * POOL_GC, (g + 1) * POOL_GC)
        cur = cur_ref[:, sl]
        src = jnp.concatenate([prev_ref[ROW_BLK - POOL_HALO:, sl], cur, next_ref[:POOL_HALO, sl]], axis=0)
        lo = jnp.maximum(t - w // 2, lo_lim)
        hi = jnp.minimum(t + w // 2 - 1, hi_lim)
        band = jnp.where(jnp.logical_and(j >= lo, j <= hi), 1.0, 0.0).astype(BF16)
        cnt = (jnp.minimum(t1 + w // 2 - 1, hi_lim) - jnp.maximum(t1 - w // 2, lo_lim) + 1).astype(F32)
        s_hi, s_mid, s_lo = _split3(src)
        wsum = _dot(band, s_hi) + _dot(band, s_mid) + _dot(band, s_lo)
        z = (wsum / cnt - cur).astype(BF16)
        y = _dot(z, w_ref[g]) * sc_ref[:, sl]
        o_ref[:, sl] = (y * _silu(gate_ref[:, sl])).astype(BF16)


def _pool(u, w_pool, pool_scale):
    def prev_map(i):
        return (jnp.where(i < N_LAT_BLK, jnp.maximum(i - 1, 0), N_LAT_BLK), 0)

    def next_map(i):
        return (jnp.where(i < N_LAT_BLK, jnp.minimum(i + 1, N_LAT_BLK - 1), N_LAT_BLK), 0)

    return pl.pallas_call(
        _pool_kernel,
        grid=(N_BLK,),
        in_specs=[pl.BlockSpec((ROW_BLK, POOL_W), prev_map),
                  pl.BlockSpec((ROW_BLK, POOL_W), lambda i: (i, 0)),
                  pl.BlockSpec((ROW_BLK, POOL_W), next_map),
                  pl.BlockSpec((ROW_BLK, POOL_W), lambda i: (i, 1)),
                  pl.BlockSpec((4, POOL_GC, POOL_GC), lambda i: (0, 0, 0)),
                  pl.BlockSpec((1, POOL_W), lambda i: (0, 0))],
        out_specs=pl.BlockSpec((ROW_BLK, POOL_W), lambda i: (i, 0)),
        out_shape=jax.ShapeDtypeStruct((ROWS, POOL_W), BF16),
        name="pool_mix",
    )(u, u, u, u, w_pool.astype(BF16), pool_scale.reshape(1, POOL_W))


def _outproj_kernel(a_ref, b_ref, w1_ref, w2_ref, x_ref, mod_ref, nw_ref, modn_ref, xo_ref, h_ref):
    y = _dot(a_ref[...], w1_ref[...]) + _dot(b_ref[...], w2_ref[...])
    xn = x_ref[...] + mod_ref[0, 2:3, :] * y
    xo_ref[...] = xn
    h_ref[...] = _modnorm(xn, nw_ref[...], modn_ref[0]).astype(BF16)


def _outproj_final_kernel(a_ref, b_ref, w1_ref, w2_ref, x_ref, mod_ref, nw_ref, o_ref):
    y = _dot(a_ref[...], w1_ref[...]) + _dot(b_ref[...], w2_ref[...])
    xn = x_ref[...] + mod_ref[0, 2:3, :] * y
    o_ref[...] = _rms(xn) * nw_ref[...]


def _out_proj(a, b, w1, w2, x, mod, nw_next, mod_next):
    k1, k2 = a.shape[1], b.shape[1]
    row = lambda i: (i, 0)
    full = lambda i: (0, 0)
    stream = lambda i: (_is_ctx_blk(i), 0, 0)
    common = [pl.BlockSpec((ROW_BLK, k1), row), pl.BlockSpec((ROW_BLK, k2), row),
              pl.BlockSpec((k1, D_MODEL), full), pl.BlockSpec((k2, D_MODEL), full),
              pl.BlockSpec((ROW_BLK, D_MODEL), row), pl.BlockSpec((1, 8, D_MODEL), stream),
              pl.BlockSpec((1, D_MODEL), full)]
    if mod_next is None:
        return pl.pallas_call(
            _outproj_final_kernel,
            grid=(N_LAT_BLK,),
            in_specs=common,
            out_specs=pl.BlockSpec((ROW_BLK, D_MODEL), row),
            out_shape=jax.ShapeDtypeStruct((SEQ, D_MODEL), F32),
            name="out_proj_final",
        )(a, b, w1, w2, x, mod, nw_next.reshape(1, D_MODEL))
    return pl.pallas_call(
        _outproj_kernel,
        grid=(N_BLK,),
        in_specs=common + [pl.BlockSpec((1, 8, D_MODEL), stream)],
        out_specs=[pl.BlockSpec((ROW_BLK, D_MODEL), row), pl.BlockSpec((ROW_BLK, D_MODEL), row)],
        out_shape=[jax.ShapeDtypeStruct((ROWS, D_MODEL), F32), jax.ShapeDtypeStruct((ROWS, D_MODEL), BF16)],
        name="out_proj",
    )(a, b, w1, w2, x, mod, nw_next.reshape(1, D_MODEL), mod_next)


M_QSCALE = M_DQK ** -0.5


def _log_sigmoid(x):
    return jnp.minimum(x, 0.0) - jnp.log(1.0 + jnp.exp(-jnp.abs(x)))


def _mlstm_dir(q_ref, k_ref, v_ref, g_ref, gb_ref, h_ref, c_ref, n_ref, m_ref, reverse):
    T = M_CHUNK
    base = 2 * M_HEADS if reverse else 0
    g = g_ref[...] + gb_ref[...]
    glf = _log_sigmoid(g)
    r = lax.broadcasted_iota(jnp.int32, (T, T), 0)
    c = lax.broadcasted_iota(jnp.int32, (T, T), 1)
    causal = (c >= r) if reverse else (c <= r)
    tri = jnp.where(causal, 1.0, 0.0).astype(BF16)
    p_hi, p_mid, p_lo = _split3(glf)
    bcum = _dot(tri, p_hi) + _dot(tri, p_mid) + _dot(tri, p_lo)
    g_t = g.T
    b_t = bcum.T
    end = 0 if reverse else T - 1
    for h in range(M_HEADS):
        ci = base + h
        cf = base + M_HEADS + h
        st = (M_HEADS if reverse else 0) + h
        ig_col = g[:, ci:ci + 1]
        ig_row = g_t[ci:ci + 1, :]
        b_col = bcum[:, cf:cf + 1]
        b_row = b_t[cf:cf + 1, :]
        b_last = b_col[end:end + 1, :]
        m_prev = m_ref[st:st + 1, 0:1]
        a_col = b_last - b_col + ig_col
        m_new = jnp.maximum(b_last + m_prev, jnp.max(a_col, axis=0, keepdims=True))
        decay = jnp.exp(b_last + m_prev - m_new)
        w_col = jnp.exp(a_col - m_new)
        dlog = jnp.where(causal, b_col - b_row + ig_row, -jnp.inf)
        inter = b_col + m_prev
        m_t = jnp.maximum(inter, jnp.max(dlog, axis=1, keepdims=True))
        iw = jnp.exp(inter - m_t)
        dm = jnp.exp(dlog - m_t)

        qf = q_ref[:, h * M_DQK:(h + 1) * M_DQK] * M_QSCALE
        kf = k_ref[:, h * M_DQK:(h + 1) * M_DQK]
        q = qf.astype(BF16)
        k = kf.astype(BF16)
        v = v_ref[:, h * M_DV:(h + 1) * M_DV].astype(BF16)
        c_state = c_ref[st]
        n_state = n_ref[st:st + 1, :]

        s = _dot_nt(q, k) * dm
        num = iw * _dot_nt(q, c_state.astype(BF16)) + _dot(s.astype(BF16), v)
        den = iw * jnp.sum(qf * n_state, axis=1, keepdims=True) + jnp.sum(s, axis=1, keepdims=True)
        h_ref[:, h * M_DV:(h + 1) * M_DV] = num / jnp.maximum(jnp.abs(den), jnp.exp(-m_t))

        wk = w_col * kf
        c_ref[st] = decay * c_state + _dot_tn(v, wk.astype(BF16))
        n_ref[st:st + 1, :] = decay * n_state + jnp.sum(wk, axis=0, keepdims=True)
        m_ref[st:st + 1, :] = jnp.broadcast_to(m_new, (1, LANES))


def _mlstm_kernel(qf_ref, kf_ref, vf_ref, gf_ref, qb_ref, kb_ref, vb_ref, gb_ref, bias_ref,
                  hf_ref, hb_ref, c_ref, n_ref, m_ref):
    @pl.when(pl.program_id(0) == 0)
    def _():
        c_ref[...] = jnp.zeros_like(c_ref)
        n_ref[...] = jnp.zeros_like(n_ref)
        m_ref[...] = jnp.zeros_like(m_ref)

    _mlstm_dir(qf_ref, kf_ref, vf_ref, gf_ref, bias_ref, hf_ref, c_ref, n_ref, m_ref, reverse=False)
    _mlstm_dir(qb_ref, kb_ref, vb_ref, gb_ref, bias_ref, hb_ref, c_ref, n_ref, m_ref, reverse=True)


def _mlstm(u, gate_b):
    fwd = lambda k: jnp.where(k == 0, N_LAT_BLK, k - 1)
    bwd = lambda k: jnp.where(k == 0, N_LAT_BLK, N_LAT_BLK - k)
    mg_blk = OD_COL_MG // LANES

    def specs(rmap):
        return [pl.BlockSpec((M_CHUNK, M_QK_W), lambda k: (rmap(k), 0)),
                pl.BlockSpec((M_CHUNK, M_QK_W), lambda k: (rmap(k), 1)),
                pl.BlockSpec((M_CHUNK, M_V_W), lambda k: (rmap(k), 1)),
                pl.BlockSpec((M_CHUNK, LANES), lambda k: (rmap(k), mg_blk))]

    bias = jnp.concatenate([gate_b.astype(F32), jnp.zeros((LANES - 4 * M_HEADS,), F32)]).reshape(1, LANES)
    return pl.pallas_call(
        _mlstm_kernel,
        grid=(N_BLK,),
        in_specs=specs(fwd) + specs(bwd) + [pl.BlockSpec((1, LANES), lambda k: (0, 0))],
        out_specs=[pl.BlockSpec((M_CHUNK, M_V_W), lambda k: (fwd(k), 0)),
                   pl.BlockSpec((M_CHUNK, M_V_W), lambda k: (bwd(k), 0))],
        out_shape=[jax.ShapeDtypeStruct((ROWS, M_V_W), F32), jax.ShapeDtypeStruct((ROWS, M_V_W), F32)],
        scratch_shapes=[pltpu.VMEM((2 * M_HEADS, M_DV, M_DQK), F32),
                        pltpu.VMEM((2 * M_HEADS, M_DQK), F32),
                        pltpu.VMEM((2 * M_HEADS, LANES), F32)],
        compiler_params=pltpu.CompilerParams(dimension_semantics=("arbitrary",)),
        name="mlstm_scan",
    )(u, u, u, u, u, u, u, u, bias)


def _mlstm_post_kernel(hf_ref, hb_ref, o_ref, z_ref, nw_ref, y_ref):
    for h in range(M_HEADS):
        sl = slice(h * M_DV, (h + 1) * M_DV)
        hn = _rms(hf_ref[:, sl] + hb_ref[:, sl]) * nw_ref[:, sl]
        y_ref[:, sl] = (hn * _sigmoid(o_ref[:, sl]) * _silu(z_ref[:, sl])).astype(BF16)


def _mlstm_post(hf, hb, u, head_norm):
    row = lambda i: (i, 0)
    return pl.pallas_call(
        _mlstm_post_kernel,
        grid=(N_BLK,),
        in_specs=[pl.BlockSpec((ROW_BLK, M_V_W), row), pl.BlockSpec((ROW_BLK, M_V_W), row),
                  pl.BlockSpec((ROW_BLK, M_V_W), lambda i: (i, OD_COL_MO // M_V_W)),
                  pl.BlockSpec((ROW_BLK, M_V_W), lambda i: (i, OD_COL_MZ // M_V_W)),
                  pl.BlockSpec((1, M_V_W), lambda i: (0, 0))],
        out_specs=pl.BlockSpec((ROW_BLK, M_V_W), row),
        out_shape=jax.ShapeDtypeStruct((ROWS, M_V_W), BF16),
        name="mlstm_post",
    )(hf, hb, u, u, head_norm.reshape(1, M_V_W))


W_NLAT = SEQ // WINDOW
W_NBLK = ROWS // WINDOW
W_QSCALE = (A_HD ** -0.5) * LOG2E
W_KEYS = 3 * WINDOW + CTX_LEN


def _win_kernel(sink_ref, q_ref, qr_ref, az_ref, kvp_ref, kvc_ref, kvn_ref, kvx_ref,
                cosq_ref, sinq_ref, cosp_ref, sinp_ref, cosn_ref, sinn_ref, o_ref):
    bi = pl.program_id(0)
    is_lat = bi < W_NLAT
    cosq = cosq_ref[...]
    sinq = sinq_ref[...]
    lane = lax.broadcasted_iota(jnp.int32, (WINDOW, LANES), 1)
    lane_k = lax.broadcasted_iota(jnp.int32, (W_KEYS, LANES), 1)

    qi = lax.broadcasted_iota(jnp.int32, (WINDOW, WINDOW), 0)
    kj = lax.broadcasted_iota(jnp.int32, (WINDOW, WINDOW), 1)
    ok_prev = jnp.logical_and(jnp.logical_and(is_lat, bi >= 1), kj >= qi)
    ok_cur = jnp.logical_and(is_lat, kj >= 0)
    ok_next = jnp.logical_and(jnp.logical_and(is_lat, bi <= W_NLAT - 2), kj <= qi)
    ok_ctx = lax.broadcasted_iota(jnp.int32, (WINDOW, CTX_LEN), 1) >= 0
    valid = jnp.concatenate([ok_prev, ok_cur, ok_next, ok_ctx], axis=1)

    def roped(kv_ref, cos, sin, g):
        return kv_ref[:, g * LANES:(g + 1) * LANES] * cos + kv_ref[:, (2 + g) * LANES:(3 + g) * LANES] * sin

    for g in range(A_KV_HEADS):
        kk = jnp.concatenate([roped(kvp_ref, cosp_ref[...], sinp_ref[...], g),
                              roped(kvc_ref, cosq, sinq, g),
                              roped(kvn_ref, cosn_ref[...], sinn_ref[...], g),
                              kvx_ref[:, g * LANES:(g + 1) * LANES]], axis=0)
        vv = jnp.concatenate([kvp_ref[:, (4 + g) * LANES:(5 + g) * LANES],
                              kvc_ref[:, (4 + g) * LANES:(5 + g) * LANES],
                              kvn_ref[:, (4 + g) * LANES:(5 + g) * LANES],
                              kvx_ref[:, (4 + g) * LANES:(5 + g) * LANES]], axis=0)
        halves = [(lane_k < A_HD), (lane_k >= A_HD)]
        k_half = [jnp.where(hm, kk, 0.0).astype(BF16) for hm in halves]
        v_half = [jnp.where(hm, vv, 0.0).astype(BF16) for hm in halves]
        for p in range(A_HEADS // A_KV_HEADS // 2):
            col = (g * 4 + p) * LANES
            qp = ((q_ref[:, col:col + LANES] * cosq + qr_ref[:, col:col + LANES] * sinq) * W_QSCALE).astype(BF16)
            out = jnp.zeros((WINDOW, LANES), F32)
            for par in range(2):
                head = g * 8 + 2 * p + par
                snk = sink_ref[head] * LOG2E
                s = jnp.where(valid, _dot_nt(qp, k_half[par]), -jnp.inf)
                m = jnp.maximum(jnp.max(s, axis=1, keepdims=True), snk)
                e = jnp.exp2(s - m)
                den = jnp.sum(e, axis=1, keepdims=True) + jnp.exp2(snk - m)
                out = out + _dot(e.astype(BF16), v_half[par]) / den
            o_ref[:, col:col + LANES] = (out * _silu(az_ref[:, col:col + LANES])).astype(BF16)


def _win_attn(u, sink, cos, sin):
    kvb = OD_COL_KV // OD_KV_W
    cur = lambda b: jnp.minimum(b, W_NLAT - 1)
    prv = lambda b: jnp.clip(b - 1, 0, W_NLAT - 1)
    nxt = lambda b: jnp.clip(b + 1, 0, W_NLAT - 1)
    tab = lambda f: pl.BlockSpec((WINDOW, LANES), lambda b: (f(b), 0))
    return pl.pallas_call(
        _win_kernel,
        grid=(W_NBLK,),
        in_specs=[pl.BlockSpec(memory_space=pltpu.SMEM),
                  pl.BlockSpec((WINDOW, A_W), lambda b: (b, OD_COL_AQ // A_W)),
                  pl.BlockSpec((WINDOW, A_W), lambda b: (b, OD_COL_AQR // A_W)),
                  pl.BlockSpec((WINDOW, A_W), lambda b: (b, OD_COL_AZ // A_W)),
                  pl.BlockSpec((WINDOW, OD_KV_W), lambda b: (prv(b), kvb)),
                  pl.BlockSpec((WINDOW, OD_KV_W), lambda b: (cur(b), kvb)),
                  pl.BlockSpec((WINDOW, OD_KV_W), lambda b: (nxt(b), kvb)),
                  pl.BlockSpec((CTX_LEN, OD_KV_W), lambda b: (SEQ // CTX_LEN, kvb)),
                  tab(lambda b: b), tab(lambda b: b), tab(prv), tab(prv), tab(nxt), tab(nxt)],
        out_specs=pl.BlockSpec((WINDOW, A_W), lambda b: (b, 0)),
        out_shape=jax.ShapeDtypeStruct((ROWS, A_W), BF16),
        name="window_attention",
    )(sink, u, u, u, u, u, u, u, cos, sin, cos, sin, cos, sin)


def kernel(x, c, ctx, c_ctx, ada_w, ada_b, norm_w, ev_w_in, ev_q_norm, ev_kv_norm, ev_w_uq, ev_w_ukv,
           ev_w_pool, ev_pool_scale, ev_w_out, od_w_in, od_gate_b, od_head_norm, od_sink, od_w_out, final_norm):
    cos, sin = _rope_tables()
    mod = _ada(c, c_ctx, ada_w, ada_b)

    xs, h = _prenorm(x[0], ctx[0], norm_w[0], mod[0])
    out = None
    for i in range(DEPTH):
        j = i // 2
        if i % 2 == 0:
            u = _in_proj(h, _ev_layout(ev_w_in[j]), EV_TN)
            wq = ev_w_uq[j].reshape(Q_RANK, MLA_HEADS, MLA_QK)
            wq_rope = wq[:, :, MLA_NOPE:]
            q = _q_proj(u, ev_q_norm[j],
                        wq[:, :, :MLA_NOPE].reshape(Q_RANK, MLA_HEADS * MLA_NOPE).astype(BF16),
                        wq_rope.reshape(Q_RANK, MLA_HEADS * ROPE_DIM).astype(BF16),
                        _rot64(wq_rope).reshape(Q_RANK, MLA_HEADS * ROPE_DIM).astype(BF16), cos, sin)
            wkv = ev_w_ukv[j].reshape(KV_RANK, MLA_HEADS, MLA_NOPE + MLA_VDIM)
            k, v = _kv_proj(u, ev_kv_norm[j],
                            wkv[:, :, :MLA_NOPE].reshape(KV_RANK, MLA_HEADS * MLA_NOPE).astype(BF16),
                            wkv[:, :, MLA_NOPE:].reshape(KV_RANK, MLA_W).astype(BF16), cos, sin)
            a = _pool(u, ev_w_pool[j], ev_pool_scale[j])
            b = _flash(q, k, v, u)
            w_out = ev_w_out[j].astype(BF16)
            w1, w2 = w_out[:POOL_W], w_out[POOL_W:]
        else:
            u = _in_proj(h, _od_layout(od_w_in[j]), OD_TN)
            hf, hb = _mlstm(u, od_gate_b[j])
            a = _mlstm_post(hf, hb, u, od_head_norm[j])
            b = _win_attn(u, od_sink[j].astype(F32), cos, sin)
            w_out = od_w_out[j].astype(BF16)
            w1, w2 = w_out[:M_V_W], w_out[M_V_W:]
        if i < DEPTH - 1:
            xs, h = _out_proj(a, b, w1, w2, xs, mod[i], norm_w[i + 1], mod[i + 1])
        else:
            out = _out_proj(a, b, w1, w2, xs, mod[i], final_norm, None)
    return out[None]
```

```python
import functools
import math

import numpy as np
import jax
import jax.numpy as jnp
from jax import lax
from jax.experimental import pallas as pl
from jax.experimental.pallas import tpu as pltpu

F32 = jnp.float32
BF16 = jnp.bfloat16

D_MODEL = 2048
SEQ = 8192
CTX_LEN = 256
ROWS = SEQ + CTX_LEN
DEPTH = 4
GRID_W = 64
EPS = 1e-6
ROPE_THETA = 10000.0
ROPE_DIM = 64
LOG2E = math.log2(math.e)

LANES = 128
ROW_BLK = 256
N_LAT_BLK = SEQ // ROW_BLK
N_BLK = ROWS // ROW_BLK
MM_ROWS = 768

POOL_WINDOWS = (2, 4, 8, 16)
POOL_W = 512
POOL_GC = 128
POOL_HALO = 16

MLA_HEADS = 12
MLA_NOPE = 128
MLA_VDIM = 128
MLA_QK = MLA_NOPE + ROPE_DIM
MLA_W = MLA_HEADS * MLA_VDIM
Q_RANK = 512
KV_RANK = 512
HEAD_PAD = 256

M_HEADS = 4
M_DQK = 128
M_DV = 256
M_QK_W = 512
M_V_W = 1024
M_CHUNK = 256

A_HEADS = 16
A_KV_HEADS = 2
A_HD = 64
A_W = 1024
WINDOW = 128

NEG_BIG = -1e30

EV_N = 3840
EV_COL_CQ = 1024
EV_COL_CKV = 1536
EV_COL_AGATE = 2048
EV_COL_KR = 3584
EV_TN = 768

OD_N = 8064
OD_COL_MO = 2048
OD_COL_MZ = 3072
OD_COL_AQ = 4096
OD_COL_AQR = 5120
OD_COL_AZ = 6144
OD_COL_KV = 7168
OD_KV_W = 896
OD_COL_MG = 7936
OD_TN = 896


def _rot64(w):
    return jnp.concatenate([-w[..., 16:32], w[..., 0:16], -w[..., 48:64], w[..., 32:48]], axis=-1)


def _ev_layout(w):
    w = w.astype(BF16)
    kr = w[:, 2048:2112]
    krr = _rot64(kr)
    return jnp.concatenate([w[:, :2048], w[:, 2112:3648], kr, kr, krr, krr], axis=1)


def _od_layout(w):
    d = w.shape[0]
    w = w.astype(BF16)
    mg, mz, aq, ak, av, az = 3072, 3088, 4112, 5136, 5264, 5392
    aq_w = w[:, aq:aq + A_W]
    aq_r = _rot64(aq_w.reshape(d, A_HEADS, A_HD)).reshape(d, A_W)
    k = w[:, ak:ak + 2 * A_HD].reshape(d, A_KV_HEADS, A_HD)
    kr = _rot64(k)
    v = w[:, av:av + 2 * A_HD].reshape(d, A_KV_HEADS, A_HD)
    dup = lambda t: jnp.concatenate([t[:, 0], t[:, 0], t[:, 1], t[:, 1]], axis=1)
    pad = jnp.zeros((d, LANES - 4 * M_HEADS), w.dtype)
    return jnp.concatenate([w[:, :mg], w[:, mz:aq], aq_w, aq_r, w[:, az:az + A_W], dup(k), dup(kr), dup(v),
                            w[:, mg:mz], pad], axis=1)


def _rope_tables():
    t = jnp.arange(SEQ)
    row = (t // GRID_W).astype(F32)
    col = (t % GRID_W).astype(F32)
    half = ROPE_DIM // 2
    inv = ROPE_THETA ** (-jnp.arange(0, half, 2, dtype=F32) / half)
    ar = row[:, None] * inv
    ac = col[:, None] * inv
    cos64 = jnp.concatenate([jnp.cos(ar), jnp.cos(ar), jnp.cos(ac), jnp.cos(ac)], axis=-1)
    sin64 = jnp.concatenate([jnp.sin(ar), jnp.sin(ar), jnp.sin(ac), jnp.sin(ac)], axis=-1)
    cos = jnp.concatenate([jnp.tile(cos64, (1, 2)), jnp.ones((CTX_LEN, LANES), F32)], axis=0)
    sin = jnp.concatenate([jnp.tile(sin64, (1, 2)), jnp.zeros((CTX_LEN, LANES), F32)], axis=0)
    return cos, sin


def _dot(a, b):
    return jnp.dot(a, b, preferred_element_type=F32)


def _dot_nt(a, b):
    return lax.dot_general(a, b, (((1,), (1,)), ((), ())), preferred_element_type=F32)


def _dot_tn(a, b):
    return lax.dot_general(a, b, (((0,), (0,)), ((), ())), preferred_element_type=F32)


def _rms(x):
    return x * lax.rsqrt(jnp.mean(x * x, axis=-1, keepdims=True) + EPS)


def _modnorm(x, nw, mod):
    return (_rms(x) * nw) * (1.0 + mod[1:2, :]) + mod[0:1, :]


def _silu(x):
    return x * (1.0 / (1.0 + jnp.exp(-x)))


def _sigmoid(x):
    return 1.0 / (1.0 + jnp.exp(-x))


def _split3(x):
    hi = x.astype(BF16)
    r1 = x - hi.astype(F32)
    mid = r1.astype(BF16)
    lo = (r1 - mid.astype(F32)).astype(BF16)
    return hi, mid, lo


def _is_ctx_blk(i):
    return jnp.where(i < N_LAT_BLK, 0, 1)


def _ada_kernel(s_ref, w_ref, b_ref, o_ref):
    s = _silu(s_ref[...])
    hi = s.astype(BF16)
    lo = (s - hi.astype(F32)).astype(BF16)
    lhs = jnp.concatenate([hi, lo], axis=0)
    r = _dot(lhs, w_ref[0].astype(BF16))
    o_ref[0] = r[0:8] + r[8:16] + b_ref[0]


def _ada(c, c_ctx, ada_w, ada_b):
    tn = 1536
    s = jnp.concatenate([c.reshape(1, D_MODEL), c_ctx.reshape(1, D_MODEL), jnp.zeros((6, D_MODEL), F32)], axis=0)
    out = pl.pallas_call(
        _ada_kernel,
        grid=(DEPTH, 3 * D_MODEL // tn),
        in_specs=[pl.BlockSpec((8, D_MODEL), lambda l, j: (0, 0)),
                  pl.BlockSpec((1, D_MODEL, tn), lambda l, j: (l, 0, j)),
                  pl.BlockSpec((1, 1, tn), lambda l, j: (l, 0, j))],
        out_specs=pl.BlockSpec((1, 8, tn), lambda l, j: (l, 0, j)),
        out_shape=jax.ShapeDtypeStruct((DEPTH, 8, 3 * D_MODEL), F32),
        name="ada_mod",
    )(s, ada_w, ada_b.reshape(DEPTH, 1, 3 * D_MODEL))
    m = out[:, :2, :].reshape(DEPTH, 2, 3, D_MODEL)
    return jnp.concatenate([m, jnp.zeros((DEPTH, 2, 5, D_MODEL), F32)], axis=2)


def _prenorm_kernel(x_ref, c_ref, nw_ref, mod_ref, xo_ref, h_ref):
    i = pl.program_id(0)

    def emit(src):
        x = src[...]
        xo_ref[...] = x
        h_ref[...] = _modnorm(x, nw_ref[...], mod_ref[0]).astype(BF16)

    @pl.when(i < N_LAT_BLK)
    def _():
        emit(x_ref)

    @pl.when(i == N_LAT_BLK)
    def _():
        emit(c_ref)


def _prenorm(x, ctx, nw, mod):
    return pl.pallas_call(
        _prenorm_kernel,
        grid=(N_BLK,),
        in_specs=[pl.BlockSpec((ROW_BLK, D_MODEL), lambda i: (jnp.minimum(i, N_LAT_BLK - 1), 0)),
                  pl.BlockSpec((CTX_LEN, D_MODEL), lambda i: (0, 0)),
                  pl.BlockSpec((1, D_MODEL), lambda i: (0, 0)),
                  pl.BlockSpec((1, 8, D_MODEL), lambda i: (_is_ctx_blk(i), 0, 0))],
        out_specs=[pl.BlockSpec((ROW_BLK, D_MODEL), lambda i: (i, 0)),
                   pl.BlockSpec((ROW_BLK, D_MODEL), lambda i: (i, 0))],
        out_shape=[jax.ShapeDtypeStruct((ROWS, D_MODEL), F32),
                   jax.ShapeDtypeStruct((ROWS, D_MODEL), BF16)],
        name="prenorm",
    )(x, ctx, nw.reshape(1, D_MODEL), mod)


def _mm_kernel(h_ref, w_ref, o_ref):
    o_ref[...] = _dot(h_ref[...], w_ref[...])


def _in_proj(h, w, tn):
    n = w.shape[1]
    return pl.pallas_call(
        _mm_kernel,
        grid=(ROWS // MM_ROWS, n // tn),
        in_specs=[pl.BlockSpec((MM_ROWS, D_MODEL), lambda i, j: (i, 0)),
                  pl.BlockSpec((D_MODEL, tn), lambda i, j: (0, j))],
        out_specs=pl.BlockSpec((MM_ROWS, tn), lambda i, j: (i, j)),
        out_shape=jax.ShapeDtypeStruct((ROWS, n), F32),
        name="in_proj",
    )(h, w)


Q_SCALE = (MLA_QK ** -0.5) * LOG2E


def _qproj_kernel(cq_ref, nw_ref, wn_ref, wr_ref, wrr_ref, cos_ref, sin_ref, q_ref):
    xn = (_rms(cq_ref[...]) * nw_ref[...]).astype(BF16)
    qn = _dot(xn, wn_ref[...]) * Q_SCALE
    cos = cos_ref[...]
    sin = sin_ref[...]
    for h in range(MLA_HEADS):
        q_ref[:, h * HEAD_PAD:h * HEAD_PAD + LANES] = qn[:, h * LANES:(h + 1) * LANES].astype(BF16)
    qr = _dot(xn, wr_ref[...])
    qrr = _dot(xn, wrr_ref[...])
    for p in range(MLA_HEADS // 2):
        sl = slice(p * LANES, (p + 1) * LANES)
        rp = ((qr[:, sl] * cos + qrr[:, sl] * sin) * Q_SCALE).astype(BF16)
        for h in (2 * p, 2 * p + 1):
            q_ref[:, h * HEAD_PAD + LANES:(h + 1) * HEAD_PAD] = rp


def _q_proj(u, nw, wn, wr, wrr, cos, sin):
    t = MM_ROWS
    return pl.pallas_call(
        _qproj_kernel,
        grid=(ROWS // t,),
        in_specs=[pl.BlockSpec((t, Q_RANK), lambda i: (i, EV_COL_CQ // Q_RANK)),
                  pl.BlockSpec((1, Q_RANK), lambda i: (0, 0)),
                  pl.BlockSpec(wn.shape, lambda i: (0, 0)),
                  pl.BlockSpec(wr.shape, lambda i: (0, 0)),
                  pl.BlockSpec(wrr.shape, lambda i: (0, 0)),
                  pl.BlockSpec((t, LANES), lambda i: (i, 0)),
                  pl.BlockSpec((t, LANES), lambda i: (i, 0))],
        out_specs=pl.BlockSpec((t, MLA_HEADS * HEAD_PAD), lambda i: (i, 0)),
        out_shape=jax.ShapeDtypeStruct((ROWS, MLA_HEADS * HEAD_PAD), BF16),
        name="mla_q_proj",
    )(u, nw.reshape(1, Q_RANK), wn, wr, wrr, cos, sin)


def _kvproj_kernel(ckv_ref, kr_ref, nw_ref, wk_ref, wv_ref, cos_ref, sin_ref, k_ref, v_ref):
    xn = (_rms(ckv_ref[...]) * nw_ref[...]).astype(BF16)
    kn = _dot(xn, wk_ref[...])
    vv = _dot(xn, wv_ref[...])
    kr = kr_ref[...]
    kr2 = kr[:, :LANES] * cos_ref[...] + kr[:, LANES:] * sin_ref[...]
    lane = lax.broadcasted_iota(jnp.int32, kr2.shape, 1)
    k_lo = jnp.where(lane < ROPE_DIM, kr2, 0.0).astype(BF16)
    k_hi = jnp.where(lane >= ROPE_DIM, kr2, 0.0).astype(BF16)
    ones_col = jnp.where(lane == 0, 1.0, 0.0).astype(BF16)
    for h in range(MLA_HEADS):
        sl = slice(h * LANES, (h + 1) * LANES)
        k_ref[:, h * HEAD_PAD:h * HEAD_PAD + LANES] = kn[:, sl].astype(BF16)
        k_ref[:, h * HEAD_PAD + LANES:(h + 1) * HEAD_PAD] = k_lo if h % 2 == 0 else k_hi
        v_ref[:, h * HEAD_PAD:h * HEAD_PAD + LANES] = vv[:, sl].astype(BF16)
        v_ref[:, h * HEAD_PAD + LANES:(h + 1) * HEAD_PAD] = ones_col


def _kv_proj(u, nw, wk, wv, cos, sin):
    t = MM_ROWS
    wide = MLA_HEADS * HEAD_PAD
    return pl.pallas_call(
        _kvproj_kernel,
        grid=(ROWS // t,),
        in_specs=[pl.BlockSpec((t, KV_RANK), lambda i: (i, EV_COL_CKV // KV_RANK)),
                  pl.BlockSpec((t, 2 * LANES), lambda i: (i, EV_COL_KR // (2 * LANES))),
                  pl.BlockSpec((1, KV_RANK), lambda i: (0, 0)),
                  pl.BlockSpec(wk.shape, lambda i: (0, 0)),
                  pl.BlockSpec(wv.shape, lambda i: (0, 0)),
                  pl.BlockSpec((t, LANES), lambda i: (i, 0)),
                  pl.BlockSpec((t, LANES), lambda i: (i, 0))],
        out_specs=[pl.BlockSpec((t, wide), lambda i: (i, 0)),
                   pl.BlockSpec((t, wide), lambda i: (i, 0))],
        out_shape=[jax.ShapeDtypeStruct((ROWS, wide), BF16),
                   jax.ShapeDtypeStruct((ROWS, wide), BF16)],
        name="mla_kv_proj",
    )(u, u, nw.reshape(1, KV_RANK), wk, wv, cos, sin)


FL_TQ = 768
FL_TK = 768
FL_LAST = ROWS // FL_TQ - 1
FL_LAT_IN_LAST = SEQ - FL_LAST * FL_TQ


def _attend(q, k_ref, v_ref, sa_ref, sb_ref, k_start, n_chunks, tk):
    rows = q.shape[0]
    sa = sa_ref.at[0:rows, 0:tk]
    sb = sb_ref.at[0:rows, 0:tk]

    def scores_into(dst, j):
        off = pl.multiple_of(k_start + j * tk, LANES)
        dst[...] = _dot_nt(q, k_ref[pl.ds(off, tk), :])

    def update(src, j, m_prev, acc):
        off = pl.multiple_of(k_start + j * tk, LANES)
        v = v_ref[pl.ds(off, tk), :]
        s = src[...]
        m_new = jnp.maximum(m_prev, jnp.max(s, axis=1, keepdims=True))
        alpha = jnp.exp2(m_prev - m_new)
        p = jnp.exp2(s - m_new).astype(BF16)
        return m_new, alpha * acc + _dot(p, v)

    def body(i, carry):
        m, acc = carry
        scores_into(sb, 2 * i + 1)
        m, acc = update(sa, 2 * i, m, acc)
        scores_into(sa, 2 * i + 2)
        return update(sb, 2 * i + 1, m, acc)

    m = jnp.full((rows, 1), NEG_BIG, F32)
    acc = jnp.zeros((rows, HEAD_PAD), F32)
    scores_into(sa, 0)
    if n_chunks > 1:
        m, acc = lax.fori_loop(0, n_chunks // 2, body, (m, acc))
    _, acc = update(sa, n_chunks - 1, m, acc)
    return acc[:, :MLA_VDIM] / acc[:, MLA_VDIM:MLA_VDIM + 1]


def _flash_kernel(q_ref, k_ref, v_ref, g_ref, o_ref, sa_ref, sb_ref):
    i = pl.program_id(1)
    n_chunks = ROWS // FL_TK

    @pl.when(i < FL_LAST)
    def _():
        o = _attend(q_ref[...], k_ref, v_ref, sa_ref, sb_ref, 0, n_chunks, FL_TK)
        o_ref[...] = (o * _silu(g_ref[...])).astype(BF16)

    @pl.when(i == FL_LAST)
    def _():
        nl = FL_LAT_IN_LAST
        o = _attend(q_ref[0:nl, :], k_ref, v_ref, sa_ref, sb_ref, 0, n_chunks, FL_TK)
        o_ref[0:nl, :] = (o * _silu(g_ref[0:nl, :])).astype(BF16)
        oc = _attend(q_ref[nl:, :], k_ref, v_ref, sa_ref, sb_ref, SEQ, 1, CTX_LEN)
        o_ref[nl:, :] = (oc * _silu(g_ref[nl:, :])).astype(BF16)


def _flash(q, k, v, u):
    return pl.pallas_call(
        _flash_kernel,
        grid=(MLA_HEADS, ROWS // FL_TQ),
        in_specs=[pl.BlockSpec((FL_TQ, HEAD_PAD), lambda h, i: (i, h)),
                  pl.BlockSpec((ROWS, HEAD_PAD), lambda h, i: (0, h)),
                  pl.BlockSpec((ROWS, HEAD_PAD), lambda h, i: (0, h)),
                  pl.BlockSpec((FL_TQ, MLA_VDIM), lambda h, i: (i, EV_COL_AGATE // MLA_VDIM + h))],
        out_specs=pl.BlockSpec((FL_TQ, MLA_VDIM), lambda h, i: (i, h)),
        out_shape=jax.ShapeDtypeStruct((ROWS, MLA_W), BF16),
        scratch_shapes=[pltpu.VMEM((FL_TQ, FL_TK), F32), pltpu.VMEM((FL_TQ, FL_TK), F32)],
        name="mla_attention",
    )(q, k, v, u)


def _pool_kernel(prev_ref, cur_ref, next_ref, gate_ref, w_ref, sc_ref, o_ref):
    i = pl.program_id(0)
    first = jnp.logical_or(i == 0, i == N_LAT_BLK)
    last = i >= N_LAT_BLK - 1
    lo_lim = jnp.where(first, 0, -POOL_HALO)
    hi_lim = jnp.where(last, ROW_BLK - 1, ROW_BLK - 1 + POOL_HALO)
    src_rows = ROW_BLK + 2 * POOL_HALO
    t = lax.broadcasted_iota(jnp.int32, (ROW_BLK, src_rows), 0)
    j = lax.broadcasted_iota(jnp.int32, (ROW_BLK, src_rows), 1) - POOL_HALO
    t1 = lax.broadcasted_iota(jnp.int32, (ROW_BLK, 1), 0)
    for g, w in enumerate(POOL_WINDOWS):
        sl = slice(g * POOL_GC, (g + 1) * POOL_GC)
        cur = cur_ref[:, sl]
        src = jnp.concatenate([prev_ref[ROW_BLK - POOL_HALO:, sl], cur, next_ref[:POOL_HALO, sl]], axis=0)
        lo = jnp.maximum(t - w // 2, lo_lim)
        hi = jnp.minimum(t + w // 2 - 1, hi_lim)
        band = jnp.where(jnp.logical_and(j >= lo, j <= hi), 1.0, 0.0).astype(BF16)
        cnt = (jnp.minimum(t1 + w // 2 - 1, hi_lim) - jnp.maximum(t1 - w // 2, lo_lim) + 1).astype(F32)
        s_hi, s_mid, s_lo = _split3(src)
        wsum = _dot(band, s_hi) + _dot(band, s_mid) + _dot(band, s_lo)
        z = (wsum / cnt - cur).astype(BF16)
        y = _dot(z, w_ref[g]) * sc_ref[:, sl]
        o_ref[:, sl] = (y * _silu(gate_ref[:, sl])).astype(BF16)


def _pool(u, w_pool, pool_scale):
    def prev_map(i):
        return (jnp.where(i < N_LAT_BLK, jnp.maximum(i - 1, 0), N_LAT_BLK), 0)

    def next_map(i):
        return (jnp.where(i < N_LAT_BLK, jnp.minimum(i + 1, N_LAT_BLK - 1), N_LAT_BLK), 0)

    return pl.pallas_call(
        _pool_kernel,
        grid=(N_BLK,),
        in_specs=[pl.BlockSpec((ROW_BLK, POOL_W), prev_map),
                  pl.BlockSpec((ROW_BLK, POOL_W), lambda i: (i, 0)),
                  pl.BlockSpec((ROW_BLK, POOL_W), next_map),
                  pl.BlockSpec((ROW_BLK, POOL_W), lambda i: (i, 1)),
                  pl.BlockSpec((4, POOL_GC, POOL_GC), lambda i: (0, 0, 0)),
                  pl.BlockSpec((1, POOL_W), lambda i: (0, 0))],
        out_specs=pl.BlockSpec((ROW_BLK, POOL_W), lambda i: (i, 0)),
        out_shape=jax.ShapeDtypeStruct((ROWS, POOL_W), BF16),
        name="pool_mix",
    )(u, u, u, u, w_pool.astype(BF16), pool_scale.reshape(1, POOL_W))


def _outproj_residual(a_ref, b_ref, w_ref, x_ref, mod_ref, wbf_ref):
    @pl.when(pl.program_id(0) == 0)
    def _():
        wbf_ref[...] = w_ref[...].astype(BF16)

    k1 = a_ref.shape[1]
    y = _dot(a_ref[...], wbf_ref[0:k1, :]) + _dot(b_ref[...], wbf_ref[k1:, :])
    return x_ref[...] + mod_ref[0, 2:3, :] * y


def _outproj_kernel(a_ref, b_ref, w_ref, x_ref, mod_ref, nw_ref, modn_ref, xo_ref, h_ref, wbf_ref):
    xn = _outproj_residual(a_ref, b_ref, w_ref, x_ref, mod_ref, wbf_ref)
    xo_ref[...] = xn
    h_ref[...] = _modnorm(xn, nw_ref[...], modn_ref[0]).astype(BF16)


def _outproj_final_kernel(a_ref, b_ref, w_ref, x_ref, mod_ref, nw_ref, o_ref, wbf_ref):
    xn = _outproj_residual(a_ref, b_ref, w_ref, x_ref, mod_ref, wbf_ref)
    o_ref[...] = _rms(xn) * nw_ref[...]


def _out_proj(a, b, w, x, mod, nw_next, mod_next):
    k1, k2 = a.shape[1], b.shape[1]
    row = lambda i: (i, 0)
    full = lambda i: (0, 0)
    stream = lambda i: (_is_ctx_blk(i), 0, 0)
    common = [pl.BlockSpec((ROW_BLK, k1), row), pl.BlockSpec((ROW_BLK, k2), row),
              pl.BlockSpec((D_MODEL, D_MODEL), full, pipeline_mode=pl.Buffered(1)),
              pl.BlockSpec((ROW_BLK, D_MODEL), row), pl.BlockSpec((1, 8, D_MODEL), stream),
              pl.BlockSpec((1, D_MODEL), full)]
    scratch = [pltpu.VMEM((D_MODEL, D_MODEL), BF16)]
    params = pltpu.CompilerParams(dimension_semantics=("arbitrary",))
    if mod_next is None:
        return pl.pallas_call(
            _outproj_final_kernel,
            grid=(N_LAT_BLK,),
            in_specs=common,
            out_specs=pl.BlockSpec((ROW_BLK, D_MODEL), row),
            out_shape=jax.ShapeDtypeStruct((SEQ, D_MODEL), F32),
            scratch_shapes=scratch,
            compiler_params=params,
            name="out_proj_final",
        )(a, b, w, x, mod, nw_next.reshape(1, D_MODEL))
    return pl.pallas_call(
        _outproj_kernel,
        grid=(N_BLK,),
        in_specs=common + [pl.BlockSpec((1, 8, D_MODEL), stream)],
        out_specs=[pl.BlockSpec((ROW_BLK, D_MODEL), row), pl.BlockSpec((ROW_BLK, D_MODEL), row)],
        out_shape=[jax.ShapeDtypeStruct((ROWS, D_MODEL), F32), jax.ShapeDtypeStruct((ROWS, D_MODEL), BF16)],
        scratch_shapes=scratch,
        compiler_params=params,
        name="out_proj",
    )(a, b, w, x, mod, nw_next.reshape(1, D_MODEL), mod_next)


M_QSCALE = M_DQK ** -0.5


def _log_sigmoid(x):
    return jnp.minimum(x, 0.0) - jnp.log(1.0 + jnp.exp(-jnp.abs(x)))


def _mlstm_dir(q_ref, k_ref, v_ref, g_ref, gb_ref, h_ref, c_ref, n_ref, m_ref, reverse):
    T = M_CHUNK
    base = 2 * M_HEADS if reverse else 0
    g = g_ref[...] + gb_ref[...]
    glf = _log_sigmoid(g)
    r = lax.broadcasted_iota(jnp.int32, (T, T), 0)
    c = lax.broadcasted_iota(jnp.int32, (T, T), 1)
    causal = (c >= r) if reverse else (c <= r)
    tri = jnp.where(causal, 1.0, 0.0).astype(BF16)
    p_hi, p_mid, p_lo = _split3(glf)
    bcum = _dot(tri, p_hi) + _dot(tri, p_mid) + _dot(tri, p_lo)
    g_t = g.T
    b_t = bcum.T
    end = 0 if reverse else T - 1
    for h in range(M_HEADS):
        ci = base + h
        cf = base + M_HEADS + h
        st = (M_HEADS if reverse else 0) + h
        ig_col = g[:, ci:ci + 1]
        ig_row = g_t[ci:ci + 1, :]
        b_col = bcum[:, cf:cf + 1]
        b_row = b_t[cf:cf + 1, :]
        b_last = b_col[end:end + 1, :]
        m_prev = m_ref[st:st + 1, 0:1]
        a_col = b_last - b_col + ig_col
        m_new = jnp.maximum(b_last + m_prev, jnp.max(a_col, axis=0, keepdims=True))
        decay = jnp.exp(b_last + m_prev - m_new)
        w_col = jnp.exp(a_col - m_new)
        dlog = jnp.where(causal, b_col - b_row + ig_row, -jnp.inf)
        inter = b_col + m_prev
        m_t = jnp.maximum(inter, jnp.max(dlog, axis=1, keepdims=True))
        iw = jnp.exp(inter - m_t)
        dm = jnp.exp(dlog - m_t)

        qf = q_ref[:, h * M_DQK:(h + 1) * M_DQK] * M_QSCALE
        kf = k_ref[:, h * M_DQK:(h + 1) * M_DQK]
        q = qf.astype(BF16)
        k = kf.astype(BF16)
        v = v_ref[:, h * M_DV:(h + 1) * M_DV].astype(BF16)
        c_state = c_ref[st]
        n_state = n_ref[st:st + 1, :]

        s = _dot_nt(q, k) * dm
        num = iw * _dot_nt(q, c_state.astype(BF16)) + _dot(s.astype(BF16), v)
        den = iw * jnp.sum(qf * n_state, axis=1, keepdims=True) + jnp.sum(s, axis=1, keepdims=True)
        h_ref[:, h * M_DV:(h + 1) * M_DV] = num / jnp.maximum(jnp.abs(den), jnp.exp(-m_t))

        wk = w_col * kf
        c_ref[st] = decay * c_state + _dot_tn(v, wk.astype(BF16))
        n_ref[st:st + 1, :] = decay * n_state + jnp.sum(wk, axis=0, keepdims=True)
        m_ref[st:st + 1, :] = jnp.broadcast_to(m_new, (1, LANES))


def _mlstm_kernel(qf_ref, kf_ref, vf_ref, gf_ref, qb_ref, kb_ref, vb_ref, gb_ref, bias_ref,
                  hf_ref, hb_ref, c_ref, n_ref, m_ref):
    @pl.when(pl.program_id(0) == 0)
    def _():
        c_ref[...] = jnp.zeros_like(c_ref)
        n_ref[...] = jnp.zeros_like(n_ref)
        m_ref[...] = jnp.zeros_like(m_ref)

    _mlstm_dir(qf_ref, kf_ref, vf_ref, gf_ref, bias_ref, hf_ref, c_ref, n_ref, m_ref, reverse=False)
    _mlstm_dir(qb_ref, kb_ref, vb_ref, gb_ref, bias_ref, hb_ref, c_ref, n_ref, m_ref, reverse=True)


def _mlstm(u, gate_b):
    fwd = lambda k: jnp.where(k == 0, N_LAT_BLK, k - 1)
    bwd = lambda k: jnp.where(k == 0, N_LAT_BLK, N_LAT_BLK - k)
    mg_blk = OD_COL_MG // LANES

    def specs(rmap):
        return [pl.BlockSpec((M_CHUNK, M_QK_W), lambda k: (rmap(k), 0)),
                pl.BlockSpec((M_CHUNK, M_QK_W), lambda k: (rmap(k), 1)),
                pl.BlockSpec((M_CHUNK, M_V_W), lambda k: (rmap(k), 1)),
                pl.BlockSpec((M_CHUNK, LANES), lambda k: (rmap(k), mg_blk))]

    bias = jnp.concatenate([gate_b.astype(F32), jnp.zeros((LANES - 4 * M_HEADS,), F32)]).reshape(1, LANES)
    return pl.pallas_call(
        _mlstm_kernel,
        grid=(N_BLK,),
        in_specs=specs(fwd) + specs(bwd) + [pl.BlockSpec((1, LANES), lambda k: (0, 0))],
        out_specs=[pl.BlockSpec((M_CHUNK, M_V_W), lambda k: (fwd(k), 0)),
                   pl.BlockSpec((M_CHUNK, M_V_W), lambda k: (bwd(k), 0))],
        out_shape=[jax.ShapeDtypeStruct((ROWS, M_V_W), F32), jax.ShapeDtypeStruct((ROWS, M_V_W), F32)],
        scratch_shapes=[pltpu.VMEM((2 * M_HEADS, M_DV, M_DQK), F32),
                        pltpu.VMEM((2 * M_HEADS, M_DQK), F32),
                        pltpu.VMEM((2 * M_HEADS, LANES), F32)],
        compiler_params=pltpu.CompilerParams(dimension_semantics=("arbitrary",)),
        name="mlstm_scan",
    )(u, u, u, u, u, u, u, u, bias)


def _mlstm_post_kernel(hf_ref, hb_ref, o_ref, z_ref, nw_ref, y_ref):
    for h in range(M_HEADS):
        sl = slice(h * M_DV, (h + 1) * M_DV)
        hn = _rms(hf_ref[:, sl] + hb_ref[:, sl]) * nw_ref[:, sl]
        y_ref[:, sl] = (hn * _sigmoid(o_ref[:, sl]) * _silu(z_ref[:, sl])).astype(BF16)


def _mlstm_post(hf, hb, u, head_norm):
    row = lambda i: (i, 0)
    return pl.pallas_call(
        _mlstm_post_kernel,
        grid=(N_BLK,),
        in_specs=[pl.BlockSpec((ROW_BLK, M_V_W), row), pl.BlockSpec((ROW_BLK, M_V_W), row),
                  pl.BlockSpec((ROW_BLK, M_V_W), lambda i: (i, OD_COL_MO // M_V_W)),
                  pl.BlockSpec((ROW_BLK, M_V_W), lambda i: (i, OD_COL_MZ // M_V_W)),
                  pl.BlockSpec((1, M_V_W), lambda i: (0, 0))],
        out_specs=pl.BlockSpec((ROW_BLK, M_V_W), row),
        out_shape=jax.ShapeDtypeStruct((ROWS, M_V_W), BF16),
        name="mlstm_post",
    )(hf, hb, u, u, head_norm.reshape(1, M_V_W))


W_NLAT = SEQ // WINDOW
W_NBLK = ROWS // WINDOW
W_QSCALE = (A_HD ** -0.5) * LOG2E
W_KEYS = 3 * WINDOW + CTX_LEN
W_PAIRS = A_HEADS // A_KV_HEADS // 2


def _win_kernel(sink_ref, q_ref, qr_ref, az_ref, kvp_ref, kvc_ref, kvn_ref, kvx_ref,
                cosq_ref, sinq_ref, cosp_ref, sinp_ref, cosn_ref, sinn_ref, o_ref):
    bi = pl.program_id(0)
    is_lat = bi < W_NLAT
    cosq = cosq_ref[...]
    sinq = sinq_ref[...]
    lane = lax.broadcasted_iota(jnp.int32, (WINDOW, LANES), 1)
    lane_k = lax.broadcasted_iota(jnp.int32, (W_KEYS, LANES), 1)

    qi = lax.broadcasted_iota(jnp.int32, (WINDOW, WINDOW), 0)
    kj = lax.broadcasted_iota(jnp.int32, (WINDOW, WINDOW), 1)
    ok_prev = jnp.logical_and(jnp.logical_and(is_lat, bi >= 1), kj >= qi)
    ok_cur = jnp.logical_and(is_lat, kj >= 0)
    ok_next = jnp.logical_and(jnp.logical_and(is_lat, bi <= W_NLAT - 2), kj <= qi)
    ok_ctx = lax.broadcasted_iota(jnp.int32, (WINDOW, CTX_LEN), 1) >= 0
    valid = jnp.concatenate([ok_prev, ok_cur, ok_next, ok_ctx], axis=1)
    valid4 = jnp.concatenate([valid] * W_PAIRS, axis=0)
    row4 = lax.broadcasted_iota(jnp.int32, (W_PAIRS * WINDOW, 1), 0)

    def roped(kv_ref, cos, sin, g):
        return kv_ref[:, g * LANES:(g + 1) * LANES] * cos + kv_ref[:, (2 + g) * LANES:(3 + g) * LANES] * sin

    for g in range(A_KV_HEADS):
        kk = jnp.concatenate([roped(kvp_ref, cosp_ref[...], sinp_ref[...], g),
                              roped(kvc_ref, cosq, sinq, g),
                              roped(kvn_ref, cosn_ref[...], sinn_ref[...], g),
                              kvx_ref[:, g * LANES:(g + 1) * LANES]], axis=0)
        vv = jnp.concatenate([kvp_ref[:, (4 + g) * LANES:(5 + g) * LANES],
                              kvc_ref[:, (4 + g) * LANES:(5 + g) * LANES],
                              kvn_ref[:, (4 + g) * LANES:(5 + g) * LANES],
                              kvx_ref[:, (4 + g) * LANES:(5 + g) * LANES]], axis=0)
        halves = [(lane_k < A_HD), (lane_k >= A_HD)]
        k_half = [jnp.where(hm, kk, 0.0).astype(BF16) for hm in halves]
        v_half = [jnp.where(hm, vv, 0.0).astype(BF16) for hm in halves]
        cols = [(g * W_PAIRS + p) * LANES for p in range(W_PAIRS)]
        qs = jnp.concatenate(
            [((q_ref[:, c:c + LANES] * cosq + qr_ref[:, c:c + LANES] * sinq) * W_QSCALE).astype(BF16) for c in cols],
            axis=0)
        out = jnp.zeros((W_PAIRS * WINDOW, LANES), F32)
        for par in range(2):
            snk = jnp.full((W_PAIRS * WINDOW, 1), sink_ref[g * 8 + par] * LOG2E, F32)
            for p in range(1, W_PAIRS):
                snk = jnp.where(row4 >= p * WINDOW, sink_ref[g * 8 + 2 * p + par] * LOG2E, snk)
            s = jnp.where(valid4, _dot_nt(qs, k_half[par]), -jnp.inf)
            m = jnp.maximum(jnp.max(s, axis=1, keepdims=True), snk)
            e = jnp.exp2(s - m)
            den = jnp.sum(e, axis=1, keepdims=True) + jnp.exp2(snk - m)
            out = out + _dot(e.astype(BF16), v_half[par]) / den
        for p, c in enumerate(cols):
            o_ref[:, c:c + LANES] = (out[p * WINDOW:(p + 1) * WINDOW] * _silu(az_ref[:, c:c + LANES])).astype(BF16)


def _win_attn(u, sink, cos, sin):
    kvb = OD_COL_KV // OD_KV_W
    cur = lambda b: jnp.minimum(b, W_NLAT - 1)
    prv = lambda b: jnp.clip(b - 1, 0, W_NLAT - 1)
    nxt = lambda b: jnp.clip(b + 1, 0, W_NLAT - 1)
    tab = lambda f: pl.BlockSpec((WINDOW, LANES), lambda b: (f(b), 0))
    return pl.pallas_call(
        _win_kernel,
        grid=(W_NBLK,),
        in_specs=[pl.BlockSpec(memory_space=pltpu.SMEM),
                  pl.BlockSpec((WINDOW, A_W), lambda b: (b, OD_COL_AQ // A_W)),
                  pl.BlockSpec((WINDOW, A_W), lambda b: (b, OD_COL_AQR // A_W)),
                  pl.BlockSpec((WINDOW, A_W), lambda b: (b, OD_COL_AZ // A_W)),
                  pl.BlockSpec((WINDOW, OD_KV_W), lambda b: (prv(b), kvb)),
                  pl.BlockSpec((WINDOW, OD_KV_W), lambda b: (cur(b), kvb)),
                  pl.BlockSpec((WINDOW, OD_KV_W), lambda b: (nxt(b), kvb)),
                  pl.BlockSpec((CTX_LEN, OD_KV_W), lambda b: (SEQ // CTX_LEN, kvb)),
                  tab(lambda b: b), tab(lambda b: b), tab(prv), tab(prv), tab(nxt), tab(nxt)],
        out_specs=pl.BlockSpec((WINDOW, A_W), lambda b: (b, 0)),
        out_shape=jax.ShapeDtypeStruct((ROWS, A_W), BF16),
        name="window_attention",
    )(sink, u, u, u, u, u, u, u, cos, sin, cos, sin, cos, sin)


def kernel(x, c, ctx, c_ctx, ada_w, ada_b, norm_w, ev_w_in, ev_q_norm, ev_kv_norm, ev_w_uq, ev_w_ukv,
           ev_w_pool, ev_pool_scale, ev_w_out, od_w_in, od_gate_b, od_head_norm, od_sink, od_w_out, final_norm):
    cos, sin = _rope_tables()
    mod = _ada(c, c_ctx, ada_w, ada_b)

    xs, h = _prenorm(x[0], ctx[0], norm_w[0], mod[0])
    out = None
    for i in range(DEPTH):
        j = i // 2
        if i % 2 == 0:
            u = _in_proj(h, _ev_layout(ev_w_in[j]), EV_TN)
            wq = ev_w_uq[j].astype(BF16).reshape(Q_RANK, MLA_HEADS, MLA_QK)
            wq_rope = wq[:, :, MLA_NOPE:]
            q = _q_proj(u, ev_q_norm[j],
                        wq[:, :, :MLA_NOPE].reshape(Q_RANK, MLA_HEADS * MLA_NOPE),
                        wq_rope.reshape(Q_RANK, MLA_HEADS * ROPE_DIM),
                        _rot64(wq_rope).reshape(Q_RANK, MLA_HEADS * ROPE_DIM), cos, sin)
            wkv = ev_w_ukv[j].astype(BF16).reshape(KV_RANK, MLA_HEADS, MLA_NOPE + MLA_VDIM)
            k, v = _kv_proj(u, ev_kv_norm[j],
                            wkv[:, :, :MLA_NOPE].reshape(KV_RANK, MLA_HEADS * MLA_NOPE),
                            wkv[:, :, MLA_NOPE:].reshape(KV_RANK, MLA_W), cos, sin)
            a = _pool(u, ev_w_pool[j], ev_pool_scale[j])
            b = _flash(q, k, v, u)
            w_out = ev_w_out[j]
        else:
            u = _in_proj(h, _od_layout(od_w_in[j]), OD_TN)
            hf, hb = _mlstm(u, od_gate_b[j])
            a = _mlstm_post(hf, hb, u, od_head_norm[j])
            b = _win_attn(u, od_sink[j].astype(F32), cos, sin)
            w_out = od_w_out[j]
        if i < DEPTH - 1:
            xs, h = _out_proj(a, b, w_out, xs, mod[i], norm_w[i + 1], mod[i + 1])
        else:
            out = _out_proj(a, b, w_out, xs, mod[i], final_norm, None)
    return out[None]
```

```python
import functools
import math

import numpy as np
import jax
import jax.numpy as jnp
from jax import lax
from jax.experimental import pallas as pl
from jax.experimental.pallas import tpu as pltpu

F32 = jnp.float32
BF16 = jnp.bfloat16

D_MODEL = 2048
SEQ = 8192
CTX_LEN = 256
ROWS = SEQ + CTX_LEN
DEPTH = 4
GRID_W = 64
EPS = 1e-6
ROPE_THETA = 10000.0
ROPE_DIM = 64
LOG2E = math.log2(math.e)

LANES = 128
ROW_BLK = 256
N_LAT_BLK = SEQ // ROW_BLK
N_BLK = ROWS // ROW_BLK
MM_ROWS = 768

POOL_WINDOWS = (2, 4, 8, 16)
POOL_W = 512
POOL_GC = 128
POOL_HALO = 16

MLA_HEADS = 12
MLA_NOPE = 128
MLA_VDIM = 128
MLA_QK = MLA_NOPE + ROPE_DIM
MLA_W = MLA_HEADS * MLA_VDIM
Q_RANK = 512
KV_RANK = 512
HEAD_PAD = 256

M_HEADS = 4
M_DQK = 128
M_DV = 256
M_QK_W = 512
M_V_W = 1024
M_CHUNK = 256

A_HEADS = 16
A_KV_HEADS = 2
A_HD = 64
A_W = 1024
WINDOW = 128

NEG_BIG = -1e30

EV_N = 3840
EV_COL_CQ = 1024
EV_COL_CKV = 1536
EV_COL_AGATE = 2048
EV_COL_KR = 3584
EV_TN = 768

OD_N = 6528
OD_COL_MO = 2048
OD_COL_MZ = 3072
OD_COL_AQ = 4096
OD_COL_AZ = 5120
OD_COL_KV = 6144
OD_KV_W = 384
OD_COL_MG = 6400
OD_TN = 2176


def _ev_layout(w):
    w = w.astype(BF16)
    pad = jnp.zeros(w.shape[:2] + (EV_N - 3648,), BF16)
    return jnp.concatenate([w[..., :2048], w[..., 2112:], w[..., 2048:2112], pad], axis=-1)


def _od_layout(w):
    w = w.astype(BF16)
    mg, mz, ak, az = 3072, 3088, 5136, 5392
    pad = jnp.zeros(w.shape[:2] + (OD_N - 6416,), BF16)
    return jnp.concatenate([w[..., :mg], w[..., mz:ak], w[..., az:], w[..., ak:az], w[..., mg:mz], pad], axis=-1)


def _rope_tables():
    t = jnp.arange(SEQ)
    row = (t // GRID_W).astype(F32)
    col = (t % GRID_W).astype(F32)
    half = ROPE_DIM // 2
    inv = ROPE_THETA ** (-jnp.arange(0, half, 2, dtype=F32) / half)
    ar = row[:, None] * inv
    ac = col[:, None] * inv
    zero = jnp.zeros_like(ar)
    cos64 = jnp.concatenate([jnp.cos(ar), jnp.cos(ar), jnp.cos(ac), jnp.cos(ac)], axis=-1)
    sa64 = jnp.concatenate([-jnp.sin(ar), zero, -jnp.sin(ac), zero], axis=-1)
    sb64 = jnp.concatenate([zero, jnp.sin(ar), zero, jnp.sin(ac)], axis=-1)
    ctx_zero = jnp.zeros((CTX_LEN, LANES), F32)
    cos = jnp.concatenate([jnp.tile(cos64, (1, 2)), jnp.ones((CTX_LEN, LANES), F32)], axis=0)
    sin_a = jnp.concatenate([jnp.tile(sa64, (1, 2)), ctx_zero], axis=0)
    sin_b = jnp.concatenate([jnp.tile(sb64, (1, 2)), ctx_zero], axis=0)
    return jnp.stack([cos, sin_a, sin_b])


def _rope(x, cos, sin_a, sin_b):
    up = pltpu.roll(x, LANES - ROPE_DIM // 4, 1)
    down = pltpu.roll(x, ROPE_DIM // 4, 1)
    return x * cos + up * sin_a + down * sin_b


def _dot(a, b):
    return jnp.dot(a, b, preferred_element_type=F32)


def _dot_nt(a, b):
    return lax.dot_general(a, b, (((1,), (1,)), ((), ())), preferred_element_type=F32)


def _dot_tn(a, b):
    return lax.dot_general(a, b, (((0,), (0,)), ((), ())), preferred_element_type=F32)


def _rms(x):
    return x * lax.rsqrt(jnp.mean(x * x, axis=-1, keepdims=True) + EPS)


def _modnorm(x, nw, mod):
    return (_rms(x) * nw) * (1.0 + mod[1:2, :]) + mod[0:1, :]


def _silu(x):
    return x * (1.0 / (1.0 + jnp.exp(-x)))


def _sigmoid(x):
    return 1.0 / (1.0 + jnp.exp(-x))


def _split3(x):
    hi = x.astype(BF16)
    r1 = x - hi.astype(F32)
    mid = r1.astype(BF16)
    lo = (r1 - mid.astype(F32)).astype(BF16)
    return hi, mid, lo


def _is_ctx_blk(i):
    return jnp.where(i < N_LAT_BLK, 0, 1)


def _ada_kernel(s_ref, w_ref, b_ref, o_ref):
    s = _silu(s_ref[...])
    hi = s.astype(BF16)
    lo = (s - hi.astype(F32)).astype(BF16)
    lhs = jnp.concatenate([hi, lo], axis=0)
    r = _dot(lhs, w_ref[0].astype(BF16))
    o_ref[0] = r[0:8] + r[8:16] + b_ref[0]


def _ada(c, c_ctx, ada_w, ada_b):
    tn = 1536
    s = jnp.concatenate([c.reshape(1, D_MODEL), c_ctx.reshape(1, D_MODEL), jnp.zeros((6, D_MODEL), F32)], axis=0)
    out = pl.pallas_call(
        _ada_kernel,
        grid=(DEPTH, 3 * D_MODEL // tn),
        in_specs=[pl.BlockSpec((8, D_MODEL), lambda l, j: (0, 0)),
                  pl.BlockSpec((1, D_MODEL, tn), lambda l, j: (l, 0, j)),
                  pl.BlockSpec((1, 1, tn), lambda l, j: (l, 0, j))],
        out_specs=pl.BlockSpec((1, 8, tn), lambda l, j: (l, 0, j)),
        out_shape=jax.ShapeDtypeStruct((DEPTH, 8, 3 * D_MODEL), F32),
        name="ada_mod",
    )(s, ada_w, ada_b.reshape(DEPTH, 1, 3 * D_MODEL))
    m = out[:, :2, :].reshape(DEPTH, 2, 3, D_MODEL)
    return jnp.concatenate([m, jnp.zeros((DEPTH, 2, 5, D_MODEL), F32)], axis=2)


def _prenorm_kernel(x_ref, c_ref, nw_ref, mod_ref, xo_ref, h_ref):
    i = pl.program_id(0)

    def emit(src):
        x = src[...]
        xo_ref[...] = x
        h_ref[...] = _modnorm(x, nw_ref[...], mod_ref[0]).astype(BF16)

    @pl.when(i < N_LAT_BLK)
    def _():
        emit(x_ref)

    @pl.when(i == N_LAT_BLK)
    def _():
        emit(c_ref)


def _prenorm(x, ctx, nw, mod):
    return pl.pallas_call(
        _prenorm_kernel,
        grid=(N_BLK,),
        in_specs=[pl.BlockSpec((ROW_BLK, D_MODEL), lambda i: (jnp.minimum(i, N_LAT_BLK - 1), 0)),
                  pl.BlockSpec((CTX_LEN, D_MODEL), lambda i: (0, 0)),
                  pl.BlockSpec((1, D_MODEL), lambda i: (0, 0)),
                  pl.BlockSpec((1, 8, D_MODEL), lambda i: (_is_ctx_blk(i), 0, 0))],
        out_specs=[pl.BlockSpec((ROW_BLK, D_MODEL), lambda i: (i, 0)),
                   pl.BlockSpec((ROW_BLK, D_MODEL), lambda i: (i, 0))],
        out_shape=[jax.ShapeDtypeStruct((ROWS, D_MODEL), F32),
                   jax.ShapeDtypeStruct((ROWS, D_MODEL), BF16)],
        name="prenorm",
    )(x, ctx, nw.reshape(1, D_MODEL), mod)


def _mm_kernel(h_ref, w_ref, o_ref):
    o_ref[...] = _dot(h_ref[...], w_ref[...])


def _in_proj(h, w, layer, tn):
    n = w.shape[2]
    return pl.pallas_call(
        _mm_kernel,
        grid=(ROWS // MM_ROWS, n // tn),
        in_specs=[pl.BlockSpec((MM_ROWS, D_MODEL), lambda i, j: (i, 0)),
                  pl.BlockSpec((None, D_MODEL, tn), lambda i, j: (layer, 0, j))],
        out_specs=pl.BlockSpec((MM_ROWS, tn), lambda i, j: (i, j)),
        out_shape=jax.ShapeDtypeStruct((ROWS, n), F32),
        name="in_proj",
    )(h, w)


Q_SCALE = (MLA_QK ** -0.5) * LOG2E


def _qproj_kernel(cq_ref, nw_ref, wn_ref, wr_ref, tab_ref, q_ref):
    xn = (_rms(cq_ref[...]) * nw_ref[...]).astype(BF16)
    qn = _dot(xn, wn_ref[...]) * Q_SCALE
    for h in range(MLA_HEADS):
        q_ref[:, h * HEAD_PAD:h * HEAD_PAD + LANES] = qn[:, h * LANES:(h + 1) * LANES].astype(BF16)
    qr = _dot(xn, wr_ref[...])
    for p in range(MLA_HEADS // 2):
        rp = (_rope(qr[:, p * LANES:(p + 1) * LANES], tab_ref[0], tab_ref[1], tab_ref[2]) * Q_SCALE).astype(BF16)
        for h in (2 * p, 2 * p + 1):
            q_ref[:, h * HEAD_PAD + LANES:(h + 1) * HEAD_PAD] = rp


def _layer_spec(w, layer):
    return pl.BlockSpec((None,) + w.shape[1:], lambda i: (layer, 0, 0))


def _q_proj(u, layer, nw, wn, wr, tabs):
    t = MM_ROWS
    return pl.pallas_call(
        _qproj_kernel,
        grid=(ROWS // t,),
        in_specs=[pl.BlockSpec((t, Q_RANK), lambda i: (i, EV_COL_CQ // Q_RANK)),
                  _layer_spec(nw, layer), _layer_spec(wn, layer), _layer_spec(wr, layer),
                  pl.BlockSpec((3, t, LANES), lambda i: (0, i, 0))],
        out_specs=pl.BlockSpec((t, MLA_HEADS * HEAD_PAD), lambda i: (i, 0)),
        out_shape=jax.ShapeDtypeStruct((ROWS, MLA_HEADS * HEAD_PAD), BF16),
        name="mla_q_proj",
    )(u, nw, wn, wr, tabs)


def _kvproj_kernel(ckv_ref, kr_ref, nw_ref, wk_ref, wv_ref, tab_ref, k_ref, v_ref):
    xn = (_rms(ckv_ref[...]) * nw_ref[...]).astype(BF16)
    kn = _dot(xn, wk_ref[...])
    vv = _dot(xn, wv_ref[...])
    kr2 = _rope(kr_ref[...], tab_ref[0], tab_ref[1], tab_ref[2])
    lane = lax.broadcasted_iota(jnp.int32, kr2.shape, 1)
    kr_lo = jnp.where(lane < ROPE_DIM, kr2, 0.0)
    k_lo = kr_lo.astype(BF16)
    k_hi = pltpu.roll(kr_lo, ROPE_DIM, 1).astype(BF16)
    ones_col = jnp.where(lane == 0, 1.0, 0.0).astype(BF16)
    for h in range(MLA_HEADS):
        sl = slice(h * LANES, (h + 1) * LANES)
        k_ref[:, h * HEAD_PAD:h * HEAD_PAD + LANES] = kn[:, sl].astype(BF16)
        k_ref[:, h * HEAD_PAD + LANES:(h + 1) * HEAD_PAD] = k_lo if h % 2 == 0 else k_hi
        v_ref[:, h * HEAD_PAD:h * HEAD_PAD + LANES] = vv[:, sl].astype(BF16)
        v_ref[:, h * HEAD_PAD + LANES:(h + 1) * HEAD_PAD] = ones_col


def _kv_proj(u, layer, nw, wk, wv, tabs):
    t = MM_ROWS
    wide = MLA_HEADS * HEAD_PAD
    return pl.pallas_call(
        _kvproj_kernel,
        grid=(ROWS // t,),
        in_specs=[pl.BlockSpec((t, KV_RANK), lambda i: (i, EV_COL_CKV // KV_RANK)),
                  pl.BlockSpec((t, LANES), lambda i: (i, EV_COL_KR // LANES)),
                  _layer_spec(nw, layer), _layer_spec(wk, layer), _layer_spec(wv, layer),
                  pl.BlockSpec((3, t, LANES), lambda i: (0, i, 0))],
        out_specs=[pl.BlockSpec((t, wide), lambda i: (i, 0)),
                   pl.BlockSpec((t, wide), lambda i: (i, 0))],
        out_shape=[jax.ShapeDtypeStruct((ROWS, wide), BF16),
                   jax.ShapeDtypeStruct((ROWS, wide), BF16)],
        name="mla_kv_proj",
    )(u, u, nw, wk, wv, tabs)


FL_TQ = 768
FL_TK = 768
FL_LAST = ROWS // FL_TQ - 1
FL_LAT_IN_LAST = SEQ - FL_LAST * FL_TQ


def _attend(q, k_ref, v_ref, sa_ref, sb_ref, k_start, n_chunks, tk):
    rows = q.shape[0]
    sa = sa_ref.at[0:rows, 0:tk]
    sb = sb_ref.at[0:rows, 0:tk]

    def scores_into(dst, j):
        off = pl.multiple_of(k_start + j * tk, LANES)
        dst[...] = _dot_nt(q, k_ref[pl.ds(off, tk), :])

    def update(src, j, m_prev, acc):
        off = pl.multiple_of(k_start + j * tk, LANES)
        v = v_ref[pl.ds(off, tk), :]
        s = src[...]
        m_new = jnp.maximum(m_prev, jnp.max(s, axis=1, keepdims=True))
        alpha = jnp.exp2(m_prev - m_new)
        p = jnp.exp2(s - m_new).astype(BF16)
        return m_new, alpha * acc + _dot(p, v)

    m = jnp.full((rows, 1), NEG_BIG, F32)
    acc = jnp.zeros((rows, HEAD_PAD), F32)
    bufs = (sa, sb)
    scores_into(sa, 0)
    for j in range(n_chunks):
        if j + 1 < n_chunks:
            scores_into(bufs[(j + 1) % 2], j + 1)
        m, acc = update(bufs[j % 2], j, m, acc)
    return acc[:, :MLA_VDIM] / acc[:, MLA_VDIM:MLA_VDIM + 1]


def _flash_kernel(q_ref, k_ref, v_ref, g_ref, o_ref, sa_ref, sb_ref):
    i = pl.program_id(1)
    n_chunks = ROWS // FL_TK

    @pl.when(i < FL_LAST)
    def _():
        o = _attend(q_ref[...], k_ref, v_ref, sa_ref, sb_ref, 0, n_chunks, FL_TK)
        o_ref[...] = (o * _silu(g_ref[...])).astype(BF16)

    @pl.when(i == FL_LAST)
    def _():
        nl = FL_LAT_IN_LAST
        o = _attend(q_ref[0:nl, :], k_ref, v_ref, sa_ref, sb_ref, 0, n_chunks, FL_TK)
        o_ref[0:nl, :] = (o * _silu(g_ref[0:nl, :])).astype(BF16)
        oc = _attend(q_ref[nl:, :], k_ref, v_ref, sa_ref, sb_ref, SEQ, 1, CTX_LEN)
        o_ref[nl:, :] = (oc * _silu(g_ref[nl:, :])).astype(BF16)


def _flash(q, k, v, u):
    return pl.pallas_call(
        _flash_kernel,
        grid=(MLA_HEADS, ROWS // FL_TQ),
        in_specs=[pl.BlockSpec((FL_TQ, HEAD_PAD), lambda h, i: (i, h)),
                  pl.BlockSpec((ROWS, HEAD_PAD), lambda h, i: (0, h)),
                  pl.BlockSpec((ROWS, HEAD_PAD), lambda h, i: (0, h)),
                  pl.BlockSpec((FL_TQ, MLA_VDIM), lambda h, i: (i, EV_COL_AGATE // MLA_VDIM + h))],
        out_specs=pl.BlockSpec((FL_TQ, MLA_VDIM), lambda h, i: (i, h)),
        out_shape=jax.ShapeDtypeStruct((ROWS, MLA_W), BF16),
        scratch_shapes=[pltpu.VMEM((FL_TQ, FL_TK), F32), pltpu.VMEM((FL_TQ, FL_TK), F32)],
        name="mla_attention",
    )(q, k, v, u)


def _pool_kernel(prev_ref, cur_ref, next_ref, gate_ref, w_ref, sc_ref, o_ref):
    i = pl.program_id(0)
    first = jnp.logical_or(i == 0, i == N_LAT_BLK)
    last = i >= N_LAT_BLK - 1
    lo_lim = jnp.where(first, 0, -POOL_HALO)
    hi_lim = jnp.where(last, ROW_BLK - 1, ROW_BLK - 1 + POOL_HALO)
    src_rows = ROW_BLK + 2 * POOL_HALO
    t = lax.broadcasted_iota(jnp.int32, (ROW_BLK, src_rows), 0)
    j = lax.broadcasted_iota(jnp.int32, (ROW_BLK, src_rows), 1) - POOL_HALO
    t1 = lax.broadcasted_iota(jnp.int32, (ROW_BLK, 1), 0)
    for g, w in enumerate(POOL_WINDOWS):
        sl = slice(g * POOL_GC, (g + 1) * POOL_GC)
        cur = cur_ref[:, sl]
        src = jnp.concatenate([prev_ref[ROW_BLK - POOL_HALO:, sl], cur, next_ref[:POOL_HALO, sl]], axis=0)
        lo = jnp.maximum(t - w // 2, lo_lim)
        hi = jnp.minimum(t + w // 2 - 1, hi_lim)
        band = jnp.where(jnp.logical_and(j >= lo, j <= hi), 1.0, 0.0).astype(BF16)
        cnt = (jnp.minimum(t1 + w // 2 - 1, hi_lim) - jnp.maximum(t1 - w // 2, lo_lim) + 1).astype(F32)
        s_hi, s_mid, s_lo = _split3(src)
        wsum = _dot(band, s_hi) + _dot(band, s_mid) + _dot(band, s_lo)
        z = (wsum / cnt - cur).astype(BF16)
        y = _dot(z, w_ref[g]) * sc_ref[:, sl]
        o_ref[:, sl] = (y * _silu(gate_ref[:, sl])).astype(BF16)


def _pool(u, w_pool, pool_scale):
    def prev_map(i):
        return (jnp.where(i < N_LAT_BLK, jnp.maximum(i - 1, 0), N_LAT_BLK), 0)

    def next_map(i):
        return (jnp.where(i < N_LAT_BLK, jnp.minimum(i + 1, N_LAT_BLK - 1), N_LAT_BLK), 0)

    return pl.pallas_call(
        _pool_kernel,
        grid=(N_BLK,),
        in_specs=[pl.BlockSpec((ROW_BLK, POOL_W), prev_map),
                  pl.BlockSpec((ROW_BLK, POOL_W), lambda i: (i, 0)),
                  pl.BlockSpec((ROW_BLK, POOL_W), next_map),
                  pl.BlockSpec((ROW_BLK, POOL_W), lambda i: (i, 1)),
                  pl.BlockSpec((4, POOL_GC, POOL_GC), lambda i: (0, 0, 0)),
                  pl.BlockSpec((1, POOL_W), lambda i: (0, 0))],
        out_specs=pl.BlockSpec((ROW_BLK, POOL_W), lambda i: (i, 0)),
        out_shape=jax.ShapeDtypeStruct((ROWS, POOL_W), BF16),
        name="pool_mix",
    )(u, u, u, u, w_pool.astype(BF16), pool_scale.reshape(1, POOL_W))


def _outproj_residual(a_ref, b_ref, w_ref, x_ref, mod_ref, wbf_ref):
    @pl.when(pl.program_id(0) == 0)
    def _():
        wbf_ref[...] = w_ref[...].astype(BF16)

    k1 = a_ref.shape[1]
    y = _dot(a_ref[...], wbf_ref[0:k1, :]) + _dot(b_ref[...], wbf_ref[k1:, :])
    return x_ref[...] + mod_ref[0, 2:3, :] * y


def _outproj_kernel(a_ref, b_ref, w_ref, x_ref, mod_ref, nw_ref, modn_ref, xo_ref, h_ref, wbf_ref):
    xn = _outproj_residual(a_ref, b_ref, w_ref, x_ref, mod_ref, wbf_ref)
    xo_ref[...] = xn
    h_ref[...] = _modnorm(xn, nw_ref[...], modn_ref[0]).astype(BF16)


def _outproj_final_kernel(a_ref, b_ref, w_ref, x_ref, mod_ref, nw_ref, o_ref, wbf_ref):
    xn = _outproj_residual(a_ref, b_ref, w_ref, x_ref, mod_ref, wbf_ref)
    o_ref[...] = _rms(xn) * nw_ref[...]


def _out_proj(a, b, w, layer, x, mod, nw_next, mod_next):
    k1, k2 = a.shape[1], b.shape[1]
    row = lambda i: (i, 0)
    full = lambda i: (0, 0)
    stream = lambda i: (_is_ctx_blk(i), 0, 0)
    common = [pl.BlockSpec((ROW_BLK, k1), row), pl.BlockSpec((ROW_BLK, k2), row),
              pl.BlockSpec((None, D_MODEL, D_MODEL), lambda i: (layer, 0, 0), pipeline_mode=pl.Buffered(1)),
              pl.BlockSpec((ROW_BLK, D_MODEL), row), pl.BlockSpec((1, 8, D_MODEL), stream),
              pl.BlockSpec((1, D_MODEL), full)]
    scratch = [pltpu.VMEM((D_MODEL, D_MODEL), BF16)]
    params = pltpu.CompilerParams(dimension_semantics=("arbitrary",))
    if mod_next is None:
        return pl.pallas_call(
            _outproj_final_kernel,
            grid=(N_LAT_BLK,),
            in_specs=common,
            out_specs=pl.BlockSpec((ROW_BLK, D_MODEL), row),
            out_shape=jax.ShapeDtypeStruct((SEQ, D_MODEL), F32),
            scratch_shapes=scratch,
            compiler_params=params,
            name="out_proj_final",
        )(a, b, w, x, mod, nw_next.reshape(1, D_MODEL))
    return pl.pallas_call(
        _outproj_kernel,
        grid=(N_BLK,),
        in_specs=common + [pl.BlockSpec((1, 8, D_MODEL), stream)],
        out_specs=[pl.BlockSpec((ROW_BLK, D_MODEL), row), pl.BlockSpec((ROW_BLK, D_MODEL), row)],
        out_shape=[jax.ShapeDtypeStruct((ROWS, D_MODEL), F32), jax.ShapeDtypeStruct((ROWS, D_MODEL), BF16)],
        scratch_shapes=scratch,
        compiler_params=params,
        name="out_proj",
    )(a, b, w, x, mod, nw_next.reshape(1, D_MODEL), mod_next)


M_QSCALE = M_DQK ** -0.5


def _log_sigmoid(x):
    return jnp.minimum(x, 0.0) - jnp.log(1.0 + jnp.exp(-jnp.abs(x)))


def _mlstm_dir(q_ref, k_ref, v_ref, g_ref, gb_ref, h_ref, c_ref, n_ref, m_ref, reverse):
    T = M_CHUNK
    base = 2 * M_HEADS if reverse else 0
    g = g_ref[...] + gb_ref[...]
    glf = _log_sigmoid(g)
    r = lax.broadcasted_iota(jnp.int32, (T, T), 0)
    c = lax.broadcasted_iota(jnp.int32, (T, T), 1)
    causal = (c >= r) if reverse else (c <= r)
    tri = jnp.where(causal, 1.0, 0.0).astype(BF16)
    p_hi, p_mid, p_lo = _split3(glf)
    bcum = _dot(tri, p_hi) + _dot(tri, p_mid) + _dot(tri, p_lo)
    g_t = g.T
    b_t = bcum.T
    end = 0 if reverse else T - 1
    for h in range(M_HEADS):
        ci = base + h
        cf = base + M_HEADS + h
        st = (M_HEADS if reverse else 0) + h
        ig_col = g[:, ci:ci + 1]
        ig_row = g_t[ci:ci + 1, :]
        b_col = bcum[:, cf:cf + 1]
        b_row = b_t[cf:cf + 1, :]
        b_last = b_col[end:end + 1, :]
        m_prev = m_ref[st:st + 1, 0:1]
        a_col = b_last - b_col + ig_col
        m_new = jnp.maximum(b_last + m_prev, jnp.max(a_col, axis=0, keepdims=True))
        decay = jnp.exp(b_last + m_prev - m_new)
        w_col = jnp.exp(a_col - m_new)
        dlog = jnp.where(causal, b_col - b_row + ig_row, -jnp.inf)
        inter = b_col + m_prev
        m_t = jnp.maximum(inter, jnp.max(dlog, axis=1, keepdims=True))
        iw = jnp.exp(inter - m_t)
        dm = jnp.exp(dlog - m_t)

        qf = q_ref[:, h * M_DQK:(h + 1) * M_DQK] * M_QSCALE
        kf = k_ref[:, h * M_DQK:(h + 1) * M_DQK]
        q = qf.astype(BF16)
        k = kf.astype(BF16)
        v = v_ref[:, h * M_DV:(h + 1) * M_DV].astype(BF16)
        c_state = c_ref[st]
        n_state = n_ref[st:st + 1, :]

        s = _dot_nt(q, k) * dm
        num = iw * _dot_nt(q, c_state.astype(BF16)) + _dot(s.astype(BF16), v)
        den = iw * jnp.sum(qf * n_state, axis=1, keepdims=True) + jnp.sum(s, axis=1, keepdims=True)
        h_ref[:, h * M_DV:(h + 1) * M_DV] = num / jnp.maximum(jnp.abs(den), jnp.exp(-m_t))

        wk = w_col * kf
        c_ref[st] = decay * c_state + _dot_tn(v, wk.astype(BF16))
        n_ref[st:st + 1, :] = decay * n_state + jnp.sum(wk, axis=0, keepdims=True)
        m_ref[st:st + 1, :] = jnp.broadcast_to(m_new, (1, LANES))


def _mlstm_kernel(qf_ref, kf_ref, vf_ref, gf_ref, qb_ref, kb_ref, vb_ref, gb_ref, bias_ref,
                  hf_ref, hb_ref, c_ref, n_ref, m_ref):
    @pl.when(pl.program_id(0) == 0)
    def _():
        c_ref[...] = jnp.zeros_like(c_ref)
        n_ref[...] = jnp.zeros_like(n_ref)
        m_ref[...] = jnp.zeros_like(m_ref)

    _mlstm_dir(qf_ref, kf_ref, vf_ref, gf_ref, bias_ref, hf_ref, c_ref, n_ref, m_ref, reverse=False)
    _mlstm_dir(qb_ref, kb_ref, vb_ref, gb_ref, bias_ref, hb_ref, c_ref, n_ref, m_ref, reverse=True)


def _mlstm(u, gate_b):
    fwd = lambda k: jnp.where(k == 0, N_LAT_BLK, k - 1)
    bwd = lambda k: jnp.where(k == 0, N_LAT_BLK, N_LAT_BLK - k)
    mg_blk = OD_COL_MG // LANES

    def specs(rmap):
        return [pl.BlockSpec((M_CHUNK, M_QK_W), lambda k: (rmap(k), 0)),
                pl.BlockSpec((M_CHUNK, M_QK_W), lambda k: (rmap(k), 1)),
                pl.BlockSpec((M_CHUNK, M_V_W), lambda k: (rmap(k), 1)),
                pl.BlockSpec((M_CHUNK, LANES), lambda k: (rmap(k), mg_blk))]

    bias = jnp.concatenate([gate_b.astype(F32), jnp.zeros((LANES - 4 * M_HEADS,), F32)]).reshape(1, LANES)
    return pl.pallas_call(
        _mlstm_kernel,
        grid=(N_BLK,),
        in_specs=specs(fwd) + specs(bwd) + [pl.BlockSpec((1, LANES), lambda k: (0, 0))],
        out_specs=[pl.BlockSpec((M_CHUNK, M_V_W), lambda k: (fwd(k), 0)),
                   pl.BlockSpec((M_CHUNK, M_V_W), lambda k: (bwd(k), 0))],
        out_shape=[jax.ShapeDtypeStruct((ROWS, M_V_W), F32), jax.ShapeDtypeStruct((ROWS, M_V_W), F32)],
        scratch_shapes=[pltpu.VMEM((2 * M_HEADS, M_DV, M_DQK), F32),
                        pltpu.VMEM((2 * M_HEADS, M_DQK), F32),
                        pltpu.VMEM((2 * M_HEADS, LANES), F32)],
        compiler_params=pltpu.CompilerParams(dimension_semantics=("arbitrary",)),
        name="mlstm_scan",
    )(u, u, u, u, u, u, u, u, bias)


def _mlstm_post_kernel(hf_ref, hb_ref, o_ref, z_ref, nw_ref, y_ref):
    for h in range(M_HEADS):
        sl = slice(h * M_DV, (h + 1) * M_DV)
        hn = _rms(hf_ref[:, sl] + hb_ref[:, sl]) * nw_ref[:, sl]
        y_ref[:, sl] = (hn * _sigmoid(o_ref[:, sl]) * _silu(z_ref[:, sl])).astype(BF16)


def _mlstm_post(hf, hb, u, head_norm):
    row = lambda i: (i, 0)
    return pl.pallas_call(
        _mlstm_post_kernel,
        grid=(N_BLK,),
        in_specs=[pl.BlockSpec((ROW_BLK, M_V_W), row), pl.BlockSpec((ROW_BLK, M_V_W), row),
                  pl.BlockSpec((ROW_BLK, M_V_W), lambda i: (i, OD_COL_MO // M_V_W)),
                  pl.BlockSpec((ROW_BLK, M_V_W), lambda i: (i, OD_COL_MZ // M_V_W)),
                  pl.BlockSpec((1, M_V_W), lambda i: (0, 0))],
        out_specs=pl.BlockSpec((ROW_BLK, M_V_W), row),
        out_shape=jax.ShapeDtypeStruct((ROWS, M_V_W), BF16),
        name="mlstm_post",
    )(hf, hb, u, u, head_norm.reshape(1, M_V_W))


W_NLAT = SEQ // WINDOW
W_NBLK = ROWS // WINDOW
W_QSCALE = (A_HD ** -0.5) * LOG2E
W_KEYS = 3 * WINDOW + CTX_LEN
W_PAIRS = A_HEADS // A_KV_HEADS // 2


def _win_kernel(sink_ref, q_ref, az_ref, kvp_ref, kvc_ref, kvn_ref, kvx_ref,
                tq_ref, tp_ref, tn_ref, o_ref):
    bi = pl.program_id(0)
    is_lat = bi < W_NLAT
    tq = (tq_ref[0], tq_ref[1], tq_ref[2])
    lane_k = lax.broadcasted_iota(jnp.int32, (W_KEYS, LANES), 1)

    qi = lax.broadcasted_iota(jnp.int32, (WINDOW, WINDOW), 0)
    kj = lax.broadcasted_iota(jnp.int32, (WINDOW, WINDOW), 1)
    ok_prev = jnp.logical_and(jnp.logical_and(is_lat, bi >= 1), kj >= qi)
    ok_cur = jnp.logical_and(is_lat, kj >= 0)
    ok_next = jnp.logical_and(jnp.logical_and(is_lat, bi <= W_NLAT - 2), kj <= qi)
    ok_ctx = lax.broadcasted_iota(jnp.int32, (WINDOW, CTX_LEN), 1) >= 0
    valid = jnp.concatenate([ok_prev, ok_cur, ok_next, ok_ctx], axis=1)
    valid4 = jnp.concatenate([valid] * W_PAIRS, axis=0)
    row4 = lax.broadcasted_iota(jnp.int32, (W_PAIRS * WINDOW, 1), 0)

    kk = jnp.concatenate([_rope(kvp_ref[:, 0:LANES], tp_ref[0], tp_ref[1], tp_ref[2]),
                          _rope(kvc_ref[:, 0:LANES], *tq),
                          _rope(kvn_ref[:, 0:LANES], tn_ref[0], tn_ref[1], tn_ref[2]),
                          kvx_ref[:, 0:LANES]], axis=0)
    vv = jnp.concatenate([kvp_ref[:, LANES:2 * LANES], kvc_ref[:, LANES:2 * LANES],
                          kvn_ref[:, LANES:2 * LANES], kvx_ref[:, LANES:2 * LANES]], axis=0)
    kk_sw = pltpu.roll(kk, A_HD, 1)
    vv_sw = pltpu.roll(vv, A_HD, 1)
    low = lane_k < A_HD

    for g in range(A_KV_HEADS):
        k_src = (kk, kk_sw) if g == 0 else (kk_sw, kk)
        v_src = (vv, vv_sw) if g == 0 else (vv_sw, vv)
        k_half = [jnp.where(low, k_src[0], 0.0).astype(BF16), jnp.where(low, 0.0, k_src[1]).astype(BF16)]
        v_half = [jnp.where(low, v_src[0], 0.0).astype(BF16), jnp.where(low, 0.0, v_src[1]).astype(BF16)]
        cols = [(g * W_PAIRS + p) * LANES for p in range(W_PAIRS)]
        qs = jnp.concatenate([(_rope(q_ref[:, c:c + LANES], *tq) * W_QSCALE).astype(BF16) for c in cols],
                             axis=0)
        out = jnp.zeros((W_PAIRS * WINDOW, LANES), F32)
        for par in range(2):
            snk = jnp.full((W_PAIRS * WINDOW, 1), sink_ref[g * 8 + par] * LOG2E, F32)
            for p in range(1, W_PAIRS):
                snk = jnp.where(row4 >= p * WINDOW, sink_ref[g * 8 + 2 * p + par] * LOG2E, snk)
            s = jnp.where(valid4, _dot_nt(qs, k_half[par]), -jnp.inf)
            m = jnp.maximum(jnp.max(s, axis=1, keepdims=True), snk)
            e = jnp.exp2(s - m)
            den = jnp.sum(e, axis=1, keepdims=True) + jnp.exp2(snk - m)
            out = out + _dot(e.astype(BF16), v_half[par]) / den
        for p, c in enumerate(cols):
            o_ref[:, c:c + LANES] = (out[p * WINDOW:(p + 1) * WINDOW] * _silu(az_ref[:, c:c + LANES])).astype(BF16)


def _win_attn(u, sink, tabs):
    kvb = OD_COL_KV // OD_KV_W
    cur = lambda b: jnp.minimum(b, W_NLAT - 1)
    prv = lambda b: jnp.clip(b - 1, 0, W_NLAT - 1)
    nxt = lambda b: jnp.clip(b + 1, 0, W_NLAT - 1)
    tab = lambda f: pl.BlockSpec((3, WINDOW, LANES), lambda b: (0, f(b), 0))
    return pl.pallas_call(
        _win_kernel,
        grid=(W_NBLK,),
        in_specs=[pl.BlockSpec(memory_space=pltpu.SMEM),
                  pl.BlockSpec((WINDOW, A_W), lambda b: (b, OD_COL_AQ // A_W)),
                  pl.BlockSpec((WINDOW, A_W), lambda b: (b, OD_COL_AZ // A_W)),
                  pl.BlockSpec((WINDOW, OD_KV_W), lambda b: (prv(b), kvb)),
                  pl.BlockSpec((WINDOW, OD_KV_W), lambda b: (cur(b), kvb)),
                  pl.BlockSpec((WINDOW, OD_KV_W), lambda b: (nxt(b), kvb)),
                  pl.BlockSpec((CTX_LEN, OD_KV_W), lambda b: (SEQ // CTX_LEN, kvb)),
                  tab(lambda b: b), tab(prv), tab(nxt)],
        out_specs=pl.BlockSpec((WINDOW, A_W), lambda b: (b, 0)),
        out_shape=jax.ShapeDtypeStruct((ROWS, A_W), BF16),
        name="window_attention",
    )(sink, u, u, u, u, u, u, tabs, tabs, tabs)


def kernel(x, c, ctx, c_ctx, ada_w, ada_b, norm_w, ev_w_in, ev_q_norm, ev_kv_norm, ev_w_uq, ev_w_ukv,
           ev_w_pool, ev_pool_scale, ev_w_out, od_w_in, od_gate_b, od_head_norm, od_sink, od_w_out, final_norm):
    tabs = _rope_tables()
    mod = _ada(c, c_ctx, ada_w, ada_b)

    n_ev = ev_w_in.shape[0]
    ev_w = _ev_layout(ev_w_in)
    od_w = _od_layout(od_w_in)
    wq = ev_w_uq.astype(BF16).reshape(n_ev, Q_RANK, MLA_HEADS, MLA_QK)
    wq_nope = wq[..., :MLA_NOPE].reshape(n_ev, Q_RANK, MLA_HEADS * MLA_NOPE)
    wq_rope = wq[..., MLA_NOPE:].reshape(n_ev, Q_RANK, MLA_HEADS * ROPE_DIM)
    wkv = ev_w_ukv.astype(BF16).reshape(n_ev, KV_RANK, MLA_HEADS, MLA_NOPE + MLA_VDIM)
    wk_nope = wkv[..., :MLA_NOPE].reshape(n_ev, KV_RANK, MLA_HEADS * MLA_NOPE)
    wv = wkv[..., MLA_NOPE:].reshape(n_ev, KV_RANK, MLA_W)
    q_norm = ev_q_norm.reshape(n_ev, 1, Q_RANK)
    kv_norm = ev_kv_norm.reshape(n_ev, 1, KV_RANK)

    xs, h = _prenorm(x[0], ctx[0], norm_w[0], mod[0])
    out = None
    for i in range(DEPTH):
        j = i // 2
        if i % 2 == 0:
            u = _in_proj(h, ev_w, j, EV_TN)
            q = _q_proj(u, j, q_norm, wq_nope, wq_rope, tabs)
            k, v = _kv_proj(u, j, kv_norm, wk_nope, wv, tabs)
            a = _pool(u, ev_w_pool[j], ev_pool_scale[j])
            b = _flash(q, k, v, u)
            w_out = ev_w_out
        else:
            u = _in_proj(h, od_w, j, OD_TN)
            hf, hb = _mlstm(u, od_gate_b[j])
            a = _mlstm_post(hf, hb, u, od_head_norm[j])
            b = _win_attn(u, od_sink[j].astype(F32), tabs)
            w_out = od_w_out
        if i < DEPTH - 1:
            xs, h = _out_proj(a, b, w_out, j, xs, mod[i], norm_w[i + 1], mod[i + 1])
        else:
            out = _out_proj(a, b, w_out, j, xs, mod[i], final_norm, None)
    return out[None]
```

```python
import functools
import math

import numpy as np
import jax
import jax.numpy as jnp
from jax import lax
from jax.experimental import pallas as pl
from jax.experimental.pallas import tpu as pltpu

F32 = jnp.float32
BF16 = jnp.bfloat16

D_MODEL = 2048
SEQ = 8192
CTX_LEN = 256
ROWS = SEQ + CTX_LEN
DEPTH = 4
GRID_W = 64
EPS = 1e-6
ROPE_THETA = 10000.0
ROPE_DIM = 64
LOG2E = math.log2(math.e)

LANES = 128
ROW_BLK = 256
N_LAT_BLK = SEQ // ROW_BLK
N_BLK = ROWS // ROW_BLK
MM_ROWS = 768

POOL_WINDOWS = (2, 4, 8, 16)
POOL_W = 512
POOL_GC = 128
POOL_HALO = 16

MLA_HEADS = 12
MLA_NOPE = 128
MLA_VDIM = 128
MLA_QK = MLA_NOPE + ROPE_DIM
MLA_W = MLA_HEADS * MLA_VDIM
Q_RANK = 512
KV_RANK = 512
HEAD_PAD = 256

M_HEADS = 4
M_DQK = 128
M_DV = 256
M_QK_W = 512
M_V_W = 1024
M_CHUNK = 256

A_HEADS = 16
A_KV_HEADS = 2
A_HD = 64
A_W = 1024
WINDOW = 128

NEG_BIG = -1e30

EV_N = 3840
EV_COL_CQ = 1024
EV_COL_CKV = 1536
EV_COL_AGATE = 2048
EV_COL_KR = 3584
EV_TN = 1920

OD_N = 6656
OD_COL_MO = 2048
OD_COL_MZ = 3072
OD_COL_AQ = 4096
OD_COL_KV = 5120
OD_KV_W = 256
OD_COL_AZ = 5376
OD_COL_MG = 6400
OD_TN = 1664

PREP_W = 256


def _prep_kernel(a_ref, b_ref, tail_ref, o_ref, *, n_copy, n_shift, shift, keep):
    t = pl.program_id(1)

    @pl.when(t < n_copy)
    def _():
        o_ref[...] = a_ref[...].astype(BF16)

    @pl.when(jnp.logical_and(t >= n_copy, t < n_copy + n_shift))
    def _():
        nxt = jnp.where(t == n_copy + n_shift - 1, tail_ref[...], b_ref[...])
        cat = jnp.concatenate([a_ref[...], nxt], axis=1)
        o_ref[...] = pltpu.roll(cat, cat.shape[1] - shift, 1)[:, :PREP_W].astype(BF16)

    @pl.when(t == n_copy + n_shift)
    def _():
        lane = lax.broadcasted_iota(jnp.int32, a_ref.shape, 1)
        o_ref[...] = jnp.where(lane < keep, a_ref[...], 0.0).astype(BF16)


def _prep_weight(w, n_copy, n_shift, shift):
    layers, d, n = w.shape
    n_tiles = n_copy + n_shift + 1
    n_full = n // LANES
    tail = jnp.pad(w[..., n_full * LANES:], ((0, 0), (0, 0), (0, (n_full + 1) * LANES - n)))
    a_map = lambda l, t: (l, 0, jnp.where(t == n_tiles - 1, n_copy, t))
    b_map = lambda l, t: (l, 0, jnp.minimum(2 * (t + 1), n_full - 1))
    return pl.pallas_call(
        functools.partial(_prep_kernel, n_copy=n_copy, n_shift=n_shift, shift=shift, keep=shift),
        grid=(layers, n_tiles),
        in_specs=[pl.BlockSpec((None, d, PREP_W), a_map),
                  pl.BlockSpec((None, d, LANES), b_map),
                  pl.BlockSpec((None, d, LANES), lambda l, t: (l, 0, 0))],
        out_specs=pl.BlockSpec((None, d, PREP_W), lambda l, t: (l, 0, t)),
        out_shape=jax.ShapeDtypeStruct((layers, d, n_tiles * PREP_W), BF16),
        name="weight_layout",
    )(w, w, tail)


def _ev_layout(w):
    return _prep_weight(w, n_copy=8, n_shift=6, shift=ROPE_DIM)


def _od_layout(w):
    return _prep_weight(w, n_copy=12, n_shift=13, shift=4 * M_HEADS)


def _rope_tables():
    t = jnp.arange(SEQ)
    row = (t // GRID_W).astype(F32)
    col = (t % GRID_W).astype(F32)
    half = ROPE_DIM // 2
    inv = ROPE_THETA ** (-jnp.arange(0, half, 2, dtype=F32) / half)
    ar = row[:, None] * inv
    ac = col[:, None] * inv
    zero = jnp.zeros_like(ar)
    cos64 = jnp.concatenate([jnp.cos(ar), jnp.cos(ar), jnp.cos(ac), jnp.cos(ac)], axis=-1)
    sa64 = jnp.concatenate([-jnp.sin(ar), zero, -jnp.sin(ac), zero], axis=-1)
    sb64 = jnp.concatenate([zero, jnp.sin(ar), zero, jnp.sin(ac)], axis=-1)
    ctx_zero = jnp.zeros((CTX_LEN, LANES), F32)
    cos = jnp.concatenate([jnp.tile(cos64, (1, 2)), jnp.ones((CTX_LEN, LANES), F32)], axis=0)
    sin_a = jnp.concatenate([jnp.tile(sa64, (1, 2)), ctx_zero], axis=0)
    sin_b = jnp.concatenate([jnp.tile(sb64, (1, 2)), ctx_zero], axis=0)
    return jnp.stack([cos, sin_a, sin_b])


def _rope(x, cos, sin_a, sin_b):
    up = pltpu.roll(x, LANES - ROPE_DIM // 4, 1)
    down = pltpu.roll(x, ROPE_DIM // 4, 1)
    return x * cos + up * sin_a + down * sin_b


def _dot(a, b):
    return jnp.dot(a, b, preferred_element_type=F32)


def _dot_nt(a, b):
    return lax.dot_general(a, b, (((1,), (1,)), ((), ())), preferred_element_type=F32)


def _dot_tn(a, b):
    return lax.dot_general(a, b, (((0,), (0,)), ((), ())), preferred_element_type=F32)


def _rms(x):
    return x * lax.rsqrt(jnp.mean(x * x, axis=-1, keepdims=True) + EPS)


def _modnorm(x, nw, mod):
    return (_rms(x) * nw) * (1.0 + mod[1:2, :]) + mod[0:1, :]


def _silu(x):
    return x * (1.0 / (1.0 + jnp.exp(-x)))


def _sigmoid(x):
    return 1.0 / (1.0 + jnp.exp(-x))


def _split3(x):
    hi = x.astype(BF16)
    r1 = x - hi.astype(F32)
    mid = r1.astype(BF16)
    lo = (r1 - mid.astype(F32)).astype(BF16)
    return hi, mid, lo


def _is_ctx_blk(i):
    return jnp.where(i < N_LAT_BLK, 0, 1)


def _ada_kernel(s_ref, w_ref, b_ref, o_ref):
    s = _silu(s_ref[...])
    hi = s.astype(BF16)
    lo = (s - hi.astype(F32)).astype(BF16)
    lhs = jnp.concatenate([hi, lo], axis=0)
    r = _dot(lhs, w_ref[0].astype(BF16))
    o_ref[0] = r[0:8] + r[8:16] + b_ref[0]


def _ada(c, c_ctx, ada_w, ada_b):
    tn = 1536
    s = jnp.concatenate([c.reshape(1, D_MODEL), c_ctx.reshape(1, D_MODEL), jnp.zeros((6, D_MODEL), F32)], axis=0)
    out = pl.pallas_call(
        _ada_kernel,
        grid=(DEPTH, 3 * D_MODEL // tn),
        in_specs=[pl.BlockSpec((8, D_MODEL), lambda l, j: (0, 0)),
                  pl.BlockSpec((1, D_MODEL, tn), lambda l, j: (l, 0, j)),
                  pl.BlockSpec((1, 1, tn), lambda l, j: (l, 0, j))],
        out_specs=pl.BlockSpec((1, 8, tn), lambda l, j: (l, 0, j)),
        out_shape=jax.ShapeDtypeStruct((DEPTH, 8, 3 * D_MODEL), F32),
        name="ada_mod",
    )(s, ada_w, ada_b.reshape(DEPTH, 1, 3 * D_MODEL))
    m = out[:, :2, :].reshape(DEPTH, 2, 3, D_MODEL)
    return jnp.concatenate([m, jnp.zeros((DEPTH, 2, 5, D_MODEL), F32)], axis=2)


def _prenorm_kernel(x_ref, c_ref, nw_ref, mod_ref, xo_ref, h_ref):
    i = pl.program_id(0)

    def emit(src):
        x = src[...]
        xo_ref[...] = x
        h_ref[...] = _modnorm(x, nw_ref[...], mod_ref[0]).astype(BF16)

    @pl.when(i < N_LAT_BLK)
    def _():
        emit(x_ref)

    @pl.when(i == N_LAT_BLK)
    def _():
        emit(c_ref)


def _prenorm(x, ctx, nw, mod):
    return pl.pallas_call(
        _prenorm_kernel,
        grid=(N_BLK,),
        in_specs=[pl.BlockSpec((ROW_BLK, D_MODEL), lambda i: (jnp.minimum(i, N_LAT_BLK - 1), 0)),
                  pl.BlockSpec((CTX_LEN, D_MODEL), lambda i: (0, 0)),
                  pl.BlockSpec((1, D_MODEL), lambda i: (0, 0)),
                  pl.BlockSpec((1, 8, D_MODEL), lambda i: (_is_ctx_blk(i), 0, 0))],
        out_specs=[pl.BlockSpec((ROW_BLK, D_MODEL), lambda i: (i, 0)),
                   pl.BlockSpec((ROW_BLK, D_MODEL), lambda i: (i, 0))],
        out_shape=[jax.ShapeDtypeStruct((ROWS, D_MODEL), F32),
                   jax.ShapeDtypeStruct((ROWS, D_MODEL), BF16)],
        name="prenorm",
    )(x, ctx, nw.reshape(1, D_MODEL), mod)


def _mm_kernel(h_ref, w_ref, o_ref):
    o_ref[...] = _dot(h_ref[...], w_ref[...])


def _in_proj(h, w, layer, tn):
    n = w.shape[2]
    return pl.pallas_call(
        _mm_kernel,
        grid=(ROWS // MM_ROWS, n // tn),
        in_specs=[pl.BlockSpec((MM_ROWS, D_MODEL), lambda i, j: (i, 0)),
                  pl.BlockSpec((None, D_MODEL, tn), lambda i, j: (layer, 0, j))],
        out_specs=pl.BlockSpec((MM_ROWS, tn), lambda i, j: (i, j)),
        out_shape=jax.ShapeDtypeStruct((ROWS, n), F32),
        name="in_proj",
    )(h, w)


Q_SCALE = (MLA_QK ** -0.5) * LOG2E


def _qproj_kernel(cq_ref, nw_ref, wn_ref, wr_ref, tab_ref, q_ref):
    xn = (_rms(cq_ref[...]) * nw_ref[...]).astype(BF16)
    qn = _dot(xn, wn_ref[...]) * Q_SCALE
    for h in range(MLA_HEADS):
        q_ref[:, h * HEAD_PAD:h * HEAD_PAD + LANES] = qn[:, h * LANES:(h + 1) * LANES].astype(BF16)
    qr = _dot(xn, wr_ref[...])
    for p in range(MLA_HEADS // 2):
        rp = (_rope(qr[:, p * LANES:(p + 1) * LANES], tab_ref[0], tab_ref[1], tab_ref[2]) * Q_SCALE).astype(BF16)
        for h in (2 * p, 2 * p + 1):
            q_ref[:, h * HEAD_PAD + LANES:(h + 1) * HEAD_PAD] = rp


def _layer_spec(w, layer):
    return pl.BlockSpec((None,) + w.shape[1:], lambda i: (layer, 0, 0))


def _q_proj(u, layer, nw, wn, wr, tabs):
    t = MM_ROWS
    return pl.pallas_call(
        _qproj_kernel,
        grid=(ROWS // t,),
        in_specs=[pl.BlockSpec((t, Q_RANK), lambda i: (i, EV_COL_CQ // Q_RANK)),
                  _layer_spec(nw, layer), _layer_spec(wn, layer), _layer_spec(wr, layer),
                  pl.BlockSpec((3, t, LANES), lambda i: (0, i, 0))],
        out_specs=pl.BlockSpec((t, MLA_HEADS * HEAD_PAD), lambda i: (i, 0)),
        out_shape=jax.ShapeDtypeStruct((ROWS, MLA_HEADS * HEAD_PAD), BF16),
        name="mla_q_proj",
    )(u, nw, wn, wr, tabs)


def _kvproj_kernel(ckv_ref, kr_ref, nw_ref, wk_ref, wv_ref, tab_ref, k_ref, v_ref):
    xn = (_rms(ckv_ref[...]) * nw_ref[...]).astype(BF16)
    kn = _dot(xn, wk_ref[...])
    vv = _dot(xn, wv_ref[...])
    kr2 = _rope(kr_ref[...], tab_ref[0], tab_ref[1], tab_ref[2])
    lane = lax.broadcasted_iota(jnp.int32, kr2.shape, 1)
    kr_lo = jnp.where(lane < ROPE_DIM, kr2, 0.0)
    k_lo = kr_lo.astype(BF16)
    k_hi = pltpu.roll(kr_lo, ROPE_DIM, 1).astype(BF16)
    ones_col = jnp.where(lane == 0, 1.0, 0.0).astype(BF16)
    for h in range(MLA_HEADS):
        sl = slice(h * LANES, (h + 1) * LANES)
        k_ref[:, h * HEAD_PAD:h * HEAD_PAD + LANES] = kn[:, sl].astype(BF16)
        k_ref[:, h * HEAD_PAD + LANES:(h + 1) * HEAD_PAD] = k_lo if h % 2 == 0 else k_hi
        v_ref[:, h * HEAD_PAD:h * HEAD_PAD + LANES] = vv[:, sl].astype(BF16)
        v_ref[:, h * HEAD_PAD + LANES:(h + 1) * HEAD_PAD] = ones_col


def _kv_proj(u, layer, nw, wk, wv, tabs):
    t = MM_ROWS
    wide = MLA_HEADS * HEAD_PAD
    return pl.pallas_call(
        _kvproj_kernel,
        grid=(ROWS // t,),
        in_specs=[pl.BlockSpec((t, KV_RANK), lambda i: (i, EV_COL_CKV // KV_RANK)),
                  pl.BlockSpec((t, LANES), lambda i: (i, EV_COL_KR // LANES)),
                  _layer_spec(nw, layer), _layer_spec(wk, layer), _layer_spec(wv, layer),
                  pl.BlockSpec((3, t, LANES), lambda i: (0, i, 0))],
        out_specs=[pl.BlockSpec((t, wide), lambda i: (i, 0)),
                   pl.BlockSpec((t, wide), lambda i: (i, 0))],
        out_shape=[jax.ShapeDtypeStruct((ROWS, wide), BF16),
                   jax.ShapeDtypeStruct((ROWS, wide), BF16)],
        name="mla_kv_proj",
    )(u, u, nw, wk, wv, tabs)


FL_TQ = 768
FL_TK = 768
FL_LAST = ROWS // FL_TQ - 1
FL_LAT_IN_LAST = SEQ - FL_LAST * FL_TQ


def _attend(q, k_ref, v_ref, sa_ref, sb_ref, k_start, n_chunks, tk):
    rows = q.shape[0]
    sa = sa_ref.at[0:rows, 0:tk]
    sb = sb_ref.at[0:rows, 0:tk]

    def scores_into(dst, j):
        off = pl.multiple_of(k_start + j * tk, LANES)
        dst[...] = _dot_nt(q, k_ref[pl.ds(off, tk), :])

    def update(src, j, m_prev, acc):
        off = pl.multiple_of(k_start + j * tk, LANES)
        v = v_ref[pl.ds(off, tk), :]
        s = src[...]
        m_new = jnp.maximum(m_prev, jnp.max(s, axis=1, keepdims=True))
        alpha = jnp.exp2(m_prev - m_new)
        p = jnp.exp2(s - m_new).astype(BF16)
        return m_new, alpha * acc + _dot(p, v)

    m = jnp.full((rows, 1), NEG_BIG, F32)
    acc = jnp.zeros((rows, HEAD_PAD), F32)
    bufs = (sa, sb)
    scores_into(sa, 0)
    for j in range(n_chunks):
        if j + 1 < n_chunks:
            scores_into(bufs[(j + 1) % 2], j + 1)
        m, acc = update(bufs[j % 2], j, m, acc)
    return acc[:, :MLA_VDIM] / acc[:, MLA_VDIM:MLA_VDIM + 1]


def _flash_kernel(q_ref, k_ref, v_ref, g_ref, o_ref, sa_ref, sb_ref):
    i = pl.program_id(1)
    n_chunks = ROWS // FL_TK

    @pl.when(i < FL_LAST)
    def _():
        o = _attend(q_ref[...], k_ref, v_ref, sa_ref, sb_ref, 0, n_chunks, FL_TK)
        o_ref[...] = (o * _silu(g_ref[...])).astype(BF16)

    @pl.when(i == FL_LAST)
    def _():
        nl = FL_LAT_IN_LAST
        o = _attend(q_ref[0:nl, :], k_ref, v_ref, sa_ref, sb_ref, 0, n_chunks, FL_TK)
        o_ref[0:nl, :] = (o * _silu(g_ref[0:nl, :])).astype(BF16)
        oc = _attend(q_ref[nl:, :], k_ref, v_ref, sa_ref, sb_ref, SEQ, 1, CTX_LEN)
        o_ref[nl:, :] = (oc * _silu(g_ref[nl:, :])).astype(BF16)


def _flash(q, k, v, u):
    return pl.pallas_call(
        _flash_kernel,
        grid=(MLA_HEADS, ROWS // FL_TQ),
        in_specs=[pl.BlockSpec((FL_TQ, HEAD_PAD), lambda h, i: (i, h)),
                  pl.BlockSpec((ROWS, HEAD_PAD), lambda h, i: (0, h)),
                  pl.BlockSpec((ROWS, HEAD_PAD), lambda h, i: (0, h)),
                  pl.BlockSpec((FL_TQ, MLA_VDIM), lambda h, i: (i, EV_COL_AGATE // MLA_VDIM + h))],
        out_specs=pl.BlockSpec((FL_TQ, MLA_VDIM), lambda h, i: (i, h)),
        out_shape=jax.ShapeDtypeStruct((ROWS, MLA_W), BF16),
        scratch_shapes=[pltpu.VMEM((FL_TQ, FL_TK), F32), pltpu.VMEM((FL_TQ, FL_TK), F32)],
        name="mla_attention",
    )(q, k, v, u)


def _pool_kernel(prev_ref, cur_ref, next_ref, gate_ref, w_ref, sc_ref, o_ref):
    i = pl.program_id(0)
    first = jnp.logical_or(i == 0, i == N_LAT_BLK)
    last = i >= N_LAT_BLK - 1
    lo_lim = jnp.where(first, 0, -POOL_HALO)
    hi_lim = jnp.where(last, ROW_BLK - 1, ROW_BLK - 1 + POOL_HALO)
    src_rows = ROW_BLK + 2 * POOL_HALO
    t = lax.broadcasted_iota(jnp.int32, (ROW_BLK, src_rows), 0)
    j = lax.broadcasted_iota(jnp.int32, (ROW_BLK, src_rows), 1) - POOL_HALO
    t1 = lax.broadcasted_iota(jnp.int32, (ROW_BLK, 1), 0)
    for g, w in enumerate(POOL_WINDOWS):
        sl = slice(g * POOL_GC, (g + 1) * POOL_GC)
        cur = cur_ref[:, sl]
        src = jnp.concatenate([prev_ref[ROW_BLK - POOL_HALO:, sl], cur, next_ref[:POOL_HALO, sl]], axis=0)
        lo = jnp.maximum(t - w // 2, lo_lim)
        hi = jnp.minimum(t + w // 2 - 1, hi_lim)
        band = jnp.where(jnp.logical_and(j >= lo, j <= hi), 1.0, 0.0).astype(BF16)
        cnt = (jnp.minimum(t1 + w // 2 - 1, hi_lim) - jnp.maximum(t1 - w // 2, lo_lim) + 1).astype(F32)
        s_hi, s_mid, s_lo = _split3(src)
        wsum = _dot(band, s_hi) + _dot(band, s_mid) + _dot(band, s_lo)
        z = (wsum / cnt - cur).astype(BF16)
        y = _dot(z, w_ref[g]) * sc_ref[:, sl]
        o_ref[:, sl] = (y * _silu(gate_ref[:, sl])).astype(BF16)


def _pool(u, w_pool, pool_scale):
    def prev_map(i):
        return (jnp.where(i < N_LAT_BLK, jnp.maximum(i - 1, 0), N_LAT_BLK), 0)

    def next_map(i):
        return (jnp.where(i < N_LAT_BLK, jnp.minimum(i + 1, N_LAT_BLK - 1), N_LAT_BLK), 0)

    return pl.pallas_call(
        _pool_kernel,
        grid=(N_BLK,),
        in_specs=[pl.BlockSpec((ROW_BLK, POOL_W), prev_map),
                  pl.BlockSpec((ROW_BLK, POOL_W), lambda i: (i, 0)),
                  pl.BlockSpec((ROW_BLK, POOL_W), next_map),
                  pl.BlockSpec((ROW_BLK, POOL_W), lambda i: (i, 1)),
                  pl.BlockSpec((4, POOL_GC, POOL_GC), lambda i: (0, 0, 0)),
                  pl.BlockSpec((1, POOL_W), lambda i: (0, 0))],
        out_specs=pl.BlockSpec((ROW_BLK, POOL_W), lambda i: (i, 0)),
        out_shape=jax.ShapeDtypeStruct((ROWS, POOL_W), BF16),
        name="pool_mix",
    )(u, u, u, u, w_pool.astype(BF16), pool_scale.reshape(1, POOL_W))


def _outproj_residual(a_ref, b_ref, w_ref, x_ref, mod_ref, wbf_ref):
    @pl.when(pl.program_id(0) == 0)
    def _():
        wbf_ref[...] = w_ref[...].astype(BF16)

    k1 = a_ref.shape[1]
    y = _dot(a_ref[...], wbf_ref[0:k1, :]) + _dot(b_ref[...], wbf_ref[k1:, :])
    return x_ref[...] + mod_ref[0, 2:3, :] * y


def _outproj_kernel(a_ref, b_ref, w_ref, x_ref, mod_ref, nw_ref, modn_ref, xo_ref, h_ref, wbf_ref):
    xn = _outproj_residual(a_ref, b_ref, w_ref, x_ref, mod_ref, wbf_ref)
    xo_ref[...] = xn
    h_ref[...] = _modnorm(xn, nw_ref[...], modn_ref[0]).astype(BF16)


def _outproj_final_kernel(a_ref, b_ref, w_ref, x_ref, mod_ref, nw_ref, o_ref, wbf_ref):
    xn = _outproj_residual(a_ref, b_ref, w_ref, x_ref, mod_ref, wbf_ref)
    o_ref[...] = _rms(xn) * nw_ref[...]


def _out_proj(a, b, w, layer, x, mod, nw_next, mod_next):
    k1, k2 = a.shape[1], b.shape[1]
    row = lambda i: (i, 0)
    full = lambda i: (0, 0)
    stream = lambda i: (_is_ctx_blk(i), 0, 0)
    common = [pl.BlockSpec((ROW_BLK, k1), row), pl.BlockSpec((ROW_BLK, k2), row),
              pl.BlockSpec((None, D_MODEL, D_MODEL), lambda i: (layer, 0, 0), pipeline_mode=pl.Buffered(1)),
              pl.BlockSpec((ROW_BLK, D_MODEL), row), pl.BlockSpec((1, 8, D_MODEL), stream),
              pl.BlockSpec((1, D_MODEL), full)]
    scratch = [pltpu.VMEM((D_MODEL, D_MODEL), BF16)]
    params = pltpu.CompilerParams(dimension_semantics=("arbitrary",))
    if mod_next is None:
        return pl.pallas_call(
            _outproj_final_kernel,
            grid=(N_LAT_BLK,),
            in_specs=common,
            out_specs=pl.BlockSpec((ROW_BLK, D_MODEL), row),
            out_shape=jax.ShapeDtypeStruct((SEQ, D_MODEL), F32),
            scratch_shapes=scratch,
            compiler_params=params,
            name="out_proj_final",
        )(a, b, w, x, mod, nw_next.reshape(1, D_MODEL))
    return pl.pallas_call(
        _outproj_kernel,
        grid=(N_BLK,),
        in_specs=common + [pl.BlockSpec((1, 8, D_MODEL), stream)],
        out_specs=[pl.BlockSpec((ROW_BLK, D_MODEL), row), pl.BlockSpec((ROW_BLK, D_MODEL), row)],
        out_shape=[jax.ShapeDtypeStruct((ROWS, D_MODEL), F32), jax.ShapeDtypeStruct((ROWS, D_MODEL), BF16)],
        scratch_shapes=scratch,
        compiler_params=params,
        name="out_proj",
    )(a, b, w, x, mod, nw_next.reshape(1, D_MODEL), mod_next)


M_QSCALE = M_DQK ** -0.5


def _log_sigmoid(x):
    return jnp.minimum(x, 0.0) - jnp.log(1.0 + jnp.exp(-jnp.abs(x)))


def _mlstm_dir(q_ref, k_ref, v_ref, g_ref, gb_ref, ht_ref, c_ref, n_ref, m_ref, reverse):
    T = M_CHUNK
    d = 1 if reverse else 0
    base = 2 * M_HEADS * d
    g = g_ref[...] + gb_ref[...]
    glf = _log_sigmoid(g)
    row = lax.broadcasted_iota(jnp.int32, (T, T), 0)
    col = lax.broadcasted_iota(jnp.int32, (T, T), 1)
    tri = jnp.where((col >= row) if reverse else (col <= row), 1.0, 0.0).astype(BF16)
    reach = (row >= col) if reverse else (row <= col)
    p_hi, p_mid, p_lo = _split3(glf)
    bcum = _dot(tri, p_hi) + _dot(tri, p_mid) + _dot(tri, p_lo)
    g_sh = pltpu.roll(g, M_HEADS, 1)
    r_all = g_sh - bcum
    end = 0 if reverse else T - 1
    b_last = bcum[end:end + 1, :]
    a_all = b_last - bcum + g_sh
    m_prev_row = m_ref[d:d + 1, :]
    m_new_row = jnp.maximum(b_last + m_prev_row, jnp.max(a_all, axis=0, keepdims=True))
    decay_row = jnp.exp(b_last + m_prev_row - m_new_row)
    w_all = jnp.exp(a_all - m_new_row)
    m_ref[d:d + 1, :] = m_new_row
    b_t = bcum.T
    for h in range(M_HEADS):
        cf = base + M_HEADS + h
        st = M_HEADS * d + h
        m_prev = m_prev_row[:, cf:cf + 1]
        decay = decay_row[:, cf:cf + 1]
        r_mat = jnp.where(reach, r_all[:, cf:cf + 1], -jnp.inf)
        mx = jnp.maximum(m_prev, jnp.max(r_mat, axis=0, keepdims=True))
        iw = jnp.exp(m_prev - mx)
        dm = jnp.exp(r_mat - mx)

        q = (q_ref[:, h * M_DQK:(h + 1) * M_DQK] * M_QSCALE).astype(BF16)
        kf = k_ref[:, h * M_DQK:(h + 1) * M_DQK]
        v = v_ref[:, h * M_DV:(h + 1) * M_DV].astype(BF16)
        c_state = c_ref[st]
        n_state = n_ref[st:st + 1, :]
        n_rows = jnp.broadcast_to(n_state, (8, M_DQK)).astype(BF16)

        s = _dot_nt(kf.astype(BF16), q) * dm
        den = iw * _dot_nt(n_rows, q)[0:1, :] + jnp.sum(s, axis=0, keepdims=True)
        inv = 1.0 / jnp.maximum(jnp.abs(den), jnp.exp(-(b_t[cf:cf + 1, :] + mx)))
        ht_ref[h * M_DV:(h + 1) * M_DV, :] = (_dot_tn(v, (s * inv).astype(BF16))
                                              + _dot_nt(c_state.astype(BF16), q) * (iw * inv))

        wk = w_all[:, cf:cf + 1] * kf
        c_ref[st] = decay * c_state + _dot_tn(v, wk.astype(BF16))
        n_ref[st:st + 1, :] = decay * n_state + jnp.sum(wk, axis=0, keepdims=True)


def _mlstm_kernel(qf_ref, kf_ref, vf_ref, gf_ref, qb_ref, kb_ref, vb_ref, gb_ref, bias_ref,
                  hf_ref, hb_ref, c_ref, n_ref, m_ref):
    @pl.when(pl.program_id(0) == 0)
    def _():
        c_ref[...] = jnp.zeros_like(c_ref)
        n_ref[...] = jnp.zeros_like(n_ref)
        m_ref[...] = jnp.zeros_like(m_ref)

    _mlstm_dir(qf_ref, kf_ref, vf_ref, gf_ref, bias_ref, hf_ref, c_ref, n_ref, m_ref, reverse=False)
    _mlstm_dir(qb_ref, kb_ref, vb_ref, gb_ref, bias_ref, hb_ref, c_ref, n_ref, m_ref, reverse=True)


def _mlstm(u, gate_b):
    fwd = lambda k: jnp.where(k == 0, N_LAT_BLK, k - 1)
    bwd = lambda k: jnp.where(k == 0, N_LAT_BLK, N_LAT_BLK - k)
    mg_blk = OD_COL_MG // LANES

    def specs(rmap):
        return [pl.BlockSpec((M_CHUNK, M_QK_W), lambda k: (rmap(k), 0)),
                pl.BlockSpec((M_CHUNK, M_QK_W), lambda k: (rmap(k), 1)),
                pl.BlockSpec((M_CHUNK, M_V_W), lambda k: (rmap(k), 1)),
                pl.BlockSpec((M_CHUNK, LANES), lambda k: (rmap(k), mg_blk))]

    bias = jnp.concatenate([gate_b.astype(F32), jnp.zeros((LANES - 4 * M_HEADS,), F32)]).reshape(1, LANES)
    return pl.pallas_call(
        _mlstm_kernel,
        grid=(N_BLK,),
        in_specs=specs(fwd) + specs(bwd) + [pl.BlockSpec((1, LANES), lambda k: (0, 0))],
        out_specs=[pl.BlockSpec((M_V_W, M_CHUNK), lambda k: (0, fwd(k))),
                   pl.BlockSpec((M_V_W, M_CHUNK), lambda k: (0, bwd(k)))],
        out_shape=[jax.ShapeDtypeStruct((M_V_W, ROWS), F32), jax.ShapeDtypeStruct((M_V_W, ROWS), F32)],
        scratch_shapes=[pltpu.VMEM((2 * M_HEADS, M_DV, M_DQK), F32),
                        pltpu.VMEM((2 * M_HEADS, M_DQK), F32),
                        pltpu.VMEM((8, LANES), F32)],
        compiler_params=pltpu.CompilerParams(dimension_semantics=("arbitrary",)),
        name="mlstm_scan",
    )(u, u, u, u, u, u, u, u, bias)


def _mlstm_post_kernel(hf_ref, hb_ref, o_ref, z_ref, nw_ref, y_ref):
    for h in range(M_HEADS):
        sl = slice(h * M_DV, (h + 1) * M_DV)
        ht = hf_ref[sl, :] + hb_ref[sl, :]
        hn = (ht * lax.rsqrt(jnp.mean(ht * ht, axis=0, keepdims=True) + EPS)).T * nw_ref[:, sl]
        y_ref[:, sl] = (hn * _sigmoid(o_ref[:, sl]) * _silu(z_ref[:, sl])).astype(BF16)


def _mlstm_post(hf, hb, u, head_norm):
    row = lambda i: (i, 0)
    col = lambda i: (0, i)
    return pl.pallas_call(
        _mlstm_post_kernel,
        grid=(N_BLK,),
        in_specs=[pl.BlockSpec((M_V_W, ROW_BLK), col), pl.BlockSpec((M_V_W, ROW_BLK), col),
                  pl.BlockSpec((ROW_BLK, M_V_W), lambda i: (i, OD_COL_MO // M_V_W)),
                  pl.BlockSpec((ROW_BLK, M_V_W), lambda i: (i, OD_COL_MZ // M_V_W)),
                  pl.BlockSpec((1, M_V_W), lambda i: (0, 0))],
        out_specs=pl.BlockSpec((ROW_BLK, M_V_W), row),
        out_shape=jax.ShapeDtypeStruct((ROWS, M_V_W), BF16),
        name="mlstm_post",
    )(hf, hb, u, u, head_norm.reshape(1, M_V_W))


W_NLAT = SEQ // WINDOW
W_NBLK = ROWS // WINDOW
W_QSCALE = (A_HD ** -0.5) * LOG2E
W_KEYS = 3 * WINDOW + CTX_LEN
W_PAIRS = A_HEADS // A_KV_HEADS // 2


def _win_kernel(sink_ref, q_ref, az0_ref, az1_ref, az2_ref, az3_ref, kvp_ref, kvc_ref, kvn_ref, kvx_ref,
                tq_ref, tp_ref, tn_ref, o_ref):
    az_refs = (az0_ref, az1_ref, az2_ref, az3_ref)
    bi = pl.program_id(0)
    is_lat = bi < W_NLAT
    tq = (tq_ref[0], tq_ref[1], tq_ref[2])
    lane_k = lax.broadcasted_iota(jnp.int32, (W_KEYS, LANES), 1)

    qi = lax.broadcasted_iota(jnp.int32, (WINDOW, WINDOW), 0)
    kj = lax.broadcasted_iota(jnp.int32, (WINDOW, WINDOW), 1)
    ok_prev = jnp.logical_and(jnp.logical_and(is_lat, bi >= 1), kj >= qi)
    ok_cur = jnp.logical_and(is_lat, kj >= 0)
    ok_next = jnp.logical_and(jnp.logical_and(is_lat, bi <= W_NLAT - 2), kj <= qi)
    ok_ctx = lax.broadcasted_iota(jnp.int32, (WINDOW, CTX_LEN), 1) >= 0
    valid = jnp.concatenate([ok_prev, ok_cur, ok_next, ok_ctx], axis=1)
    valid4 = jnp.concatenate([valid] * W_PAIRS, axis=0)
    row4 = lax.broadcasted_iota(jnp.int32, (W_PAIRS * WINDOW, 1), 0)

    kk = jnp.concatenate([_rope(kvp_ref[:, 0:LANES], tp_ref[0], tp_ref[1], tp_ref[2]),
                          _rope(kvc_ref[:, 0:LANES], *tq),
                          _rope(kvn_ref[:, 0:LANES], tn_ref[0], tn_ref[1], tn_ref[2]),
                          kvx_ref[:, 0:LANES]], axis=0)
    vv = jnp.concatenate([kvp_ref[:, LANES:2 * LANES], kvc_ref[:, LANES:2 * LANES],
                          kvn_ref[:, LANES:2 * LANES], kvx_ref[:, LANES:2 * LANES]], axis=0)
    kk_sw = pltpu.roll(kk, A_HD, 1)
    vv_sw = pltpu.roll(vv, A_HD, 1)
    low = lane_k < A_HD

    for g in range(A_KV_HEADS):
        k_src = (kk, kk_sw) if g == 0 else (kk_sw, kk)
        v_src = (vv, vv_sw) if g == 0 else (vv_sw, vv)
        k_half = [jnp.where(low, k_src[0], 0.0).astype(BF16), jnp.where(low, 0.0, k_src[1]).astype(BF16)]
        v_half = [jnp.where(low, v_src[0], 0.0).astype(BF16), jnp.where(low, 0.0, v_src[1]).astype(BF16)]
        cols = [(g * W_PAIRS + p) * LANES for p in range(W_PAIRS)]
        qs = jnp.concatenate([(_rope(q_ref[:, c:c + LANES], *tq) * W_QSCALE).astype(BF16) for c in cols],
                             axis=0)
        out = jnp.zeros((W_PAIRS * WINDOW, LANES), F32)
        for par in range(2):
            snk = jnp.full((W_PAIRS * WINDOW, 1), sink_ref[g * 8 + par] * LOG2E, F32)
            for p in range(1, W_PAIRS):
                snk = jnp.where(row4 >= p * WINDOW, sink_ref[g * 8 + 2 * p + par] * LOG2E, snk)
            s = jnp.where(valid4, _dot_nt(qs, k_half[par]), -jnp.inf)
            m = jnp.maximum(jnp.max(s, axis=1, keepdims=True), snk)
            e = jnp.exp2(s - m)
            den = jnp.sum(e, axis=1, keepdims=True) + jnp.exp2(snk - m)
            out = out + _dot(e.astype(BF16), v_half[par]) / den
        for p, c in enumerate(cols):
            grp = c // LANES
            az = az_refs[grp // 2][:, (grp % 2) * LANES:(grp % 2 + 1) * LANES]
            o_ref[:, c:c + LANES] = (out[p * WINDOW:(p + 1) * WINDOW] * _silu(az)).astype(BF16)


def _win_attn(u, sink, tabs):
    kvb = OD_COL_KV // OD_KV_W
    cur = lambda b: jnp.minimum(b, W_NLAT - 1)
    prv = lambda b: jnp.clip(b - 1, 0, W_NLAT - 1)
    nxt = lambda b: jnp.clip(b + 1, 0, W_NLAT - 1)
    tab = lambda f: pl.BlockSpec((3, WINDOW, LANES), lambda b: (0, f(b), 0))
    az_spec = lambda n: pl.BlockSpec((WINDOW, 2 * LANES), lambda b: (b, OD_COL_AZ // (2 * LANES) + n))
    return pl.pallas_call(
        _win_kernel,
        grid=(W_NBLK,),
        in_specs=[pl.BlockSpec(memory_space=pltpu.SMEM),
                  pl.BlockSpec((WINDOW, A_W), lambda b: (b, OD_COL_AQ // A_W)),
                  az_spec(0), az_spec(1), az_spec(2), az_spec(3),
                  pl.BlockSpec((WINDOW, OD_KV_W), lambda b: (prv(b), kvb)),
                  pl.BlockSpec((WINDOW, OD_KV_W), lambda b: (cur(b), kvb)),
                  pl.BlockSpec((WINDOW, OD_KV_W), lambda b: (nxt(b), kvb)),
                  pl.BlockSpec((CTX_LEN, OD_KV_W), lambda b: (SEQ // CTX_LEN, kvb)),
                  tab(lambda b: b), tab(prv), tab(nxt)],
        out_specs=pl.BlockSpec((WINDOW, A_W), lambda b: (b, 0)),
        out_shape=jax.ShapeDtypeStruct((ROWS, A_W), BF16),
        name="window_attention",
    )(sink, u, u, u, u, u, u, u, u, u, tabs, tabs, tabs)


def kernel(x, c, ctx, c_ctx, ada_w, ada_b, norm_w, ev_w_in, ev_q_norm, ev_kv_norm, ev_w_uq, ev_w_ukv,
           ev_w_pool, ev_pool_scale, ev_w_out, od_w_in, od_gate_b, od_head_norm, od_sink, od_w_out, final_norm):
    tabs = _rope_tables()
    mod = _ada(c, c_ctx, ada_w, ada_b)

    n_ev = ev_w_in.shape[0]
    ev_w = _ev_layout(ev_w_in)
    od_w = _od_layout(od_w_in)
    wq = ev_w_uq.astype(BF16).reshape(n_ev, Q_RANK, MLA_HEADS, MLA_QK)
    wq_nope = wq[..., :MLA_NOPE].reshape(n_ev, Q_RANK, MLA_HEADS * MLA_NOPE)
    wq_rope = wq[..., MLA_NOPE:].reshape(n_ev, Q_RANK, MLA_HEADS * ROPE_DIM)
    wkv = ev_w_ukv.astype(BF16).reshape(n_ev, KV_RANK, MLA_HEADS, MLA_NOPE + MLA_VDIM)
    wk_nope = wkv[..., :MLA_NOPE].reshape(n_ev, KV_RANK, MLA_HEADS * MLA_NOPE)
    wv = wkv[..., MLA_NOPE:].reshape(n_ev, KV_RANK, MLA_W)
    q_norm = ev_q_norm.reshape(n_ev, 1, Q_RANK)
    kv_norm = ev_kv_norm.reshape(n_ev, 1, KV_RANK)

    xs, h = _prenorm(x[0], ctx[0], norm_w[0], mod[0])
    out = None
    for i in range(DEPTH):
        j = i // 2
        if i % 2 == 0:
            u = _in_proj(h, ev_w, j, EV_TN)
            q = _q_proj(u, j, q_norm, wq_nope, wq_rope, tabs)
            k, v = _kv_proj(u, j, kv_norm, wk_nope, wv, tabs)
            a = _pool(u, ev_w_pool[j], ev_pool_scale[j])
            b = _flash(q, k, v, u)
            w_out = ev_w_out
        else:
            u = _in_proj(h, od_w, j, OD_TN)
            hf, hb = _mlstm(u, od_gate_b[j])
            a = _mlstm_post(hf, hb, u, od_head_norm[j])
            b = _win_attn(u, od_sink[j].astype(F32), tabs)
            w_out = od_w_out
        if i < DEPTH - 1:
            xs, h = _out_proj(a, b, w_out, j, xs, mod[i], norm_w[i + 1], mod[i + 1])
        else:
            out = _out_proj(a, b, w_out, j, xs, mod[i], final_norm, None)
    return out[None]
```

```python
import functools
import math

import numpy as np
import jax
import jax.numpy as jnp
from jax import lax
from jax.experimental import pallas as pl
from jax.experimental.pallas import tpu as pltpu

F32 = jnp.float32
BF16 = jnp.bfloat16

D_MODEL = 2048
SEQ = 8192
CTX_LEN = 256
ROWS = SEQ + CTX_LEN
DEPTH = 4
GRID_W = 64
EPS = 1e-6
ROPE_THETA = 10000.0
ROPE_DIM = 64
LOG2E = math.log2(math.e)

LANES = 128
ROW_BLK = 256
N_LAT_BLK = SEQ // ROW_BLK
N_BLK = ROWS // ROW_BLK
MM_ROWS = 768

POOL_WINDOWS = (2, 4, 8, 16)
POOL_W = 512
POOL_GC = 128
POOL_HALO = 16

MLA_HEADS = 12
MLA_NOPE = 128
MLA_VDIM = 128
MLA_QK = MLA_NOPE + ROPE_DIM
MLA_W = MLA_HEADS * MLA_VDIM
Q_RANK = 512
KV_RANK = 512
HEAD_PAD = 256

M_HEADS = 4
M_DQK = 128
M_DV = 256
M_QK_W = 512
M_V_W = 1024
M_CHUNK = 256

A_HEADS = 16
A_KV_HEADS = 2
A_HD = 64
A_W = 1024
WINDOW = 128

NEG_BIG = -1e30

EV_N = 3840
EV_COL_CQ = 1024
EV_COL_CKV = 1536
EV_COL_AGATE = 2048
EV_COL_KR = 3584
EV_TN = 1920

OD_N = 6656
OD_COL_MO = 2048
OD_COL_MZ = 3072
OD_COL_AQ = 4096
OD_COL_KV = 5120
OD_KV_W = 256
OD_COL_AZ = 5376
OD_COL_MG = 6400
OD_TN = 1664

PREP_W = 256


def _prep_kernel(a_ref, b_ref, o_ref, *, n_copy, n_shift, shift):
    t = pl.program_id(1)

    @pl.when(t < n_copy)
    def _():
        o_ref[...] = a_ref[...].T.astype(BF16)

    @pl.when(jnp.logical_and(t >= n_copy, t < n_copy + n_shift))
    def _():
        o_ref[...] = jnp.concatenate([a_ref[shift:, :], b_ref[...]], axis=0).T.astype(BF16)

    @pl.when(t == n_copy + n_shift)
    def _():
        pad = jnp.zeros((PREP_W - shift, b_ref.shape[1]), F32)
        o_ref[...] = jnp.concatenate([b_ref[...], pad], axis=0).T.astype(BF16)


def _prep_weight(w, n_copy, n_shift, shift):
    layers, d, n = w.shape
    n_tiles = n_copy + n_shift + 1
    per = PREP_W // shift
    a_map = lambda l, t: (l, jnp.minimum(t, n_copy + n_shift - 1), 0)
    b_map = lambda l, t: (l, jnp.where(t == n_tiles - 1, n_copy * per, jnp.minimum((t + 1) * per, n // shift - 1)), 0)
    return pl.pallas_call(
        functools.partial(_prep_kernel, n_copy=n_copy, n_shift=n_shift, shift=shift),
        grid=(layers, n_tiles),
        in_specs=[pl.BlockSpec((None, PREP_W, d), a_map),
                  pl.BlockSpec((None, shift, d), b_map)],
        out_specs=pl.BlockSpec((None, d, PREP_W), lambda l, t: (l, 0, t)),
        out_shape=jax.ShapeDtypeStruct((layers, d, n_tiles * PREP_W), BF16),
        name="weight_layout",
    )(jnp.swapaxes(w, 1, 2), jnp.swapaxes(w, 1, 2))


def _ev_layout(w):
    return _prep_weight(w, n_copy=8, n_shift=6, shift=ROPE_DIM)


def _od_layout(w):
    return _prep_weight(w, n_copy=12, n_shift=13, shift=4 * M_HEADS)


def _rope_tables():
    t = jnp.arange(SEQ)
    row = (t // GRID_W).astype(F32)
    col = (t % GRID_W).astype(F32)
    half = ROPE_DIM // 2
    inv = ROPE_THETA ** (-jnp.arange(0, half, 2, dtype=F32) / half)
    ar = row[:, None] * inv
    ac = col[:, None] * inv
    zero = jnp.zeros_like(ar)
    cos64 = jnp.concatenate([jnp.cos(ar), jnp.cos(ar), jnp.cos(ac), jnp.cos(ac)], axis=-1)
    sa64 = jnp.concatenate([-jnp.sin(ar), zero, -jnp.sin(ac), zero], axis=-1)
    sb64 = jnp.concatenate([zero, jnp.sin(ar), zero, jnp.sin(ac)], axis=-1)
    ctx_zero = jnp.zeros((CTX_LEN, LANES), F32)
    cos = jnp.concatenate([jnp.tile(cos64, (1, 2)), jnp.ones((CTX_LEN, LANES), F32)], axis=0)
    sin_a = jnp.concatenate([jnp.tile(sa64, (1, 2)), ctx_zero], axis=0)
    sin_b = jnp.concatenate([jnp.tile(sb64, (1, 2)), ctx_zero], axis=0)
    return jnp.stack([cos, sin_a, sin_b])


def _rope(x, cos, sin_a, sin_b):
    up = pltpu.roll(x, LANES - ROPE_DIM // 4, 1)
    down = pltpu.roll(x, ROPE_DIM // 4, 1)
    return x * cos + up * sin_a + down * sin_b


def _dot(a, b):
    return jnp.dot(a, b, preferred_element_type=F32)


def _dot_nt(a, b):
    return lax.dot_general(a, b, (((1,), (1,)), ((), ())), preferred_element_type=F32)


def _dot_tn(a, b):
    return lax.dot_general(a, b, (((0,), (0,)), ((), ())), preferred_element_type=F32)


def _rms(x):
    return x * lax.rsqrt(jnp.mean(x * x, axis=-1, keepdims=True) + EPS)


def _modnorm(x, nw, mod):
    return (_rms(x) * nw) * (1.0 + mod[1:2, :]) + mod[0:1, :]


def _silu(x):
    return x * (1.0 / (1.0 + jnp.exp(-x)))


def _sigmoid(x):
    return 1.0 / (1.0 + jnp.exp(-x))


def _split3(x):
    hi = x.astype(BF16)
    r1 = x - hi.astype(F32)
    mid = r1.astype(BF16)
    lo = (r1 - mid.astype(F32)).astype(BF16)
    return hi, mid, lo


def _is_ctx_blk(i):
    return jnp.where(i < N_LAT_BLK, 0, 1)


def _ada_kernel(s_ref, w_ref, b_ref, o_ref):
    s = _silu(s_ref[...])
    hi = s.astype(BF16)
    lo = (s - hi.astype(F32)).astype(BF16)
    lhs = jnp.concatenate([hi, lo], axis=0)
    r = _dot(lhs, w_ref[0].astype(BF16))
    o_ref[0] = r[0:8] + r[8:16] + b_ref[0]


def _ada(c, c_ctx, ada_w, ada_b):
    tn = 1536
    s = jnp.concatenate([c.reshape(1, D_MODEL), c_ctx.reshape(1, D_MODEL), jnp.zeros((6, D_MODEL), F32)], axis=0)
    out = pl.pallas_call(
        _ada_kernel,
        grid=(DEPTH, 3 * D_MODEL // tn),
        in_specs=[pl.BlockSpec((8, D_MODEL), lambda l, j: (0, 0)),
                  pl.BlockSpec((1, D_MODEL, tn), lambda l, j: (l, 0, j)),
                  pl.BlockSpec((1, 1, tn), lambda l, j: (l, 0, j))],
        out_specs=pl.BlockSpec((1, 8, tn), lambda l, j: (l, 0, j)),
        out_shape=jax.ShapeDtypeStruct((DEPTH, 8, 3 * D_MODEL), F32),
        name="ada_mod",
    )(s, ada_w, ada_b.reshape(DEPTH, 1, 3 * D_MODEL))
    m = out[:, :2, :].reshape(DEPTH, 2, 3, D_MODEL)
    return jnp.concatenate([m, jnp.zeros((DEPTH, 2, 5, D_MODEL), F32)], axis=2)


def _split_rows_specs():
    return [pl.BlockSpec((ROW_BLK, D_MODEL), lambda i: (jnp.minimum(i, N_LAT_BLK - 1), 0)),
            pl.BlockSpec((CTX_LEN, D_MODEL), lambda i: (0, 0))]


def _split_rows(x_ref, c_ref):
    return jnp.where(pl.program_id(0) < N_LAT_BLK, x_ref[...], c_ref[...])


def _prenorm_kernel(x_ref, c_ref, nw_ref, mod_ref, h_ref):
    h_ref[...] = _modnorm(_split_rows(x_ref, c_ref), nw_ref[...], mod_ref[0]).astype(BF16)


def _prenorm(x, ctx, nw, mod):
    return pl.pallas_call(
        _prenorm_kernel,
        grid=(N_BLK,),
        in_specs=_split_rows_specs() + [pl.BlockSpec((1, D_MODEL), lambda i: (0, 0)),
                                        pl.BlockSpec((1, 8, D_MODEL), lambda i: (_is_ctx_blk(i), 0, 0))],
        out_specs=pl.BlockSpec((ROW_BLK, D_MODEL), lambda i: (i, 0)),
        out_shape=jax.ShapeDtypeStruct((ROWS, D_MODEL), BF16),
        name="prenorm",
    )(x, ctx, nw.reshape(1, D_MODEL), mod)


def _mm_kernel(h_ref, w_ref, o_ref):
    o_ref[...] = _dot(h_ref[...], w_ref[...])


def _in_proj(h, w, layer, tn):
    n = w.shape[2]
    return pl.pallas_call(
        _mm_kernel,
        grid=(ROWS // MM_ROWS, n // tn),
        in_specs=[pl.BlockSpec((MM_ROWS, D_MODEL), lambda i, j: (i, 0)),
                  pl.BlockSpec((None, D_MODEL, tn), lambda i, j: (layer, 0, j))],
        out_specs=pl.BlockSpec((MM_ROWS, tn), lambda i, j: (i, j)),
        out_shape=jax.ShapeDtypeStruct((ROWS, n), F32),
        name="in_proj",
    )(h, w)


Q_SCALE = (MLA_QK ** -0.5) * LOG2E


def _qproj_kernel(cq_ref, nw_ref, wn_ref, wr_ref, tab_ref, q_ref):
    xn = (_rms(cq_ref[...]) * nw_ref[...]).astype(BF16)
    qn = _dot(xn, wn_ref[...]) * Q_SCALE
    for h in range(MLA_HEADS):
        q_ref[:, h * HEAD_PAD:h * HEAD_PAD + LANES] = qn[:, h * LANES:(h + 1) * LANES].astype(BF16)
    qr = _dot(xn, wr_ref[...])
    for p in range(MLA_HEADS // 2):
        rp = (_rope(qr[:, p * LANES:(p + 1) * LANES], tab_ref[0], tab_ref[1], tab_ref[2]) * Q_SCALE).astype(BF16)
        for h in (2 * p, 2 * p + 1):
            q_ref[:, h * HEAD_PAD + LANES:(h + 1) * HEAD_PAD] = rp


def _layer_spec(w, layer):
    return pl.BlockSpec((None,) + w.shape[1:], lambda i: (layer, 0, 0))


def _q_proj(u, layer, nw, wn, wr, tabs):
    t = MM_ROWS
    return pl.pallas_call(
        _qproj_kernel,
        grid=(ROWS // t,),
        in_specs=[pl.BlockSpec((t, Q_RANK), lambda i: (i, EV_COL_CQ // Q_RANK)),
                  _layer_spec(nw, layer), _layer_spec(wn, layer), _layer_spec(wr, layer),
                  pl.BlockSpec((3, t, LANES), lambda i: (0, i, 0))],
        out_specs=pl.BlockSpec((t, MLA_HEADS * HEAD_PAD), lambda i: (i, 0)),
        out_shape=jax.ShapeDtypeStruct((ROWS, MLA_HEADS * HEAD_PAD), BF16),
        name="mla_q_proj",
    )(u, nw, wn, wr, tabs)


def _kvproj_kernel(ckv_ref, kr_ref, nw_ref, wk_ref, wv_ref, tab_ref, k_ref, v_ref):
    xn = (_rms(ckv_ref[...]) * nw_ref[...]).astype(BF16)
    kn = _dot(xn, wk_ref[...])
    vv = _dot(xn, wv_ref[...])
    kr2 = _rope(kr_ref[...], tab_ref[0], tab_ref[1], tab_ref[2])
    lane = lax.broadcasted_iota(jnp.int32, kr2.shape, 1)
    kr_lo = jnp.where(lane < ROPE_DIM, kr2, 0.0)
    k_lo = kr_lo.astype(BF16)
    k_hi = pltpu.roll(kr_lo, ROPE_DIM, 1).astype(BF16)
    ones_col = jnp.where(lane == 0, 1.0, 0.0).astype(BF16)
    for h in range(MLA_HEADS):
        sl = slice(h * LANES, (h + 1) * LANES)
        k_ref[:, h * HEAD_PAD:h * HEAD_PAD + LANES] = kn[:, sl].astype(BF16)
        k_ref[:, h * HEAD_PAD + LANES:(h + 1) * HEAD_PAD] = k_lo if h % 2 == 0 else k_hi
        v_ref[:, h * HEAD_PAD:h * HEAD_PAD + LANES] = vv[:, sl].astype(BF16)
        v_ref[:, h * HEAD_PAD + LANES:(h + 1) * HEAD_PAD] = ones_col


def _kv_proj(u, layer, nw, wk, wv, tabs):
    t = MM_ROWS
    wide = MLA_HEADS * HEAD_PAD
    return pl.pallas_call(
        _kvproj_kernel,
        grid=(ROWS // t,),
        in_specs=[pl.BlockSpec((t, KV_RANK), lambda i: (i, EV_COL_CKV // KV_RANK)),
                  pl.BlockSpec((t, LANES), lambda i: (i, EV_COL_KR // LANES)),
                  _layer_spec(nw, layer), _layer_spec(wk, layer), _layer_spec(wv, layer),
                  pl.BlockSpec((3, t, LANES), lambda i: (0, i, 0))],
        out_specs=[pl.BlockSpec((t, wide), lambda i: (i, 0)),
                   pl.BlockSpec((t, wide), lambda i: (i, 0))],
        out_shape=[jax.ShapeDtypeStruct((ROWS, wide), BF16),
                   jax.ShapeDtypeStruct((ROWS, wide), BF16)],
        name="mla_kv_proj",
    )(u, u, nw, wk, wv, tabs)


FL_TQ = 768
FL_TK = 768
FL_LAST = ROWS // FL_TQ - 1
FL_LAT_IN_LAST = SEQ - FL_LAST * FL_TQ


def _key_chunks():
    return [(i * FL_TK, FL_TK) for i in range(ROWS // FL_TK)]


def _attend(q, k_ref, v_ref, sa_ref, sb_ref, chunks):
    rows = q.shape[0]
    bufs = (sa_ref, sb_ref)

    def scores_into(j):
        off, size = chunks[j]
        s = _dot_nt(q, k_ref[off:off + size, :])
        bufs[j % 2][0:rows, 0:size] = s
        return jnp.max(s, axis=1, keepdims=True)

    def update(j, m_chunk, m_prev, acc):
        off, size = chunks[j]
        m_new = jnp.maximum(m_prev, m_chunk)
        alpha = jnp.exp2(m_prev - m_new)
        p = jnp.exp2(bufs[j % 2][0:rows, 0:size] - m_new).astype(BF16)
        return m_new, alpha * acc + _dot(p, v_ref[off:off + size, :])

    m = jnp.full((rows, 1), NEG_BIG, F32)
    acc = jnp.zeros((rows, HEAD_PAD), F32)
    m_next = scores_into(0)
    for j in range(len(chunks)):
        m_chunk = m_next
        if j + 1 < len(chunks):
            m_next = scores_into(j + 1)
        m, acc = update(j, m_chunk, m, acc)
    return acc[:, :MLA_VDIM] / acc[:, MLA_VDIM:MLA_VDIM + 1]


def _flash_kernel(q_ref, k_ref, v_ref, g_ref, o_ref, sa_ref, sb_ref):
    i = pl.program_id(1)
    chunks = _key_chunks()

    @pl.when(i < FL_LAST)
    def _():
        o = _attend(q_ref[...], k_ref, v_ref, sa_ref, sb_ref, chunks)
        o_ref[...] = (o * _silu(g_ref[...])).astype(BF16)

    @pl.when(i == FL_LAST)
    def _():
        nl = FL_LAT_IN_LAST
        o = _attend(q_ref[0:nl, :], k_ref, v_ref, sa_ref, sb_ref, chunks)
        o_ref[0:nl, :] = (o * _silu(g_ref[0:nl, :])).astype(BF16)
        oc = _attend(q_ref[nl:, :], k_ref, v_ref, sa_ref, sb_ref, [(SEQ, CTX_LEN)])
        o_ref[nl:, :] = (oc * _silu(g_ref[nl:, :])).astype(BF16)


def _flash(q, k, v, u):
    return pl.pallas_call(
        _flash_kernel,
        grid=(MLA_HEADS, ROWS // FL_TQ),
        in_specs=[pl.BlockSpec((FL_TQ, HEAD_PAD), lambda h, i: (i, h)),
                  pl.BlockSpec((ROWS, HEAD_PAD), lambda h, i: (0, h)),
                  pl.BlockSpec((ROWS, HEAD_PAD), lambda h, i: (0, h)),
                  pl.BlockSpec((FL_TQ, MLA_VDIM), lambda h, i: (i, EV_COL_AGATE // MLA_VDIM + h))],
        out_specs=pl.BlockSpec((FL_TQ, MLA_VDIM), lambda h, i: (i, h)),
        out_shape=jax.ShapeDtypeStruct((ROWS, MLA_W), BF16),
        scratch_shapes=[pltpu.VMEM((FL_TQ, FL_TK), F32), pltpu.VMEM((FL_TQ, FL_TK), F32)],
        name="mla_attention",
    )(q, k, v, u)


def _pool_kernel(prev_ref, cur_ref, next_ref, gate_ref, w_ref, sc_ref, o_ref):
    i = pl.program_id(0)
    first = jnp.logical_or(i == 0, i == N_LAT_BLK)
    last = i >= N_LAT_BLK - 1
    lo_lim = jnp.where(first, 0, -POOL_HALO)
    hi_lim = jnp.where(last, ROW_BLK - 1, ROW_BLK - 1 + POOL_HALO)
    src_rows = ROW_BLK + 2 * POOL_HALO
    t = lax.broadcasted_iota(jnp.int32, (ROW_BLK, src_rows), 0)
    j = lax.broadcasted_iota(jnp.int32, (ROW_BLK, src_rows), 1) - POOL_HALO
    t1 = lax.broadcasted_iota(jnp.int32, (ROW_BLK, 1), 0)
    for g, w in enumerate(POOL_WINDOWS):
        sl = slice(g * POOL_GC, (g + 1) * POOL_GC)
        cur = cur_ref[:, sl]
        src = jnp.concatenate([prev_ref[ROW_BLK - POOL_HALO:, sl], cur, next_ref[:POOL_HALO, sl]], axis=0)
        lo = jnp.maximum(t - w // 2, lo_lim)
        hi = jnp.minimum(t + w // 2 - 1, hi_lim)
        band = jnp.where(jnp.logical_and(j >= lo, j <= hi), 1.0, 0.0).astype(BF16)
        cnt = (jnp.minimum(t1 + w // 2 - 1, hi_lim) - jnp.maximum(t1 - w // 2, lo_lim) + 1).astype(F32)
        s_hi, s_mid, s_lo = _split3(src)
        wsum = _dot(band, s_hi) + _dot(band, s_mid) + _dot(band, s_lo)
        z = (wsum / cnt - cur).astype(BF16)
        y = _dot(z, w_ref[g]) * sc_ref[:, sl]
        o_ref[:, sl] = (y * _silu(gate_ref[:, sl])).astype(BF16)


def _pool(u, w_pool, pool_scale):
    def prev_map(i):
        return (jnp.where(i < N_LAT_BLK, jnp.maximum(i - 1, 0), N_LAT_BLK), 0)

    def next_map(i):
        return (jnp.where(i < N_LAT_BLK, jnp.minimum(i + 1, N_LAT_BLK - 1), N_LAT_BLK), 0)

    return pl.pallas_call(
        _pool_kernel,
        grid=(N_BLK,),
        in_specs=[pl.BlockSpec((ROW_BLK, POOL_W), prev_map),
                  pl.BlockSpec((ROW_BLK, POOL_W), lambda i: (i, 0)),
                  pl.BlockSpec((ROW_BLK, POOL_W), next_map),
                  pl.BlockSpec((ROW_BLK, POOL_W), lambda i: (i, 1)),
                  pl.BlockSpec((4, POOL_GC, POOL_GC), lambda i: (0, 0, 0)),
                  pl.BlockSpec((1, POOL_W), lambda i: (0, 0))],
        out_specs=pl.BlockSpec((ROW_BLK, POOL_W), lambda i: (i, 0)),
        out_shape=jax.ShapeDtypeStruct((ROWS, POOL_W), BF16),
        name="pool_mix",
    )(u, u, u, u, w_pool.astype(BF16), pool_scale.reshape(1, POOL_W))


def _outproj_residual(a_ref, b_ref, w_ref, x_ref, c_ref, mod_ref, wbf_ref):
    @pl.when(pl.program_id(0) == 0)
    def _():
        wbf_ref[...] = w_ref[...].astype(BF16)

    k1 = a_ref.shape[1]
    y = _dot(a_ref[...], wbf_ref[0:k1, :]) + _dot(b_ref[...], wbf_ref[k1:, :])
    return _split_rows(x_ref, c_ref) + mod_ref[0, 2:3, :] * y


def _outproj_kernel(a_ref, b_ref, w_ref, x_ref, c_ref, mod_ref, nw_ref, modn_ref, xo_ref, h_ref, wbf_ref):
    xn = _outproj_residual(a_ref, b_ref, w_ref, x_ref, c_ref, mod_ref, wbf_ref)
    xo_ref[...] = xn
    h_ref[...] = _modnorm(xn, nw_ref[...], modn_ref[0]).astype(BF16)


def _outproj_final_kernel(a_ref, b_ref, w_ref, x_ref, c_ref, mod_ref, nw_ref, o_ref, wbf_ref):
    xn = _outproj_residual(a_ref, b_ref, w_ref, x_ref, c_ref, mod_ref, wbf_ref)
    o_ref[...] = _rms(xn) * nw_ref[...]


def _out_proj(a, b, w, layer, x, x_ctx, mod, nw_next, mod_next):
    k1, k2 = a.shape[1], b.shape[1]
    row = lambda i: (i, 0)
    full = lambda i: (0, 0)
    stream = lambda i: (_is_ctx_blk(i), 0, 0)
    ctx_blk = x_ctx.shape[0] // CTX_LEN - 1
    common = [pl.BlockSpec((ROW_BLK, k1), row), pl.BlockSpec((ROW_BLK, k2), row),
              pl.BlockSpec((None, D_MODEL, D_MODEL), lambda i: (layer, 0, 0), pipeline_mode=pl.Buffered(1)),
              pl.BlockSpec((ROW_BLK, D_MODEL), lambda i: (jnp.minimum(i, N_LAT_BLK - 1), 0)),
              pl.BlockSpec((CTX_LEN, D_MODEL), lambda i: (ctx_blk, 0)),
              pl.BlockSpec((1, 8, D_MODEL), stream),
              pl.BlockSpec((1, D_MODEL), full)]
    scratch = [pltpu.VMEM((D_MODEL, D_MODEL), BF16)]
    params = pltpu.CompilerParams(dimension_semantics=("arbitrary",))
    if mod_next is None:
        return pl.pallas_call(
            _outproj_final_kernel,
            grid=(N_LAT_BLK,),
            in_specs=common,
            out_specs=pl.BlockSpec((ROW_BLK, D_MODEL), row),
            out_shape=jax.ShapeDtypeStruct((SEQ, D_MODEL), F32),
            scratch_shapes=scratch,
            compiler_params=params,
            name="out_proj_final",
        )(a, b, w, x, x_ctx, mod, nw_next.reshape(1, D_MODEL))
    return pl.pallas_call(
        _outproj_kernel,
        grid=(N_BLK,),
        in_specs=common + [pl.BlockSpec((1, 8, D_MODEL), stream)],
        out_specs=[pl.BlockSpec((ROW_BLK, D_MODEL), row), pl.BlockSpec((ROW_BLK, D_MODEL), row)],
        out_shape=[jax.ShapeDtypeStruct((ROWS, D_MODEL), F32), jax.ShapeDtypeStruct((ROWS, D_MODEL), BF16)],
        scratch_shapes=scratch,
        compiler_params=params,
        name="out_proj",
    )(a, b, w, x, x_ctx, mod, nw_next.reshape(1, D_MODEL), mod_next)


M_QSCALE = M_DQK ** -0.5


def _log_sigmoid(x):
    return jnp.minimum(x, 0.0) - jnp.log(1.0 + jnp.exp(-jnp.abs(x)))


def _mlstm_dir(q_ref, k_ref, v_ref, g_ref, gb_ref, ht_ref, c_ref, n_ref, m_ref, reverse):
    T = M_CHUNK
    d = 1 if reverse else 0
    base = 2 * M_HEADS * d
    g = g_ref[...] + gb_ref[...]
    glf = _log_sigmoid(g)
    row = lax.broadcasted_iota(jnp.int32, (T, T), 0)
    col = lax.broadcasted_iota(jnp.int32, (T, T), 1)
    tri = jnp.where((col >= row) if reverse else (col <= row), 1.0, 0.0).astype(BF16)
    reach = (row >= col) if reverse else (row <= col)
    p_hi, p_mid, p_lo = _split3(glf)
    bcum = _dot(tri, p_hi) + _dot(tri, p_mid) + _dot(tri, p_lo)
    g_sh = pltpu.roll(g, M_HEADS, 1)
    r_all = g_sh - bcum
    end = 0 if reverse else T - 1
    b_last = bcum[end:end + 1, :]
    a_all = b_last - bcum + g_sh
    m_prev_row = m_ref[d:d + 1, :]
    m_new_row = jnp.maximum(b_last + m_prev_row, jnp.max(a_all, axis=0, keepdims=True))
    decay_row = jnp.exp(b_last + m_prev_row - m_new_row)
    w_all = jnp.exp(a_all - m_new_row)
    m_ref[d:d + 1, :] = m_new_row
    b_t = bcum.T
    for h in range(M_HEADS):
        cf = base + M_HEADS + h
        st = M_HEADS * d + h
        m_prev = m_prev_row[:, cf:cf + 1]
        decay = decay_row[:, cf:cf + 1]
        r_mat = jnp.where(reach, r_all[:, cf:cf + 1], -jnp.inf)
        mx = jnp.maximum(m_prev, jnp.max(r_mat, axis=0, keepdims=True))
        iw = jnp.exp(m_prev - mx)
        dm = jnp.exp(r_mat - mx)

        q = (q_ref[:, h * M_DQK:(h + 1) * M_DQK] * M_QSCALE).astype(BF16)
        kf = k_ref[:, h * M_DQK:(h + 1) * M_DQK]
        v = v_ref[:, h * M_DV:(h + 1) * M_DV].astype(BF16)
        c_state = c_ref[st]
        n_state = n_ref[st:st + 1, :]
        n_rows = jnp.broadcast_to(n_state, (8, M_DQK)).astype(BF16)

        s = _dot_nt(kf.astype(BF16), q) * dm
        den = iw * _dot_nt(n_rows, q)[0:1, :] + jnp.sum(s, axis=0, keepdims=True)
        inv = 1.0 / jnp.maximum(jnp.abs(den), jnp.exp(-(b_t[cf:cf + 1, :] + mx)))
        ht_ref[h * M_DV:(h + 1) * M_DV, :] = (_dot_tn(v, (s * inv).astype(BF16))
                                              + _dot_nt(c_state.astype(BF16), q) * (iw * inv))

        wk = w_all[:, cf:cf + 1] * kf
        c_ref[st] = decay * c_state + _dot_tn(v, wk.astype(BF16))
        n_ref[st:st + 1, :] = decay * n_state + jnp.sum(wk, axis=0, keepdims=True)


def _mlstm_kernel(qf_ref, kf_ref, vf_ref, gf_ref, qb_ref, kb_ref, vb_ref, gb_ref, bias_ref,
                  hf_ref, hb_ref, c_ref, n_ref, m_ref):
    @pl.when(pl.program_id(0) == 0)
    def _():
        c_ref[...] = jnp.zeros_like(c_ref)
        n_ref[...] = jnp.zeros_like(n_ref)
        m_ref[...] = jnp.zeros_like(m_ref)

    _mlstm_dir(qf_ref, kf_ref, vf_ref, gf_ref, bias_ref, hf_ref, c_ref, n_ref, m_ref, reverse=False)
    _mlstm_dir(qb_ref, kb_ref, vb_ref, gb_ref, bias_ref, hb_ref, c_ref, n_ref, m_ref, reverse=True)


def _mlstm(u, gate_b):
    fwd = lambda k: jnp.where(k == 0, N_LAT_BLK, k - 1)
    bwd = lambda k: jnp.where(k == 0, N_LAT_BLK, N_LAT_BLK - k)
    mg_blk = OD_COL_MG // LANES

    def specs(rmap):
        return [pl.BlockSpec((M_CHUNK, M_QK_W), lambda k: (rmap(k), 0)),
                pl.BlockSpec((M_CHUNK, M_QK_W), lambda k: (rmap(k), 1)),
                pl.BlockSpec((M_CHUNK, M_V_W), lambda k: (rmap(k), 1)),
                pl.BlockSpec((M_CHUNK, LANES), lambda k: (rmap(k), mg_blk))]

    bias = jnp.concatenate([gate_b.astype(F32), jnp.zeros((LANES - 4 * M_HEADS,), F32)]).reshape(1, LANES)
    return pl.pallas_call(
        _mlstm_kernel,
        grid=(N_BLK,),
        in_specs=specs(fwd) + specs(bwd) + [pl.BlockSpec((1, LANES), lambda k: (0, 0))],
        out_specs=[pl.BlockSpec((M_V_W, M_CHUNK), lambda k: (0, fwd(k))),
                   pl.BlockSpec((M_V_W, M_CHUNK), lambda k: (0, bwd(k)))],
        out_shape=[jax.ShapeDtypeStruct((M_V_W, ROWS), F32), jax.ShapeDtypeStruct((M_V_W, ROWS), F32)],
        scratch_shapes=[pltpu.VMEM((2 * M_HEADS, M_DV, M_DQK), F32),
                        pltpu.VMEM((2 * M_HEADS, M_DQK), F32),
                        pltpu.VMEM((8, LANES), F32)],
        compiler_params=pltpu.CompilerParams(dimension_semantics=("arbitrary",)),
        name="mlstm_scan",
    )(u, u, u, u, u, u, u, u, bias)


def _mlstm_post_kernel(hf_ref, hb_ref, o_ref, z_ref, nw_ref, y_ref):
    for h in range(M_HEADS):
        sl = slice(h * M_DV, (h + 1) * M_DV)
        ht = hf_ref[sl, :] + hb_ref[sl, :]
        hn = (ht * lax.rsqrt(jnp.mean(ht * ht, axis=0, keepdims=True) + EPS)).T * nw_ref[:, sl]
        y_ref[:, sl] = (hn * _sigmoid(o_ref[:, sl]) * _silu(z_ref[:, sl])).astype(BF16)


def _mlstm_post(hf, hb, u, head_norm):
    row = lambda i: (i, 0)
    col = lambda i: (0, i)
    return pl.pallas_call(
        _mlstm_post_kernel,
        grid=(N_BLK,),
        in_specs=[pl.BlockSpec((M_V_W, ROW_BLK), col), pl.BlockSpec((M_V_W, ROW_BLK), col),
                  pl.BlockSpec((ROW_BLK, M_V_W), lambda i: (i, OD_COL_MO // M_V_W)),
                  pl.BlockSpec((ROW_BLK, M_V_W), lambda i: (i, OD_COL_MZ // M_V_W)),
                  pl.BlockSpec((1, M_V_W), lambda i: (0, 0))],
        out_specs=pl.BlockSpec((ROW_BLK, M_V_W), row),
        out_shape=jax.ShapeDtypeStruct((ROWS, M_V_W), BF16),
        name="mlstm_post",
    )(hf, hb, u, u, head_norm.reshape(1, M_V_W))


W_NLAT = SEQ // WINDOW
W_NBLK = ROWS // WINDOW
W_QSCALE = (A_HD ** -0.5) * LOG2E
W_KEYS = 3 * WINDOW + CTX_LEN
W_PAIRS = A_HEADS // A_KV_HEADS // 2


def _win_kernel(sink_ref, q_ref, az0_ref, az1_ref, az2_ref, az3_ref, kvp_ref, kvc_ref, kvn_ref, kvx_ref,
                tq_ref, tp_ref, tn_ref, o_ref):
    az_refs = (az0_ref, az1_ref, az2_ref, az3_ref)
    bi = pl.program_id(0)
    is_lat = bi < W_NLAT
    tq = (tq_ref[0], tq_ref[1], tq_ref[2])
    lane_k = lax.broadcasted_iota(jnp.int32, (W_KEYS, LANES), 1)

    kj = lax.broadcasted_iota(jnp.int32, (WINDOW, WINDOW), 0)
    qi = lax.broadcasted_iota(jnp.int32, (WINDOW, WINDOW), 1)
    ok_prev = jnp.logical_and(jnp.logical_and(is_lat, bi >= 1), kj >= qi)
    ok_cur = jnp.logical_and(is_lat, kj >= 0)
    ok_next = jnp.logical_and(jnp.logical_and(is_lat, bi <= W_NLAT - 2), kj <= qi)
    ok_ctx = lax.broadcasted_iota(jnp.int32, (CTX_LEN, WINDOW), 1) >= 0
    valid = jnp.concatenate([ok_prev, ok_cur, ok_next, ok_ctx], axis=0)
    valid4 = jnp.concatenate([valid] * W_PAIRS, axis=1)
    lane4 = lax.broadcasted_iota(jnp.int32, (1, W_PAIRS * WINDOW), 1)

    kk = jnp.concatenate([_rope(kvp_ref[:, 0:LANES], tp_ref[0], tp_ref[1], tp_ref[2]),
                          _rope(kvc_ref[:, 0:LANES], *tq),
                          _rope(kvn_ref[:, 0:LANES], tn_ref[0], tn_ref[1], tn_ref[2]),
                          kvx_ref[:, 0:LANES]], axis=0)
    vv = jnp.concatenate([kvp_ref[:, LANES:2 * LANES], kvc_ref[:, LANES:2 * LANES],
                          kvn_ref[:, LANES:2 * LANES], kvx_ref[:, LANES:2 * LANES]], axis=0)
    kk_sw = pltpu.roll(kk, A_HD, 1)
    vv_sw = pltpu.roll(vv, A_HD, 1)
    low = lane_k < A_HD

    for g in range(A_KV_HEADS):
        k_src = (kk, kk_sw) if g == 0 else (kk_sw, kk)
        v_src = (vv, vv_sw) if g == 0 else (vv_sw, vv)
        k_half = [jnp.where(low, k_src[0], 0.0).astype(BF16), jnp.where(low, 0.0, k_src[1]).astype(BF16)]
        v_half = [jnp.where(low, v_src[0], 0.0).astype(BF16), jnp.where(low, 0.0, v_src[1]).astype(BF16)]
        cols = [(g * W_PAIRS + p) * LANES for p in range(W_PAIRS)]
        qs = jnp.concatenate([(_rope(q_ref[:, c:c + LANES], *tq) * W_QSCALE).astype(BF16) for c in cols],
                             axis=0)
        out_t = jnp.zeros((LANES, W_PAIRS * WINDOW), F32)
        for par in range(2):
            snk = jnp.full((1, W_PAIRS * WINDOW), sink_ref[g * 8 + par] * LOG2E, F32)
            for p in range(1, W_PAIRS):
                snk = jnp.where(lane4 >= p * WINDOW, sink_ref[g * 8 + 2 * p + par] * LOG2E, snk)
            s = jnp.where(valid4, _dot_nt(k_half[par], qs), -jnp.inf)
            m = jnp.maximum(jnp.max(s, axis=0, keepdims=True), snk)
            e = jnp.exp2(s - m)
            den = jnp.sum(e, axis=0, keepdims=True) + jnp.exp2(snk - m)
            out_t = out_t + _dot_tn(v_half[par], e.astype(BF16)) * (1.0 / den)
        out = out_t.T
        for p, c in enumerate(cols):
            grp = c // LANES
            az = az_refs[grp // 2][:, (grp % 2) * LANES:(grp % 2 + 1) * LANES]
            o_ref[:, c:c + LANES] = (out[p * WINDOW:(p + 1) * WINDOW] * _silu(az)).astype(BF16)


def _win_attn(u, sink, tabs):
    kvb = OD_COL_KV // OD_KV_W
    cur = lambda b: jnp.minimum(b, W_NLAT - 1)
    prv = lambda b: jnp.clip(b - 1, 0, W_NLAT - 1)
    nxt = lambda b: jnp.clip(b + 1, 0, W_NLAT - 1)
    tab = lambda f: pl.BlockSpec((3, WINDOW, LANES), lambda b: (0, f(b), 0))
    az_spec = lambda n: pl.BlockSpec((WINDOW, 2 * LANES), lambda b: (b, OD_COL_AZ // (2 * LANES) + n))
    return pl.pallas_call(
        _win_kernel,
        grid=(W_NBLK,),
        in_specs=[pl.BlockSpec(memory_space=pltpu.SMEM),
                  pl.BlockSpec((WINDOW, A_W), lambda b: (b, OD_COL_AQ // A_W)),
                  az_spec(0), az_spec(1), az_spec(2), az_spec(3),
                  pl.BlockSpec((WINDOW, OD_KV_W), lambda b: (prv(b), kvb)),
                  pl.BlockSpec((WINDOW, OD_KV_W), lambda b: (cur(b), kvb)),
                  pl.BlockSpec((WINDOW, OD_KV_W), lambda b: (nxt(b), kvb)),
                  pl.BlockSpec((CTX_LEN, OD_KV_W), lambda b: (SEQ // CTX_LEN, kvb)),
                  tab(lambda b: b), tab(prv), tab(nxt)],
        out_specs=pl.BlockSpec((WINDOW, A_W), lambda b: (b, 0)),
        out_shape=jax.ShapeDtypeStruct((ROWS, A_W), BF16),
        name="window_attention",
    )(sink, u, u, u, u, u, u, u, u, u, tabs, tabs, tabs)


def kernel(x, c, ctx, c_ctx, ada_w, ada_b, norm_w, ev_w_in, ev_q_norm, ev_kv_norm, ev_w_uq, ev_w_ukv,
           ev_w_pool, ev_pool_scale, ev_w_out, od_w_in, od_gate_b, od_head_norm, od_sink, od_w_out, final_norm):
    tabs = _rope_tables()
    mod = _ada(c, c_ctx, ada_w, ada_b)

    n_ev = ev_w_in.shape[0]
    ev_w = _ev_layout(ev_w_in)
    od_w = _od_layout(od_w_in)
    wq = ev_w_uq.astype(BF16).reshape(n_ev, Q_RANK, MLA_HEADS, MLA_QK)
    wq_nope = wq[..., :MLA_NOPE].reshape(n_ev, Q_RANK, MLA_HEADS * MLA_NOPE)
    wq_rope = wq[..., MLA_NOPE:].reshape(n_ev, Q_RANK, MLA_HEADS * ROPE_DIM)
    wkv = ev_w_ukv.astype(BF16).reshape(n_ev, KV_RANK, MLA_HEADS, MLA_NOPE + MLA_VDIM)
    wk_nope = wkv[..., :MLA_NOPE].reshape(n_ev, KV_RANK, MLA_HEADS * MLA_NOPE)
    wv = wkv[..., MLA_NOPE:].reshape(n_ev, KV_RANK, MLA_W)
    q_norm = ev_q_norm.reshape(n_ev, 1, Q_RANK)
    kv_norm = ev_kv_norm.reshape(n_ev, 1, KV_RANK)

    h = _prenorm(x[0], ctx[0], norm_w[0], mod[0])
    res_lat, res_ctx = x[0], ctx[0]
    out = None
    for i in range(DEPTH):
        j = i // 2
        if i % 2 == 0:
            u = _in_proj(h, ev_w, j, EV_TN)
            q = _q_proj(u, j, q_norm, wq_nope, wq_rope, tabs)
            k, v = _kv_proj(u, j, kv_norm, wk_nope, wv, tabs)
            a = _pool(u, ev_w_pool[j], ev_pool_scale[j])
            b = _flash(q, k, v, u)
            w_out = ev_w_out
        else:
            u = _in_proj(h, od_w, j, OD_TN)
            hf, hb = _mlstm(u, od_gate_b[j])
            a = _mlstm_post(hf, hb, u, od_head_norm[j])
            b = _win_attn(u, od_sink[j].astype(F32), tabs)
            w_out = od_w_out
        if i < DEPTH - 1:
            xs, h = _out_proj(a, b, w_out, j, res_lat, res_ctx, mod[i], norm_w[i + 1], mod[i + 1])
            res_lat = res_ctx = xs
        else:
            out = _out_proj(a, b, w_out, j, res_lat, res_ctx, mod[i], final_norm, None)
    return out[None]
```

```python
import functools
import math

import numpy as np
import jax
import jax.numpy as jnp
from jax import lax
from jax.experimental import pallas as pl
from jax.experimental.pallas import tpu as pltpu

F32 = jnp.float32
BF16 = jnp.bfloat16

D_MODEL = 2048
SEQ = 8192
CTX_LEN = 256
ROWS = SEQ + CTX_LEN
DEPTH = 4
GRID_W = 64
EPS = 1e-6
ROPE_THETA = 10000.0
ROPE_DIM = 64
LOG2E = math.log2(math.e)

LANES = 128
ROW_BLK = 256
N_LAT_BLK = SEQ // ROW_BLK
N_BLK = ROWS // ROW_BLK
MM_ROWS = 768
IN_ROWS = 384

POOL_WINDOWS = (2, 4, 8, 16)
POOL_W = 512
POOL_GC = 128
POOL_HALO = 16

MLA_HEADS = 12
MLA_NOPE = 128
MLA_VDIM = 128
MLA_QK = MLA_NOPE + ROPE_DIM
MLA_W = MLA_HEADS * MLA_VDIM
Q_RANK = 512
KV_RANK = 512
HEAD_PAD = 256

M_HEADS = 4
M_DQK = 128
M_DV = 256
M_QK_W = 512
M_V_W = 1024
M_CHUNK = 256

A_HEADS = 16
A_KV_HEADS = 2
A_HD = 64
A_W = 1024
WINDOW = 128

NEG_BIG = -1e30

EV_N = 3840
EV_COL_CQ = 1024
EV_COL_CKV = 1536
EV_COL_AGATE = 2048
EV_COL_KR = 3584
EV_TN = 3840

OD_N = 6656
OD_COL_MO = 2048
OD_COL_MZ = 3072
OD_COL_AQ = 4096
OD_COL_KV = 5120
OD_KV_W = 256
OD_COL_AZ = 5376
OD_COL_MG = 6400
OD_TN = 3328

PREP_W = 256


def _prep_kernel(a_ref, b_ref, o_ref, *, n_copy, n_shift, shift):
    t = pl.program_id(1)

    @pl.when(t < n_copy)
    def _():
        o_ref[...] = a_ref[...].astype(BF16).T

    @pl.when(jnp.logical_and(t >= n_copy, t < n_copy + n_shift))
    def _():
        o_ref[...] = jnp.concatenate([a_ref[shift:, :], b_ref[...]], axis=0).astype(BF16).T

    @pl.when(t == n_copy + n_shift)
    def _():
        pad = jnp.zeros((PREP_W - shift, b_ref.shape[1]), F32)
        o_ref[...] = jnp.concatenate([b_ref[...], pad], axis=0).astype(BF16).T


def _prep_weight(w, n_copy, n_shift, shift):
    layers, d, n = w.shape
    n_tiles = n_copy + n_shift + 1
    per = PREP_W // shift
    a_map = lambda l, t: (l, jnp.minimum(t, n_copy + n_shift - 1), 0)
    b_map = lambda l, t: (l, jnp.where(t == n_tiles - 1, n_copy * per, jnp.minimum((t + 1) * per, n // shift - 1)), 0)
    return pl.pallas_call(
        functools.partial(_prep_kernel, n_copy=n_copy, n_shift=n_shift, shift=shift),
        grid=(layers, n_tiles),
        in_specs=[pl.BlockSpec((None, PREP_W, d), a_map),
                  pl.BlockSpec((None, shift, d), b_map)],
        out_specs=pl.BlockSpec((None, d, PREP_W), lambda l, t: (l, 0, t)),
        out_shape=jax.ShapeDtypeStruct((layers, d, n_tiles * PREP_W), BF16),
        name="weight_layout",
    )(jnp.swapaxes(w, 1, 2), jnp.swapaxes(w, 1, 2))


def _ev_layout(w):
    return _prep_weight(w, n_copy=8, n_shift=6, shift=ROPE_DIM)


def _od_layout(w):
    return _prep_weight(w, n_copy=12, n_shift=13, shift=4 * M_HEADS)


def _rope_tables():
    t = jnp.arange(SEQ)
    row = (t // GRID_W).astype(F32)
    col = (t % GRID_W).astype(F32)
    half = ROPE_DIM // 2
    inv = ROPE_THETA ** (-jnp.arange(0, half, 2, dtype=F32) / half)
    ar = row[:, None] * inv
    ac = col[:, None] * inv
    zero = jnp.zeros_like(ar)
    cos64 = jnp.concatenate([jnp.cos(ar), jnp.cos(ar), jnp.cos(ac), jnp.cos(ac)], axis=-1)
    sa64 = jnp.concatenate([-jnp.sin(ar), zero, -jnp.sin(ac), zero], axis=-1)
    sb64 = jnp.concatenate([zero, jnp.sin(ar), zero, jnp.sin(ac)], axis=-1)
    ctx_zero = jnp.zeros((CTX_LEN, LANES), F32)
    cos = jnp.concatenate([jnp.tile(cos64, (1, 2)), jnp.ones((CTX_LEN, LANES), F32)], axis=0)
    sin_a = jnp.concatenate([jnp.tile(sa64, (1, 2)), ctx_zero], axis=0)
    sin_b = jnp.concatenate([jnp.tile(sb64, (1, 2)), ctx_zero], axis=0)
    return jnp.stack([cos, sin_a, sin_b])


def _rope(x, cos, sin_a, sin_b):
    up = pltpu.roll(x, LANES - ROPE_DIM // 4, 1)
    down = pltpu.roll(x, ROPE_DIM // 4, 1)
    return x * cos + up * sin_a + down * sin_b


def _dot(a, b):
    return jnp.dot(a, b, preferred_element_type=F32)


def _dot_nt(a, b):
    return lax.dot_general(a, b, (((1,), (1,)), ((), ())), preferred_element_type=F32)


def _dot_tn(a, b):
    return lax.dot_general(a, b, (((0,), (0,)), ((), ())), preferred_element_type=F32)


def _rms(x):
    return x * lax.rsqrt(jnp.mean(x * x, axis=-1, keepdims=True) + EPS)


def _modnorm(x, nw, mod):
    return (_rms(x) * nw) * (1.0 + mod[1:2, :]) + mod[0:1, :]


def _silu(x):
    return x * (1.0 / (1.0 + jnp.exp(-x)))


def _sigmoid(x):
    return 1.0 / (1.0 + jnp.exp(-x))


def _split3(x):
    hi = x.astype(BF16)
    r1 = x - hi.astype(F32)
    mid = r1.astype(BF16)
    lo = (r1 - mid.astype(F32)).astype(BF16)
    return hi, mid, lo


def _is_ctx_blk(i):
    return jnp.where(i < N_LAT_BLK, 0, 1)


def _ada_kernel(s_ref, w_ref, b_ref, o_ref):
    s = _silu(s_ref[...])
    hi = s.astype(BF16)
    lo = (s - hi.astype(F32)).astype(BF16)
    lhs = jnp.concatenate([hi, lo], axis=0)
    r = _dot(lhs, w_ref[0].astype(BF16))
    o_ref[0] = r[0:8] + r[8:16] + b_ref[0]


def _ada(c, c_ctx, ada_w, ada_b):
    tn = 1536
    s = jnp.concatenate([c.reshape(1, D_MODEL), c_ctx.reshape(1, D_MODEL), jnp.zeros((6, D_MODEL), F32)], axis=0)
    out = pl.pallas_call(
        _ada_kernel,
        grid=(DEPTH, 3 * D_MODEL // tn),
        in_specs=[pl.BlockSpec((8, D_MODEL), lambda l, j: (0, 0)),
                  pl.BlockSpec((1, D_MODEL, tn), lambda l, j: (l, 0, j)),
                  pl.BlockSpec((1, 1, tn), lambda l, j: (l, 0, j))],
        out_specs=pl.BlockSpec((1, 8, tn), lambda l, j: (l, 0, j)),
        out_shape=jax.ShapeDtypeStruct((DEPTH, 8, 3 * D_MODEL), F32),
        name="ada_mod",
    )(s, ada_w, ada_b.reshape(DEPTH, 1, 3 * D_MODEL))
    m = out[:, :2, :].reshape(DEPTH, 2, 3, D_MODEL)
    return jnp.concatenate([m, jnp.zeros((DEPTH, 2, 5, D_MODEL), F32)], axis=2)


def _split_rows_specs():
    return [pl.BlockSpec((ROW_BLK, D_MODEL), lambda i: (jnp.minimum(i, N_LAT_BLK - 1), 0)),
            pl.BlockSpec((CTX_LEN, D_MODEL), lambda i: (0, 0))]


def _split_rows(x_ref, c_ref, blk):
    return jnp.where(blk < N_LAT_BLK, x_ref[...], c_ref[...])


def _prenorm_kernel(x_ref, c_ref, nw_ref, mod_ref, h_ref):
    x = _split_rows(x_ref, c_ref, pl.program_id(0))
    h_ref[...] = _modnorm(x, nw_ref[...], mod_ref[0]).astype(BF16)


def _prenorm(x, ctx, nw, mod):
    return pl.pallas_call(
        _prenorm_kernel,
        grid=(N_BLK,),
        in_specs=_split_rows_specs() + [pl.BlockSpec((1, D_MODEL), lambda i: (0, 0)),
                                        pl.BlockSpec((1, 8, D_MODEL), lambda i: (_is_ctx_blk(i), 0, 0))],
        out_specs=pl.BlockSpec((ROW_BLK, D_MODEL), lambda i: (i, 0)),
        out_shape=jax.ShapeDtypeStruct((ROWS, D_MODEL), BF16),
        name="prenorm",
    )(x, ctx, nw.reshape(1, D_MODEL), mod)


def _mm_kernel(h_ref, w_ref, o_ref):
    o_ref[...] = _dot(h_ref[...], w_ref[...])


def _in_proj(h, w, layer, tn):
    n = w.shape[2]
    return pl.pallas_call(
        _mm_kernel,
        grid=(n // tn, ROWS // IN_ROWS),
        in_specs=[pl.BlockSpec((IN_ROWS, D_MODEL), lambda j, i: (i, 0)),
                  pl.BlockSpec((None, D_MODEL, tn), lambda j, i: (layer, 0, j))],
        out_specs=pl.BlockSpec((IN_ROWS, tn), lambda j, i: (i, j)),
        out_shape=jax.ShapeDtypeStruct((ROWS, n), F32),
        name="in_proj",
    )(h, w)


Q_SCALE = (MLA_QK ** -0.5) * LOG2E


def _qproj_kernel(cq_ref, nw_ref, wn_ref, wr_ref, tab_ref, q_ref):
    xn = (_rms(cq_ref[...]) * nw_ref[...]).astype(BF16)
    qn = _dot(xn, wn_ref[...]) * Q_SCALE
    for h in range(MLA_HEADS):
        q_ref[:, h * HEAD_PAD:h * HEAD_PAD + LANES] = qn[:, h * LANES:(h + 1) * LANES].astype(BF16)
    qr = _dot(xn, wr_ref[...])
    for p in range(MLA_HEADS // 2):
        rp = (_rope(qr[:, p * LANES:(p + 1) * LANES], tab_ref[0], tab_ref[1], tab_ref[2]) * Q_SCALE).astype(BF16)
        for h in (2 * p, 2 * p + 1):
            q_ref[:, h * HEAD_PAD + LANES:(h + 1) * HEAD_PAD] = rp


def _layer_spec(w, layer):
    return pl.BlockSpec((None,) + w.shape[1:], lambda i: (layer, 0, 0))


def _q_proj(u, layer, nw, wn, wr, tabs):
    t = MM_ROWS
    return pl.pallas_call(
        _qproj_kernel,
        grid=(ROWS // t,),
        in_specs=[pl.BlockSpec((t, Q_RANK), lambda i: (i, EV_COL_CQ // Q_RANK)),
                  _layer_spec(nw, layer), _layer_spec(wn, layer), _layer_spec(wr, layer),
                  pl.BlockSpec((3, t, LANES), lambda i: (0, i, 0))],
        out_specs=pl.BlockSpec((t, MLA_HEADS * HEAD_PAD), lambda i: (i, 0)),
        out_shape=jax.ShapeDtypeStruct((ROWS, MLA_HEADS * HEAD_PAD), BF16),
        name="mla_q_proj",
    )(u, nw, wn, wr, tabs)


def _kvproj_kernel(ckv_ref, kr_ref, nw_ref, wk_ref, wv_ref, tab_ref, k_ref, v_ref):
    xn = (_rms(ckv_ref[...]) * nw_ref[...]).astype(BF16)
    kn = _dot(xn, wk_ref[...])
    vv = _dot(xn, wv_ref[...])
    kr2 = _rope(kr_ref[...], tab_ref[0], tab_ref[1], tab_ref[2])
    lane = lax.broadcasted_iota(jnp.int32, kr2.shape, 1)
    kr_lo = jnp.where(lane < ROPE_DIM, kr2, 0.0)
    k_lo = kr_lo.astype(BF16)
    k_hi = pltpu.roll(kr_lo, ROPE_DIM, 1).astype(BF16)
    ones_col = jnp.where(lane == 0, 1.0, 0.0).astype(BF16)
    for h in range(MLA_HEADS):
        sl = slice(h * LANES, (h + 1) * LANES)
        k_ref[:, h * HEAD_PAD:h * HEAD_PAD + LANES] = kn[:, sl].astype(BF16)
        k_ref[:, h * HEAD_PAD + LANES:(h + 1) * HEAD_PAD] = k_lo if h % 2 == 0 else k_hi
        v_ref[:, h * HEAD_PAD:h * HEAD_PAD + LANES] = vv[:, sl].astype(BF16)
        v_ref[:, h * HEAD_PAD + LANES:(h + 1) * HEAD_PAD] = ones_col


def _kv_proj(u, layer, nw, wk, wv, tabs):
    t = MM_ROWS
    wide = MLA_HEADS * HEAD_PAD
    return pl.pallas_call(
        _kvproj_kernel,
        grid=(ROWS // t,),
        in_specs=[pl.BlockSpec((t, KV_RANK), lambda i: (i, EV_COL_CKV // KV_RANK)),
                  pl.BlockSpec((t, LANES), lambda i: (i, EV_COL_KR // LANES)),
                  _layer_spec(nw, layer), _layer_spec(wk, layer), _layer_spec(wv, layer),
                  pl.BlockSpec((3, t, LANES), lambda i: (0, i, 0))],
        out_specs=[pl.BlockSpec((t, wide), lambda i: (i, 0)),
                   pl.BlockSpec((t, wide), lambda i: (i, 0))],
        out_shape=[jax.ShapeDtypeStruct((ROWS, wide), BF16),
                   jax.ShapeDtypeStruct((ROWS, wide), BF16)],
        name="mla_kv_proj",
    )(u, u, nw, wk, wv, tabs)


FL_TQ = 768
FL_TK = 768
FL_LAST = ROWS // FL_TQ - 1
FL_LAT_IN_LAST = SEQ - FL_LAST * FL_TQ


def _key_chunks():
    return [(i * FL_TK, FL_TK) for i in range(ROWS // FL_TK)]


def _attend(q, k_ref, v_ref, sa_ref, sb_ref, chunks):
    rows = q.shape[0]
    bufs = (sa_ref, sb_ref)

    def scores_into(j):
        off, size = chunks[j]
        s = _dot_nt(q, k_ref[off:off + size, :])
        bufs[j % 2][0:rows, 0:size] = s
        return jnp.max(s, axis=1, keepdims=True)

    def update(j, m_chunk, m_prev, acc):
        off, size = chunks[j]
        m_new = jnp.maximum(m_prev, m_chunk)
        alpha = jnp.exp2(m_prev - m_new)
        p = jnp.exp2(bufs[j % 2][0:rows, 0:size] - m_new).astype(BF16)
        return m_new, alpha * acc + _dot(p, v_ref[off:off + size, :])

    m = jnp.full((rows, 1), NEG_BIG, F32)
    acc = jnp.zeros((rows, HEAD_PAD), F32)
    m_next = scores_into(0)
    for j in range(len(chunks)):
        m_chunk = m_next
        if j + 1 < len(chunks):
            m_next = scores_into(j + 1)
        m, acc = update(j, m_chunk, m, acc)
    return acc[:, :MLA_VDIM] / acc[:, MLA_VDIM:MLA_VDIM + 1]


def _flash_kernel(q_ref, k_ref, v_ref, g_ref, o_ref, sa_ref, sb_ref):
    i = pl.program_id(1)
    chunks = _key_chunks()

    @pl.when(i < FL_LAST)
    def _():
        o = _attend(q_ref[...], k_ref, v_ref, sa_ref, sb_ref, chunks)
        o_ref[...] = (o * _silu(g_ref[...])).astype(BF16)

    @pl.when(i == FL_LAST)
    def _():
        nl = FL_LAT_IN_LAST
        o = _attend(q_ref[0:nl, :], k_ref, v_ref, sa_ref, sb_ref, chunks)
        o_ref[0:nl, :] = (o * _silu(g_ref[0:nl, :])).astype(BF16)
        oc = _attend(q_ref[nl:, :], k_ref, v_ref, sa_ref, sb_ref, [(SEQ, CTX_LEN)])
        o_ref[nl:, :] = (oc * _silu(g_ref[nl:, :])).astype(BF16)


def _flash(q, k, v, u):
    return pl.pallas_call(
        _flash_kernel,
        grid=(MLA_HEADS, ROWS // FL_TQ),
        in_specs=[pl.BlockSpec((FL_TQ, HEAD_PAD), lambda h, i: (i, h)),
                  pl.BlockSpec((ROWS, HEAD_PAD), lambda h, i: (0, h)),
                  pl.BlockSpec((ROWS, HEAD_PAD), lambda h, i: (0, h)),
                  pl.BlockSpec((FL_TQ, MLA_VDIM), lambda h, i: (i, EV_COL_AGATE // MLA_VDIM + h))],
        out_specs=pl.BlockSpec((FL_TQ, MLA_VDIM), lambda h, i: (i, h)),
        out_shape=jax.ShapeDtypeStruct((ROWS, MLA_W), BF16),
        scratch_shapes=[pltpu.VMEM((FL_TQ, FL_TK), F32), pltpu.VMEM((FL_TQ, FL_TK), F32)],
        name="mla_attention",
    )(q, k, v, u)


def _pool_kernel(prev_ref, cur_ref, next_ref, gate_ref, w_ref, sc_ref, o_ref):
    i = pl.program_id(0)
    first = jnp.logical_or(i == 0, i == N_LAT_BLK)
    last = i >= N_LAT_BLK - 1
    lo_lim = jnp.where(first, 0, -POOL_HALO)
    hi_lim = jnp.where(last, ROW_BLK - 1, ROW_BLK - 1 + POOL_HALO)
    src_rows = ROW_BLK + 2 * POOL_HALO
    t = lax.broadcasted_iota(jnp.int32, (ROW_BLK, src_rows), 0)
    j = lax.broadcasted_iota(jnp.int32, (ROW_BLK, src_rows), 1) - POOL_HALO
    t1 = lax.broadcasted_iota(jnp.int32, (ROW_BLK, 1), 0)
    for g, w in enumerate(POOL_WINDOWS):
        sl = slice(g * POOL_GC, (g + 1) * POOL_GC)
        cur = cur_ref[:, sl]
        src = jnp.concatenate([prev_ref[ROW_BLK - POOL_HALO:, sl], cur, next_ref[:POOL_HALO, sl]], axis=0)
        lo = jnp.maximum(t - w // 2, lo_lim)
        hi = jnp.minimum(t + w // 2 - 1, hi_lim)
        band = jnp.where(jnp.logical_and(j >= lo, j <= hi), 1.0, 0.0).astype(BF16)
        cnt = (jnp.minimum(t1 + w // 2 - 1, hi_lim) - jnp.maximum(t1 - w // 2, lo_lim) + 1).astype(F32)
        s_hi, s_mid, s_lo = _split3(src)
        wsum = _dot(band, s_hi) + _dot(band, s_mid) + _dot(band, s_lo)
        z = (wsum / cnt - cur).astype(BF16)
        y = _dot(z, w_ref[g]) * sc_ref[:, sl]
        o_ref[:, sl] = (y * _silu(gate_ref[:, sl])).astype(BF16)


def _pool(u, w_pool, pool_scale):
    def prev_map(i):
        return (jnp.where(i < N_LAT_BLK, jnp.maximum(i - 1, 0), N_LAT_BLK), 0)

    def next_map(i):
        return (jnp.where(i < N_LAT_BLK, jnp.minimum(i + 1, N_LAT_BLK - 1), N_LAT_BLK), 0)

    return pl.pallas_call(
        _pool_kernel,
        grid=(N_BLK,),
        in_specs=[pl.BlockSpec((ROW_BLK, POOL_W), prev_map),
                  pl.BlockSpec((ROW_BLK, POOL_W), lambda i: (i, 0)),
                  pl.BlockSpec((ROW_BLK, POOL_W), next_map),
                  pl.BlockSpec((ROW_BLK, POOL_W), lambda i: (i, 1)),
                  pl.BlockSpec((4, POOL_GC, POOL_GC), lambda i: (0, 0, 0)),
                  pl.BlockSpec((1, POOL_W), lambda i: (0, 0))],
        out_specs=pl.BlockSpec((ROW_BLK, POOL_W), lambda i: (i, 0)),
        out_shape=jax.ShapeDtypeStruct((ROWS, POOL_W), BF16),
        name="pool_mix",
    )(u, u, u, u, w_pool.astype(BF16), pool_scale.reshape(1, POOL_W))


def _outproj_residual(a_ref, b_ref, w_ref, x_ref, c_ref, mod_ref, wbf_ref):
    i = pl.program_id(0)

    @pl.when(i == 0)
    def _():
        wbf_ref[...] = w_ref[...].astype(BF16)

    k1 = a_ref.shape[1]
    y = _dot(a_ref[...], wbf_ref[0:k1, :]) + _dot(b_ref[...], wbf_ref[k1:, :])
    return _split_rows(x_ref, c_ref, i) + mod_ref[0, 2:3, :] * y


def _outproj_kernel(a_ref, b_ref, w_ref, x_ref, c_ref, mod_ref, nw_ref, modn_ref, xo_ref, h_ref, wbf_ref):
    xn = _outproj_residual(a_ref, b_ref, w_ref, x_ref, c_ref, mod_ref, wbf_ref)
    xo_ref[...] = xn
    h_ref[...] = _modnorm(xn, nw_ref[...], modn_ref[0]).astype(BF16)


def _outproj_final_kernel(a_ref, b_ref, w_ref, x_ref, c_ref, mod_ref, nw_ref, o_ref, wbf_ref):
    xn = _outproj_residual(a_ref, b_ref, w_ref, x_ref, c_ref, mod_ref, wbf_ref)
    o_ref[...] = _rms(xn) * nw_ref[...]


def _out_proj(a, b, w, layer, x, x_ctx, mod, nw_next, mod_next):
    k1, k2 = a.shape[1], b.shape[1]
    n_blk = N_LAT_BLK if mod_next is None else N_BLK
    row = lambda i: (i, 0)
    full = lambda i: (0, 0)
    stream = lambda i: (_is_ctx_blk(i), 0, 0)
    ctx_blk = x_ctx.shape[0] // CTX_LEN - 1
    common = [pl.BlockSpec((ROW_BLK, k1), row), pl.BlockSpec((ROW_BLK, k2), row),
              pl.BlockSpec((None, D_MODEL, D_MODEL), lambda i: (layer, 0, 0), pipeline_mode=pl.Buffered(1)),
              pl.BlockSpec((ROW_BLK, D_MODEL), lambda i: (jnp.minimum(i, N_LAT_BLK - 1), 0)),
              pl.BlockSpec((CTX_LEN, D_MODEL), lambda i: (ctx_blk, 0)),
              pl.BlockSpec((1, 8, D_MODEL), stream),
              pl.BlockSpec((1, D_MODEL), full)]
    scratch = [pltpu.VMEM((D_MODEL, D_MODEL), BF16)]
    params = pltpu.CompilerParams(dimension_semantics=("arbitrary",))
    if mod_next is None:
        return pl.pallas_call(
            _outproj_final_kernel,
            grid=(n_blk,),
            in_specs=common,
            out_specs=pl.BlockSpec((ROW_BLK, D_MODEL), row),
            out_shape=jax.ShapeDtypeStruct((SEQ, D_MODEL), F32),
            scratch_shapes=scratch,
            compiler_params=params,
            name="out_proj_final",
        )(a, b, w, x, x_ctx, mod, nw_next.reshape(1, D_MODEL))
    return pl.pallas_call(
        _outproj_kernel,
        grid=(n_blk,),
        in_specs=common + [pl.BlockSpec((1, 8, D_MODEL), stream)],
        out_specs=[pl.BlockSpec((ROW_BLK, D_MODEL), row), pl.BlockSpec((ROW_BLK, D_MODEL), row)],
        out_shape=[jax.ShapeDtypeStruct((ROWS, D_MODEL), F32), jax.ShapeDtypeStruct((ROWS, D_MODEL), BF16)],
        scratch_shapes=scratch,
        compiler_params=params,
        name="out_proj",
    )(a, b, w, x, x_ctx, mod, nw_next.reshape(1, D_MODEL), mod_next)


M_QSCALE = M_DQK ** -0.5


def _log_sigmoid(x):
    return jnp.minimum(x, 0.0) - jnp.log(1.0 + jnp.exp(-jnp.abs(x)))


def _mlstm_dir(q_ref, k_ref, v_ref, g_ref, gb_ref, ht_ref, c_ref, n_ref, m_ref, reverse):
    T = M_CHUNK
    d = 1 if reverse else 0
    base = 2 * M_HEADS * d
    g = g_ref[...] + gb_ref[...]
    glf = _log_sigmoid(g)
    row = lax.broadcasted_iota(jnp.int32, (T, T), 0)
    col = lax.broadcasted_iota(jnp.int32, (T, T), 1)
    tri = jnp.where((col >= row) if reverse else (col <= row), 1.0, 0.0).astype(BF16)
    reach = (row >= col) if reverse else (row <= col)
    p_hi, p_mid, p_lo = _split3(glf)
    bcum = _dot(tri, p_hi) + _dot(tri, p_mid) + _dot(tri, p_lo)
    g_sh = pltpu.roll(g, M_HEADS, 1)
    r_all = g_sh - bcum
    end = 0 if reverse else T - 1
    b_last = bcum[end:end + 1, :]
    a_all = b_last - bcum + g_sh
    m_prev_row = m_ref[d:d + 1, :]
    m_new_row = jnp.maximum(b_last + m_prev_row, jnp.max(a_all, axis=0, keepdims=True))
    decay_row = jnp.exp(b_last + m_prev_row - m_new_row)
    w_all = jnp.exp(a_all - m_new_row)
    m_ref[d:d + 1, :] = m_new_row
    b_t = bcum.T
    for h in range(M_HEADS):
        cf = base + M_HEADS + h
        st = M_HEADS * d + h
        m_prev = m_prev_row[:, cf:cf + 1]
        decay = decay_row[:, cf:cf + 1]
        r_mat = jnp.where(reach, r_all[:, cf:cf + 1], -jnp.inf)
        mx = jnp.maximum(m_prev, jnp.max(r_mat, axis=0, keepdims=True))
        iw = jnp.exp(m_prev - mx)
        dm = jnp.exp(r_mat - mx)

        q = (q_ref[:, h * M_DQK:(h + 1) * M_DQK] * M_QSCALE).astype(BF16)
        kf = k_ref[:, h * M_DQK:(h + 1) * M_DQK]
        v = v_ref[:, h * M_DV:(h + 1) * M_DV].astype(BF16)
        c_state = c_ref[st]
        n_state = n_ref[st:st + 1, :]
        n_rows = jnp.broadcast_to(n_state, (8, M_DQK)).astype(BF16)

        s = _dot_nt(kf.astype(BF16), q) * dm
        den = iw * _dot_nt(n_rows, q)[0:1, :] + jnp.sum(s, axis=0, keepdims=True)
        inv = 1.0 / jnp.maximum(jnp.abs(den), jnp.exp(-(b_t[cf:cf + 1, :] + mx)))
        ht_ref[h * M_DV:(h + 1) * M_DV, :] = (_dot_tn(v, (s * inv).astype(BF16))
                                              + _dot_nt(c_state.astype(BF16), q) * (iw * inv))

        wk = w_all[:, cf:cf + 1] * kf
        c_ref[st] = decay * c_state + _dot_tn(v, wk.astype(BF16))
        n_ref[st:st + 1, :] = decay * n_state + jnp.sum(wk, axis=0, keepdims=True)


def _mlstm_kernel(qf_ref, kf_ref, vf_ref, gf_ref, qb_ref, kb_ref, vb_ref, gb_ref, bias_ref,
                  hf_ref, hb_ref, c_ref, n_ref, m_ref):
    @pl.when(pl.program_id(0) == 0)
    def _():
        c_ref[...] = jnp.zeros_like(c_ref)
        n_ref[...] = jnp.zeros_like(n_ref)
        m_ref[...] = jnp.zeros_like(m_ref)

    _mlstm_dir(qf_ref, kf_ref, vf_ref, gf_ref, bias_ref, hf_ref, c_ref, n_ref, m_ref, reverse=False)
    _mlstm_dir(qb_ref, kb_ref, vb_ref, gb_ref, bias_ref, hb_ref, c_ref, n_ref, m_ref, reverse=True)


def _mlstm(u, gate_b):
    fwd = lambda k: jnp.where(k == 0, N_LAT_BLK, k - 1)
    bwd = lambda k: jnp.where(k == 0, N_LAT_BLK, N_LAT_BLK - k)
    mg_blk = OD_COL_MG // LANES

    def specs(rmap):
        return [pl.BlockSpec((M_CHUNK, M_QK_W), lambda k: (rmap(k), 0)),
                pl.BlockSpec((M_CHUNK, M_QK_W), lambda k: (rmap(k), 1)),
                pl.BlockSpec((M_CHUNK, M_V_W), lambda k: (rmap(k), 1)),
                pl.BlockSpec((M_CHUNK, LANES), lambda k: (rmap(k), mg_blk))]

    bias = jnp.concatenate([gate_b.astype(F32), jnp.zeros((LANES - 4 * M_HEADS,), F32)]).reshape(1, LANES)
    return pl.pallas_call(
        _mlstm_kernel,
        grid=(N_BLK,),
        in_specs=specs(fwd) + specs(bwd) + [pl.BlockSpec((1, LANES), lambda k: (0, 0))],
        out_specs=[pl.BlockSpec((M_V_W, M_CHUNK), lambda k: (0, fwd(k))),
                   pl.BlockSpec((M_V_W, M_CHUNK), lambda k: (0, bwd(k)))],
        out_shape=[jax.ShapeDtypeStruct((M_V_W, ROWS), F32), jax.ShapeDtypeStruct((M_V_W, ROWS), F32)],
        scratch_shapes=[pltpu.VMEM((2 * M_HEADS, M_DV, M_DQK), F32),
                        pltpu.VMEM((2 * M_HEADS, M_DQK), F32),
                        pltpu.VMEM((8, LANES), F32)],
        compiler_params=pltpu.CompilerParams(dimension_semantics=("arbitrary",)),
        name="mlstm_scan",
    )(u, u, u, u, u, u, u, u, bias)


def _mlstm_post_kernel(hf_ref, hb_ref, o_ref, z_ref, nw_ref, y_ref):
    for h in range(M_HEADS):
        sl = slice(h * M_DV, (h + 1) * M_DV)
        ht = hf_ref[sl, :] + hb_ref[sl, :]
        hn = (ht * lax.rsqrt(jnp.mean(ht * ht, axis=0, keepdims=True) + EPS)).T * nw_ref[:, sl]
        y_ref[:, sl] = (hn * _sigmoid(o_ref[:, sl]) * _silu(z_ref[:, sl])).astype(BF16)


def _mlstm_post(hf, hb, u, head_norm):
    row = lambda i: (i, 0)
    col = lambda i: (0, i)
    return pl.pallas_call(
        _mlstm_post_kernel,
        grid=(N_BLK,),
        in_specs=[pl.BlockSpec((M_V_W, ROW_BLK), col), pl.BlockSpec((M_V_W, ROW_BLK), col),
                  pl.BlockSpec((ROW_BLK, M_V_W), lambda i: (i, OD_COL_MO // M_V_W)),
                  pl.BlockSpec((ROW_BLK, M_V_W), lambda i: (i, OD_COL_MZ // M_V_W)),
                  pl.BlockSpec((1, M_V_W), lambda i: (0, 0))],
        out_specs=pl.BlockSpec((ROW_BLK, M_V_W), row),
        out_shape=jax.ShapeDtypeStruct((ROWS, M_V_W), BF16),
        name="mlstm_post",
    )(hf, hb, u, u, head_norm.reshape(1, M_V_W))


W_NLAT = SEQ // WINDOW
W_NBLK = ROWS // WINDOW
W_QSCALE = (A_HD ** -0.5) * LOG2E
W_KEYS = 3 * WINDOW + CTX_LEN
W_PAIRS = A_HEADS // A_KV_HEADS // 2


def _win_kernel(sink_ref, q_ref, az0_ref, az1_ref, az2_ref, az3_ref, kvp_ref, kvc_ref, kvn_ref, kvx_ref,
                tq_ref, tp_ref, tn_ref, o_ref):
    az_refs = (az0_ref, az1_ref, az2_ref, az3_ref)
    bi = pl.program_id(0)
    is_lat = bi < W_NLAT
    tq = (tq_ref[0], tq_ref[1], tq_ref[2])
    lane_k = lax.broadcasted_iota(jnp.int32, (W_KEYS, LANES), 1)

    kj = lax.broadcasted_iota(jnp.int32, (WINDOW, WINDOW), 0)
    qi = lax.broadcasted_iota(jnp.int32, (WINDOW, WINDOW), 1)
    ok_prev = jnp.logical_and(jnp.logical_and(is_lat, bi >= 1), kj >= qi)
    ok_cur = jnp.logical_and(is_lat, kj >= 0)
    ok_next = jnp.logical_and(jnp.logical_and(is_lat, bi <= W_NLAT - 2), kj <= qi)
    ok_ctx = lax.broadcasted_iota(jnp.int32, (CTX_LEN, WINDOW), 1) >= 0
    valid = jnp.concatenate([ok_prev, ok_cur, ok_next, ok_ctx], axis=0)
    valid4 = jnp.concatenate([valid] * W_PAIRS, axis=1)
    lane4 = lax.broadcasted_iota(jnp.int32, (1, W_PAIRS * WINDOW), 1)

    kk = jnp.concatenate([_rope(kvp_ref[:, 0:LANES], tp_ref[0], tp_ref[1], tp_ref[2]),
                          _rope(kvc_ref[:, 0:LANES], *tq),
                          _rope(kvn_ref[:, 0:LANES], tn_ref[0], tn_ref[1], tn_ref[2]),
                          kvx_ref[:, 0:LANES]], axis=0)
    vv = jnp.concatenate([kvp_ref[:, LANES:2 * LANES], kvc_ref[:, LANES:2 * LANES],
                          kvn_ref[:, LANES:2 * LANES], kvx_ref[:, LANES:2 * LANES]], axis=0)
    kk_sw = pltpu.roll(kk, A_HD, 1)
    vv_sw = pltpu.roll(vv, A_HD, 1)
    low = lane_k < A_HD

    for g in range(A_KV_HEADS):
        k_src = (kk, kk_sw) if g == 0 else (kk_sw, kk)
        v_src = (vv, vv_sw) if g == 0 else (vv_sw, vv)
        k_half = [jnp.where(low, k_src[0], 0.0).astype(BF16), jnp.where(low, 0.0, k_src[1]).astype(BF16)]
        v_half = [jnp.where(low, v_src[0], 0.0).astype(BF16), jnp.where(low, 0.0, v_src[1]).astype(BF16)]
        cols = [(g * W_PAIRS + p) * LANES for p in range(W_PAIRS)]
        qs = jnp.concatenate([(_rope(q_ref[:, c:c + LANES], *tq) * W_QSCALE).astype(BF16) for c in cols],
                             axis=0)
        out_t = jnp.zeros((LANES, W_PAIRS * WINDOW), F32)
        for par in range(2):
            snk = jnp.full((1, W_PAIRS * WINDOW), sink_ref[g * 8 + par] * LOG2E, F32)
            for p in range(1, W_PAIRS):
                snk = jnp.where(lane4 >= p * WINDOW, sink_ref[g * 8 + 2 * p + par] * LOG2E, snk)
            s = jnp.where(valid4, _dot_nt(k_half[par], qs), -jnp.inf)
            m = jnp.maximum(jnp.max(s, axis=0, keepdims=True), snk)
            e = jnp.exp2(s - m)
            den = jnp.sum(e, axis=0, keepdims=True) + jnp.exp2(snk - m)
            out_t = out_t + _dot_tn(v_half[par], e.astype(BF16)) * (1.0 / den)
        out = out_t.T
        for p, c in enumerate(cols):
            grp = c // LANES
            az = az_refs[grp // 2][:, (grp % 2) * LANES:(grp % 2 + 1) * LANES]
            o_ref[:, c:c + LANES] = (out[p * WINDOW:(p + 1) * WINDOW] * _silu(az)).astype(BF16)


def _win_attn(u, sink, tabs):
    kvb = OD_COL_KV // OD_KV_W
    cur = lambda b: jnp.minimum(b, W_NLAT - 1)
    prv = lambda b: jnp.clip(b - 1, 0, W_NLAT - 1)
    nxt = lambda b: jnp.clip(b + 1, 0, W_NLAT - 1)
    tab = lambda f: pl.BlockSpec((3, WINDOW, LANES), lambda b: (0, f(b), 0))
    az_spec = lambda n: pl.BlockSpec((WINDOW, 2 * LANES), lambda b: (b, OD_COL_AZ // (2 * LANES) + n))
    return pl.pallas_call(
        _win_kernel,
        grid=(W_NBLK,),
        in_specs=[pl.BlockSpec(memory_space=pltpu.SMEM),
                  pl.BlockSpec((WINDOW, A_W), lambda b: (b, OD_COL_AQ // A_W)),
                  az_spec(0), az_spec(1), az_spec(2), az_spec(3),
                  pl.BlockSpec((WINDOW, OD_KV_W), lambda b: (prv(b), kvb)),
                  pl.BlockSpec((WINDOW, OD_KV_W), lambda b: (cur(b), kvb)),
                  pl.BlockSpec((WINDOW, OD_KV_W), lambda b: (nxt(b), kvb)),
                  pl.BlockSpec((CTX_LEN, OD_KV_W), lambda b: (SEQ // CTX_LEN, kvb)),
                  tab(lambda b: b), tab(prv), tab(nxt)],
        out_specs=pl.BlockSpec((WINDOW, A_W), lambda b: (b, 0)),
        out_shape=jax.ShapeDtypeStruct((ROWS, A_W), BF16),
        name="window_attention",
    )(sink, u, u, u, u, u, u, u, u, u, tabs, tabs, tabs)


def kernel(x, c, ctx, c_ctx, ada_w, ada_b, norm_w, ev_w_in, ev_q_norm, ev_kv_norm, ev_w_uq, ev_w_ukv,
           ev_w_pool, ev_pool_scale, ev_w_out, od_w_in, od_gate_b, od_head_norm, od_sink, od_w_out, final_norm):
    tabs = _rope_tables()
    mod = _ada(c, c_ctx, ada_w, ada_b)

    n_ev = ev_w_in.shape[0]
    ev_w = _ev_layout(ev_w_in)
    od_w = _od_layout(od_w_in)
    wq = ev_w_uq.astype(BF16).reshape(n_ev, Q_RANK, MLA_HEADS, MLA_QK)
    wq_nope = wq[..., :MLA_NOPE].reshape(n_ev, Q_RANK, MLA_HEADS * MLA_NOPE)
    wq_rope = wq[..., MLA_NOPE:].reshape(n_ev, Q_RANK, MLA_HEADS * ROPE_DIM)
    wkv = ev_w_ukv.astype(BF16).reshape(n_ev, KV_RANK, MLA_HEADS, MLA_NOPE + MLA_VDIM)
    wk_nope = wkv[..., :MLA_NOPE].reshape(n_ev, KV_RANK, MLA_HEADS * MLA_NOPE)
    wv = wkv[..., MLA_NOPE:].reshape(n_ev, KV_RANK, MLA_W)
    q_norm = ev_q_norm.reshape(n_ev, 1, Q_RANK)
    kv_norm = ev_kv_norm.reshape(n_ev, 1, KV_RANK)

    h = _prenorm(x[0], ctx[0], norm_w[0], mod[0])
    res_lat, res_ctx = x[0], ctx[0]
    out = None
    for i in range(DEPTH):
        j = i // 2
        if i % 2 == 0:
            u = _in_proj(h, ev_w, j, EV_TN)
            q = _q_proj(u, j, q_norm, wq_nope, wq_rope, tabs)
            k, v = _kv_proj(u, j, kv_norm, wk_nope, wv, tabs)
            a = _pool(u, ev_w_pool[j], ev_pool_scale[j])
            b = _flash(q, k, v, u)
            w_out = ev_w_out
        else:
            u = _in_proj(h, od_w, j, OD_TN)
            hf, hb = _mlstm(u, od_gate_b[j])
            a = _mlstm_post(hf, hb, u, od_head_norm[j])
            b = _win_attn(u, od_sink[j].astype(F32), tabs)
            w_out = od_w_out
        if i < DEPTH - 1:
            xs, h = _out_proj(a, b, w_out, j, res_lat, res_ctx, mod[i], norm_w[i + 1], mod[i + 1])
            res_lat = res_ctx = xs
        else:
            out = _out_proj(a, b, w_out, j, res_lat, res_ctx, mod[i], final_norm, None)
    return out[None]
```

```python
import functools
import math

import numpy as np
import jax
import jax.numpy as jnp
from jax import lax
from jax.experimental import pallas as pl
from jax.experimental.pallas import tpu as pltpu

F32 = jnp.float32
BF16 = jnp.bfloat16

D_MODEL = 2048
SEQ = 8192
CTX_LEN = 256
ROWS = SEQ + CTX_LEN
DEPTH = 4
GRID_W = 64
EPS = 1e-6
ROPE_THETA = 10000.0
ROPE_DIM = 64
LOG2E = math.log2(math.e)

LANES = 128
ROW_BLK = 256
N_LAT_BLK = SEQ // ROW_BLK
N_BLK = ROWS // ROW_BLK
MM_ROWS = 768
IN_ROWS = 384

POOL_WINDOWS = (2, 4, 8, 16)
POOL_W = 512
POOL_GC = 128
POOL_HALO = 16

MLA_HEADS = 12
MLA_NOPE = 128
MLA_VDIM = 128
MLA_QK = MLA_NOPE + ROPE_DIM
MLA_W = MLA_HEADS * MLA_VDIM
Q_RANK = 512
KV_RANK = 512
HEAD_PAD = 256

M_HEADS = 4
M_DQK = 128
M_DV = 256
M_QK_W = 512
M_V_W = 1024
M_CHUNK = 256

A_HEADS = 16
A_KV_HEADS = 2
A_HD = 64
A_W = 1024
WINDOW = 128

NEG_BIG = -1e30

EV_N = 3840
EV_COL_CQ = 1024
EV_COL_CKV = 1536
EV_COL_AGATE = 2048
EV_COL_KR = 3584
EV_TN = 3840

OD_N = 6656
OD_COL_MO = 2048
OD_COL_MZ = 3072
OD_COL_AQ = 4096
OD_COL_KV = 5120
OD_KV_W = 256
OD_COL_AZ = 5376
OD_COL_MG = 6400
OD_TN = 3328

PREP_W = 256


def _prep_kernel(a_ref, b_ref, o_ref, *, n_copy, n_shift, shift):
    t = pl.program_id(1)

    @pl.when(t < n_copy)
    def _():
        o_ref[...] = a_ref[...].astype(BF16).T

    @pl.when(jnp.logical_and(t >= n_copy, t < n_copy + n_shift))
    def _():
        o_ref[...] = jnp.concatenate([a_ref[shift:, :], b_ref[...]], axis=0).astype(BF16).T

    @pl.when(t == n_copy + n_shift)
    def _():
        pad = jnp.zeros((PREP_W - shift, b_ref.shape[1]), F32)
        o_ref[...] = jnp.concatenate([b_ref[...], pad], axis=0).astype(BF16).T


def _prep_weight(w, n_copy, n_shift, shift):
    layers, d, n = w.shape
    n_tiles = n_copy + n_shift + 1
    per = PREP_W // shift
    a_map = lambda l, t: (l, jnp.minimum(t, n_copy + n_shift - 1), 0)
    b_map = lambda l, t: (l, jnp.where(t == n_tiles - 1, n_copy * per, jnp.minimum((t + 1) * per, n // shift - 1)), 0)
    return pl.pallas_call(
        functools.partial(_prep_kernel, n_copy=n_copy, n_shift=n_shift, shift=shift),
        grid=(layers, n_tiles),
        in_specs=[pl.BlockSpec((None, PREP_W, d), a_map),
                  pl.BlockSpec((None, shift, d), b_map)],
        out_specs=pl.BlockSpec((None, d, PREP_W), lambda l, t: (l, 0, t)),
        out_shape=jax.ShapeDtypeStruct((layers, d, n_tiles * PREP_W), BF16),
        name="weight_layout",
    )(jnp.swapaxes(w, 1, 2), jnp.swapaxes(w, 1, 2))


def _ev_layout(w):
    return _prep_weight(w, n_copy=8, n_shift=6, shift=ROPE_DIM)


def _od_layout(w):
    return _prep_weight(w, n_copy=12, n_shift=13, shift=4 * M_HEADS)


def _rope_tables():
    t = jnp.arange(SEQ)
    row = (t // GRID_W).astype(F32)
    col = (t % GRID_W).astype(F32)
    half = ROPE_DIM // 2
    inv = ROPE_THETA ** (-jnp.arange(0, half, 2, dtype=F32) / half)
    ar = row[:, None] * inv
    ac = col[:, None] * inv
    zero = jnp.zeros_like(ar)
    cos64 = jnp.concatenate([jnp.cos(ar), jnp.cos(ar), jnp.cos(ac), jnp.cos(ac)], axis=-1)
    sa64 = jnp.concatenate([-jnp.sin(ar), zero, -jnp.sin(ac), zero], axis=-1)
    sb64 = jnp.concatenate([zero, jnp.sin(ar), zero, jnp.sin(ac)], axis=-1)
    ctx_zero = jnp.zeros((CTX_LEN, LANES), F32)
    cos = jnp.concatenate([jnp.tile(cos64, (1, 2)), jnp.ones((CTX_LEN, LANES), F32)], axis=0)
    sin_a = jnp.concatenate([jnp.tile(sa64, (1, 2)), ctx_zero], axis=0)
    sin_b = jnp.concatenate([jnp.tile(sb64, (1, 2)), ctx_zero], axis=0)
    return jnp.stack([cos, sin_a, sin_b])


def _rope(x, cos, sin_a, sin_b):
    up = pltpu.roll(x, LANES - ROPE_DIM // 4, 1)
    down = pltpu.roll(x, ROPE_DIM // 4, 1)
    return x * cos + up * sin_a + down * sin_b


def _dot(a, b):
    return jnp.dot(a, b, preferred_element_type=F32)


def _dot_nt(a, b):
    return lax.dot_general(a, b, (((1,), (1,)), ((), ())), preferred_element_type=F32)


def _dot_tn(a, b):
    return lax.dot_general(a, b, (((0,), (0,)), ((), ())), preferred_element_type=F32)


def _rms(x):
    return x * lax.rsqrt(jnp.mean(x * x, axis=-1, keepdims=True) + EPS)


def _modnorm(x, nw, mod):
    return (_rms(x) * nw) * (1.0 + mod[1:2, :]) + mod[0:1, :]


def _silu(x):
    return x * (1.0 / (1.0 + jnp.exp(-x)))


def _sigmoid(x):
    return 1.0 / (1.0 + jnp.exp(-x))


def _split3(x):
    hi = x.astype(BF16)
    r1 = x - hi.astype(F32)
    mid = r1.astype(BF16)
    lo = (r1 - mid.astype(F32)).astype(BF16)
    return hi, mid, lo


def _is_ctx_blk(i):
    return jnp.where(i < N_LAT_BLK, 0, 1)


def _ada_kernel(s_ref, w_ref, b_ref, o_ref):
    s = _silu(s_ref[...])
    hi = s.astype(BF16)
    lo = (s - hi.astype(F32)).astype(BF16)
    lhs = jnp.concatenate([hi, lo], axis=0)
    r = _dot(lhs, w_ref[0].astype(BF16))
    o_ref[0] = r[0:8] + r[8:16] + b_ref[0]


def _ada(c, c_ctx, ada_w, ada_b):
    tn = 1536
    s = jnp.concatenate([c.reshape(1, D_MODEL), c_ctx.reshape(1, D_MODEL), jnp.zeros((6, D_MODEL), F32)], axis=0)
    out = pl.pallas_call(
        _ada_kernel,
        grid=(DEPTH, 3 * D_MODEL // tn),
        in_specs=[pl.BlockSpec((8, D_MODEL), lambda l, j: (0, 0)),
                  pl.BlockSpec((1, D_MODEL, tn), lambda l, j: (l, 0, j)),
                  pl.BlockSpec((1, 1, tn), lambda l, j: (l, 0, j))],
        out_specs=pl.BlockSpec((1, 8, tn), lambda l, j: (l, 0, j)),
        out_shape=jax.ShapeDtypeStruct((DEPTH, 8, 3 * D_MODEL), F32),
        name="ada_mod",
    )(s, ada_w, ada_b.reshape(DEPTH, 1, 3 * D_MODEL))
    m = out[:, :2, :].reshape(DEPTH, 2, 3, D_MODEL)
    return jnp.concatenate([m, jnp.zeros((DEPTH, 2, 5, D_MODEL), F32)], axis=2)


def _split_rows_specs():
    return [pl.BlockSpec((ROW_BLK, D_MODEL), lambda i: (jnp.minimum(i, N_LAT_BLK - 1), 0)),
            pl.BlockSpec((CTX_LEN, D_MODEL), lambda i: (0, 0))]


def _split_rows(x_ref, c_ref, blk):
    return jnp.where(blk < N_LAT_BLK, x_ref[...], c_ref[...])


def _prenorm_kernel(x_ref, c_ref, nw_ref, mod_ref, h_ref):
    x = _split_rows(x_ref, c_ref, pl.program_id(0))
    h_ref[...] = _modnorm(x, nw_ref[...], mod_ref[0]).astype(BF16)


def _prenorm(x, ctx, nw, mod):
    return pl.pallas_call(
        _prenorm_kernel,
        grid=(N_BLK,),
        in_specs=_split_rows_specs() + [pl.BlockSpec((1, D_MODEL), lambda i: (0, 0)),
                                        pl.BlockSpec((1, 8, D_MODEL), lambda i: (_is_ctx_blk(i), 0, 0))],
        out_specs=pl.BlockSpec((ROW_BLK, D_MODEL), lambda i: (i, 0)),
        out_shape=jax.ShapeDtypeStruct((ROWS, D_MODEL), BF16),
        name="prenorm",
    )(x, ctx, nw.reshape(1, D_MODEL), mod)


def _mm_kernel(h_ref, w_ref, o_ref):
    o_ref[...] = _dot(h_ref[...], w_ref[...])


def _in_proj(h, w, layer, tn):
    n = w.shape[2]
    return pl.pallas_call(
        _mm_kernel,
        grid=(n // tn, ROWS // IN_ROWS),
        in_specs=[pl.BlockSpec((IN_ROWS, D_MODEL), lambda j, i: (i, 0)),
                  pl.BlockSpec((None, D_MODEL, tn), lambda j, i: (layer, 0, j))],
        out_specs=pl.BlockSpec((IN_ROWS, tn), lambda j, i: (i, j)),
        out_shape=jax.ShapeDtypeStruct((ROWS, n), F32),
        name="in_proj",
    )(h, w)


Q_SCALE = (MLA_QK ** -0.5) * LOG2E


def _qproj_kernel(cq_ref, nw_ref, wn_ref, wr_ref, tab_ref, q_ref):
    xn = (_rms(cq_ref[...]) * nw_ref[...]).astype(BF16)
    qn = _dot(xn, wn_ref[...]) * Q_SCALE
    for h in range(MLA_HEADS):
        q_ref[:, h * HEAD_PAD:h * HEAD_PAD + LANES] = qn[:, h * LANES:(h + 1) * LANES].astype(BF16)
    qr = _dot(xn, wr_ref[...])
    for p in range(MLA_HEADS // 2):
        rp = (_rope(qr[:, p * LANES:(p + 1) * LANES], tab_ref[0], tab_ref[1], tab_ref[2]) * Q_SCALE).astype(BF16)
        for h in (2 * p, 2 * p + 1):
            q_ref[:, h * HEAD_PAD + LANES:(h + 1) * HEAD_PAD] = rp


def _layer_spec(w, layer):
    return pl.BlockSpec((None,) + w.shape[1:], lambda i: (layer, 0, 0))


def _q_proj(u, layer, nw, wn, wr, tabs):
    t = MM_ROWS
    return pl.pallas_call(
        _qproj_kernel,
        grid=(ROWS // t,),
        in_specs=[pl.BlockSpec((t, Q_RANK), lambda i: (i, EV_COL_CQ // Q_RANK)),
                  _layer_spec(nw, layer), _layer_spec(wn, layer), _layer_spec(wr, layer),
                  pl.BlockSpec((3, t, LANES), lambda i: (0, i, 0))],
        out_specs=pl.BlockSpec((t, MLA_HEADS * HEAD_PAD), lambda i: (i, 0)),
        out_shape=jax.ShapeDtypeStruct((ROWS, MLA_HEADS * HEAD_PAD), BF16),
        name="mla_q_proj",
    )(u, nw, wn, wr, tabs)


def _kvproj_kernel(ckv_ref, kr_ref, nw_ref, wk_ref, wv_ref, tab_ref, k_ref, v_ref):
    xn = (_rms(ckv_ref[...]) * nw_ref[...]).astype(BF16)
    kn = _dot(xn, wk_ref[...])
    vv = _dot(xn, wv_ref[...])
    kr2 = _rope(kr_ref[...], tab_ref[0], tab_ref[1], tab_ref[2])
    lane = lax.broadcasted_iota(jnp.int32, kr2.shape, 1)
    kr_lo = jnp.where(lane < ROPE_DIM, kr2, 0.0)
    k_lo = kr_lo.astype(BF16)
    k_hi = pltpu.roll(kr_lo, ROPE_DIM, 1).astype(BF16)
    for h in range(MLA_HEADS):
        sl = slice(h * LANES, (h + 1) * LANES)
        k_ref[:, h * HEAD_PAD:h * HEAD_PAD + LANES] = kn[:, sl].astype(BF16)
        k_ref[:, h * HEAD_PAD + LANES:(h + 1) * HEAD_PAD] = k_lo if h % 2 == 0 else k_hi
    v_ref[...] = vv.astype(BF16)


def _kv_proj(u, layer, nw, wk, wv, tabs):
    t = MM_ROWS
    wide = MLA_HEADS * HEAD_PAD
    return pl.pallas_call(
        _kvproj_kernel,
        grid=(ROWS // t,),
        in_specs=[pl.BlockSpec((t, KV_RANK), lambda i: (i, EV_COL_CKV // KV_RANK)),
                  pl.BlockSpec((t, LANES), lambda i: (i, EV_COL_KR // LANES)),
                  _layer_spec(nw, layer), _layer_spec(wk, layer), _layer_spec(wv, layer),
                  pl.BlockSpec((3, t, LANES), lambda i: (0, i, 0))],
        out_specs=[pl.BlockSpec((t, wide), lambda i: (i, 0)),
                   pl.BlockSpec((t, MLA_W), lambda i: (i, 0))],
        out_shape=[jax.ShapeDtypeStruct((ROWS, wide), BF16),
                   jax.ShapeDtypeStruct((ROWS, MLA_W), BF16)],
        name="mla_kv_proj",
    )(u, u, nw, wk, wv, tabs)


FL_TQ = 768
FL_TK = 768
FL_LAST = ROWS // FL_TQ - 1
FL_LAT_IN_LAST = SEQ - FL_LAST * FL_TQ


def _key_chunks():
    return [(i * FL_TK, FL_TK) for i in range(ROWS // FL_TK)]


def _attend(q, k_ref, v_ref, sa_ref, sb_ref, chunks):
    rows = q.shape[0]
    bufs = (sa_ref, sb_ref)

    def scores_into(j):
        off, size = chunks[j]
        s = _dot_nt(q, k_ref[off:off + size, :])
        bufs[j % 2][0:rows, 0:size] = s
        return jnp.max(s, axis=1, keepdims=True)

    def ones_block(size):
        lane = lax.broadcasted_iota(jnp.int32, (size, HEAD_PAD - MLA_VDIM), 1)
        return jnp.where(lane == 0, 1.0, 0.0).astype(BF16)

    def update(j, m_chunk, m_prev, acc):
        off, size = chunks[j]
        m_new = jnp.maximum(m_prev, m_chunk)
        alpha = jnp.exp2(m_prev - m_new)
        p = jnp.exp2(bufs[j % 2][0:rows, 0:size] - m_new).astype(BF16)
        v_ext = jnp.concatenate([v_ref[off:off + size, :], ones_block(size)], axis=1)
        return m_new, alpha * acc + _dot(p, v_ext)

    m = jnp.full((rows, 1), NEG_BIG, F32)
    acc = jnp.zeros((rows, HEAD_PAD), F32)
    m_next = scores_into(0)
    for j in range(len(chunks)):
        m_chunk = m_next
        if j + 1 < len(chunks):
            m_next = scores_into(j + 1)
        m, acc = update(j, m_chunk, m, acc)
    return acc[:, :MLA_VDIM] / acc[:, MLA_VDIM:MLA_VDIM + 1]


def _flash_kernel(q_ref, k_ref, v_ref, g_ref, o_ref, sa_ref, sb_ref):
    i = pl.program_id(1)
    chunks = _key_chunks()

    @pl.when(i < FL_LAST)
    def _():
        o = _attend(q_ref[...], k_ref, v_ref, sa_ref, sb_ref, chunks)
        o_ref[...] = (o * _silu(g_ref[...])).astype(BF16)

    @pl.when(i == FL_LAST)
    def _():
        nl = FL_LAT_IN_LAST
        o = _attend(q_ref[0:nl, :], k_ref, v_ref, sa_ref, sb_ref, chunks)
        o_ref[0:nl, :] = (o * _silu(g_ref[0:nl, :])).astype(BF16)
        oc = _attend(q_ref[nl:, :], k_ref, v_ref, sa_ref, sb_ref, [(SEQ, CTX_LEN)])
        o_ref[nl:, :] = (oc * _silu(g_ref[nl:, :])).astype(BF16)


def _flash(q, k, v, u):
    return pl.pallas_call(
        _flash_kernel,
        grid=(MLA_HEADS, ROWS // FL_TQ),
        in_specs=[pl.BlockSpec((FL_TQ, HEAD_PAD), lambda h, i: (i, h)),
                  pl.BlockSpec((ROWS, HEAD_PAD), lambda h, i: (0, h)),
                  pl.BlockSpec((ROWS, MLA_VDIM), lambda h, i: (0, h)),
                  pl.BlockSpec((FL_TQ, MLA_VDIM), lambda h, i: (i, EV_COL_AGATE // MLA_VDIM + h))],
        out_specs=pl.BlockSpec((FL_TQ, MLA_VDIM), lambda h, i: (i, h)),
        out_shape=jax.ShapeDtypeStruct((ROWS, MLA_W), BF16),
        scratch_shapes=[pltpu.VMEM((FL_TQ, FL_TK), F32), pltpu.VMEM((FL_TQ, FL_TK), F32)],
        name="mla_attention",
    )(q, k, v, u)


def _pool_kernel(prev_ref, cur_ref, next_ref, gate_ref, w_ref, sc_ref, o_ref):
    i = pl.program_id(0)
    first = jnp.logical_or(i == 0, i == N_LAT_BLK)
    last = i >= N_LAT_BLK - 1
    lo_lim = jnp.where(first, 0, -POOL_HALO)
    hi_lim = jnp.where(last, ROW_BLK - 1, ROW_BLK - 1 + POOL_HALO)
    src_rows = ROW_BLK + 2 * POOL_HALO
    t = lax.broadcasted_iota(jnp.int32, (ROW_BLK, src_rows), 0)
    j = lax.broadcasted_iota(jnp.int32, (ROW_BLK, src_rows), 1) - POOL_HALO
    t1 = lax.broadcasted_iota(jnp.int32, (ROW_BLK, 1), 0)
    for g, w in enumerate(POOL_WINDOWS):
        sl = slice(g * POOL_GC, (g + 1) * POOL_GC)
        cur = cur_ref[:, sl]
        src = jnp.concatenate([prev_ref[ROW_BLK - POOL_HALO:, sl], cur, next_ref[:POOL_HALO, sl]], axis=0)
        lo = jnp.maximum(t - w // 2, lo_lim)
        hi = jnp.minimum(t + w // 2 - 1, hi_lim)
        band = jnp.where(jnp.logical_and(j >= lo, j <= hi), 1.0, 0.0).astype(BF16)
        cnt = (jnp.minimum(t1 + w // 2 - 1, hi_lim) - jnp.maximum(t1 - w // 2, lo_lim) + 1).astype(F32)
        s_hi, s_mid, s_lo = _split3(src)
        wsum = _dot(band, s_hi) + _dot(band, s_mid) + _dot(band, s_lo)
        z = (wsum / cnt - cur).astype(BF16)
        y = _dot(z, w_ref[g]) * sc_ref[:, sl]
        o_ref[:, sl] = (y * _silu(gate_ref[:, sl])).astype(BF16)


def _pool(u, w_pool, pool_scale):
    def prev_map(i):
        return (jnp.where(i < N_LAT_BLK, jnp.maximum(i - 1, 0), N_LAT_BLK), 0)

    def next_map(i):
        return (jnp.where(i < N_LAT_BLK, jnp.minimum(i + 1, N_LAT_BLK - 1), N_LAT_BLK), 0)

    return pl.pallas_call(
        _pool_kernel,
        grid=(N_BLK,),
        in_specs=[pl.BlockSpec((ROW_BLK, POOL_W), prev_map),
                  pl.BlockSpec((ROW_BLK, POOL_W), lambda i: (i, 0)),
                  pl.BlockSpec((ROW_BLK, POOL_W), next_map),
                  pl.BlockSpec((ROW_BLK, POOL_W), lambda i: (i, 1)),
                  pl.BlockSpec((4, POOL_GC, POOL_GC), lambda i: (0, 0, 0)),
                  pl.BlockSpec((1, POOL_W), lambda i: (0, 0))],
        out_specs=pl.BlockSpec((ROW_BLK, POOL_W), lambda i: (i, 0)),
        out_shape=jax.ShapeDtypeStruct((ROWS, POOL_W), BF16),
        name="pool_mix",
    )(u, u, u, u, w_pool.astype(BF16), pool_scale.reshape(1, POOL_W))


def _outproj_residual(a_ref, b_ref, w_ref, x_ref, c_ref, mod_ref, wbf_ref):
    i = pl.program_id(0)

    @pl.when(i == 0)
    def _():
        wbf_ref[...] = w_ref[...].astype(BF16)

    k1 = a_ref.shape[1]
    y = _dot(a_ref[...], wbf_ref[0:k1, :]) + _dot(b_ref[...], wbf_ref[k1:, :])
    return _split_rows(x_ref, c_ref, i) + mod_ref[0, 2:3, :] * y


def _outproj_kernel(a_ref, b_ref, w_ref, x_ref, c_ref, mod_ref, nw_ref, modn_ref, xo_ref, h_ref, wbf_ref):
    xn = _outproj_residual(a_ref, b_ref, w_ref, x_ref, c_ref, mod_ref, wbf_ref)
    xo_ref[...] = xn
    h_ref[...] = _modnorm(xn, nw_ref[...], modn_ref[0]).astype(BF16)


def _outproj_final_kernel(a_ref, b_ref, w_ref, x_ref, c_ref, mod_ref, nw_ref, o_ref, wbf_ref):
    xn = _outproj_residual(a_ref, b_ref, w_ref, x_ref, c_ref, mod_ref, wbf_ref)
    o_ref[...] = _rms(xn) * nw_ref[...]


def _out_proj(a, b, w, layer, x, x_ctx, mod, nw_next, mod_next):
    k1, k2 = a.shape[1], b.shape[1]
    n_blk = N_LAT_BLK if mod_next is None else N_BLK
    row = lambda i: (i, 0)
    full = lambda i: (0, 0)
    stream = lambda i: (_is_ctx_blk(i), 0, 0)
    ctx_blk = x_ctx.shape[0] // CTX_LEN - 1
    common = [pl.BlockSpec((ROW_BLK, k1), row), pl.BlockSpec((ROW_BLK, k2), row),
              pl.BlockSpec((None, D_MODEL, D_MODEL), lambda i: (layer, 0, 0), pipeline_mode=pl.Buffered(1)),
              pl.BlockSpec((ROW_BLK, D_MODEL), lambda i: (jnp.minimum(i, N_LAT_BLK - 1), 0)),
              pl.BlockSpec((CTX_LEN, D_MODEL), lambda i: (ctx_blk, 0)),
              pl.BlockSpec((1, 8, D_MODEL), stream),
              pl.BlockSpec((1, D_MODEL), full)]
    scratch = [pltpu.VMEM((D_MODEL, D_MODEL), BF16)]
    params = pltpu.CompilerParams(dimension_semantics=("arbitrary",))
    if mod_next is None:
        return pl.pallas_call(
            _outproj_final_kernel,
            grid=(n_blk,),
            in_specs=common,
            out_specs=pl.BlockSpec((ROW_BLK, D_MODEL), row),
            out_shape=jax.ShapeDtypeStruct((SEQ, D_MODEL), F32),
            scratch_shapes=scratch,
            compiler_params=params,
            name="out_proj_final",
        )(a, b, w, x, x_ctx, mod, nw_next.reshape(1, D_MODEL))
    return pl.pallas_call(
        _outproj_kernel,
        grid=(n_blk,),
        in_specs=common + [pl.BlockSpec((1, 8, D_MODEL), stream)],
        out_specs=[pl.BlockSpec((ROW_BLK, D_MODEL), row), pl.BlockSpec((ROW_BLK, D_MODEL), row)],
        out_shape=[jax.ShapeDtypeStruct((ROWS, D_MODEL), F32), jax.ShapeDtypeStruct((ROWS, D_MODEL), BF16)],
        scratch_shapes=scratch,
        compiler_params=params,
        name="out_proj",
    )(a, b, w, x, x_ctx, mod, nw_next.reshape(1, D_MODEL), mod_next)


M_QSCALE = M_DQK ** -0.5


def _log_sigmoid(x):
    return jnp.minimum(x, 0.0) - jnp.log(1.0 + jnp.exp(-jnp.abs(x)))


def _mlstm_products(q_ref, k_ref, c_ref, n_ref, qk_ref, cq_ref, nq_ref, reverse):
    d = 1 if reverse else 0
    for h in range(M_HEADS):
        st = M_HEADS * d + h
        q = (q_ref[:, h * M_DQK:(h + 1) * M_DQK] * M_QSCALE).astype(BF16)
        qk_ref[st] = _dot_nt(k_ref[:, h * M_DQK:(h + 1) * M_DQK].astype(BF16), q)
        cq_ref[st] = _dot_nt(c_ref[st].astype(BF16), q)
        nq_ref[st] = _dot_nt(jnp.broadcast_to(n_ref[st:st + 1, :], (8, M_DQK)).astype(BF16), q)


def _mlstm_dir(k_ref, v_ref, g_ref, gb_ref, ht_ref, c_ref, n_ref, m_ref, qk_ref, cq_ref, nq_ref, reverse):
    T = M_CHUNK
    d = 1 if reverse else 0
    base = 2 * M_HEADS * d
    g = g_ref[...] + gb_ref[...]
    glf = _log_sigmoid(g)
    row = lax.broadcasted_iota(jnp.int32, (T, T), 0)
    col = lax.broadcasted_iota(jnp.int32, (T, T), 1)
    tri = jnp.where((col >= row) if reverse else (col <= row), 1.0, 0.0).astype(BF16)
    reach = (row >= col) if reverse else (row <= col)
    p_hi, p_mid, p_lo = _split3(glf)
    bcum = _dot(tri, p_hi) + _dot(tri, p_mid) + _dot(tri, p_lo)
    g_sh = pltpu.roll(g, M_HEADS, 1)
    r_all = g_sh - bcum
    end = 0 if reverse else T - 1
    b_last = bcum[end:end + 1, :]
    a_all = b_last - bcum + g_sh
    m_prev_row = m_ref[d:d + 1, :]
    m_new_row = jnp.maximum(b_last + m_prev_row, jnp.max(a_all, axis=0, keepdims=True))
    decay_row = jnp.exp(b_last + m_prev_row - m_new_row)
    w_all = jnp.exp(a_all - m_new_row)
    m_ref[d:d + 1, :] = m_new_row
    b_t = bcum.T
    for h in range(M_HEADS):
        cf = base + M_HEADS + h
        st = M_HEADS * d + h
        m_prev = m_prev_row[:, cf:cf + 1]
        decay = decay_row[:, cf:cf + 1]
        r_mat = jnp.where(reach, r_all[:, cf:cf + 1], -jnp.inf)
        mx = jnp.maximum(m_prev, jnp.max(r_mat, axis=0, keepdims=True))
        iw = jnp.exp(m_prev - mx)
        dm = jnp.exp(r_mat - mx)

        v = v_ref[:, h * M_DV:(h + 1) * M_DV].astype(BF16)
        s = qk_ref[st] * dm
        den = iw * nq_ref[st, 0:1, :] + jnp.sum(s, axis=0, keepdims=True)
        inv = 1.0 / jnp.maximum(jnp.abs(den), jnp.exp(-(b_t[cf:cf + 1, :] + mx)))
        ht_ref[h * M_DV:(h + 1) * M_DV, :] = _dot_tn(v, (s * inv).astype(BF16)) + cq_ref[st] * (iw * inv)

        wk = w_all[:, cf:cf + 1] * k_ref[:, h * M_DQK:(h + 1) * M_DQK]
        c_ref[st] = decay * c_ref[st] + _dot_tn(v, wk.astype(BF16))
        n_ref[st:st + 1, :] = decay * n_ref[st:st + 1, :] + jnp.sum(wk, axis=0, keepdims=True)


def _mlstm_kernel(qf_ref, kf_ref, vf_ref, gf_ref, qb_ref, kb_ref, vb_ref, gb_ref, bias_ref,
                  hf_ref, hb_ref, c_ref, n_ref, m_ref, qk_ref, cq_ref, nq_ref):
    @pl.when(pl.program_id(0) == 0)
    def _():
        c_ref[...] = jnp.zeros_like(c_ref)
        n_ref[...] = jnp.zeros_like(n_ref)
        m_ref[...] = jnp.zeros_like(m_ref)

    products = (qk_ref, cq_ref, nq_ref)
    _mlstm_products(qf_ref, kf_ref, c_ref, n_ref, *products, reverse=False)
    _mlstm_products(qb_ref, kb_ref, c_ref, n_ref, *products, reverse=True)
    _mlstm_dir(kf_ref, vf_ref, gf_ref, bias_ref, hf_ref, c_ref, n_ref, m_ref, *products, reverse=False)
    _mlstm_dir(kb_ref, vb_ref, gb_ref, bias_ref, hb_ref, c_ref, n_ref, m_ref, *products, reverse=True)


def _mlstm(u, gate_b):
    fwd = lambda k: jnp.where(k == 0, N_LAT_BLK, k - 1)
    bwd = lambda k: jnp.where(k == 0, N_LAT_BLK, N_LAT_BLK - k)
    mg_blk = OD_COL_MG // LANES

    def specs(rmap):
        return [pl.BlockSpec((M_CHUNK, M_QK_W), lambda k: (rmap(k), 0)),
                pl.BlockSpec((M_CHUNK, M_QK_W), lambda k: (rmap(k), 1)),
                pl.BlockSpec((M_CHUNK, M_V_W), lambda k: (rmap(k), 1)),
                pl.BlockSpec((M_CHUNK, LANES), lambda k: (rmap(k), mg_blk))]

    bias = jnp.concatenate([gate_b.astype(F32), jnp.zeros((LANES - 4 * M_HEADS,), F32)]).reshape(1, LANES)
    return pl.pallas_call(
        _mlstm_kernel,
        grid=(N_BLK,),
        in_specs=specs(fwd) + specs(bwd) + [pl.BlockSpec((1, LANES), lambda k: (0, 0))],
        out_specs=[pl.BlockSpec((M_V_W, M_CHUNK), lambda k: (0, fwd(k))),
                   pl.BlockSpec((M_V_W, M_CHUNK), lambda k: (0, bwd(k)))],
        out_shape=[jax.ShapeDtypeStruct((M_V_W, ROWS), F32), jax.ShapeDtypeStruct((M_V_W, ROWS), F32)],
        scratch_shapes=[pltpu.VMEM((2 * M_HEADS, M_DV, M_DQK), F32),
                        pltpu.VMEM((2 * M_HEADS, M_DQK), F32),
                        pltpu.VMEM((8, LANES), F32),
                        pltpu.VMEM((2 * M_HEADS, M_CHUNK, M_CHUNK), F32),
                        pltpu.VMEM((2 * M_HEADS, M_DV, M_CHUNK), F32),
                        pltpu.VMEM((2 * M_HEADS, 8, M_CHUNK), F32)],
        compiler_params=pltpu.CompilerParams(dimension_semantics=("arbitrary",)),
        name="mlstm_scan",
    )(u, u, u, u, u, u, u, u, bias)


def _mlstm_post_kernel(hf_ref, hb_ref, o_ref, z_ref, nw_ref, y_ref):
    for h in range(M_HEADS):
        sl = slice(h * M_DV, (h + 1) * M_DV)
        ht = hf_ref[sl, :] + hb_ref[sl, :]
        hn = (ht * lax.rsqrt(jnp.mean(ht * ht, axis=0, keepdims=True) + EPS)).T * nw_ref[:, sl]
        y_ref[:, sl] = (hn * _sigmoid(o_ref[:, sl]) * _silu(z_ref[:, sl])).astype(BF16)


def _mlstm_post(hf, hb, u, head_norm):
    row = lambda i: (i, 0)
    col = lambda i: (0, i)
    t = MM_ROWS
    return pl.pallas_call(
        _mlstm_post_kernel,
        grid=(ROWS // t,),
        in_specs=[pl.BlockSpec((M_V_W, t), col), pl.BlockSpec((M_V_W, t), col),
                  pl.BlockSpec((t, M_V_W), lambda i: (i, OD_COL_MO // M_V_W)),
                  pl.BlockSpec((t, M_V_W), lambda i: (i, OD_COL_MZ // M_V_W)),
                  pl.BlockSpec((1, M_V_W), lambda i: (0, 0))],
        out_specs=pl.BlockSpec((t, M_V_W), row),
        out_shape=jax.ShapeDtypeStruct((ROWS, M_V_W), BF16),
        name="mlstm_post",
    )(hf, hb, u, u, head_norm.reshape(1, M_V_W))


W_NLAT = SEQ // WINDOW
W_NBLK = ROWS // WINDOW
W_QSCALE = (A_HD ** -0.5) * LOG2E
W_KEYS = 3 * WINDOW + CTX_LEN
W_PAIRS = A_HEADS // A_KV_HEADS // 2


def _win_kernel(sink_ref, q_ref, az0_ref, az1_ref, az2_ref, az3_ref, kvp_ref, kvc_ref, kvn_ref, kvx_ref,
                tq_ref, tp_ref, tn_ref, o_ref, s_ref):
    az_refs = (az0_ref, az1_ref, az2_ref, az3_ref)
    bi = pl.program_id(0)
    is_lat = bi < W_NLAT
    tq = (tq_ref[0], tq_ref[1], tq_ref[2])
    lane_k = lax.broadcasted_iota(jnp.int32, (W_KEYS, LANES), 1)

    kj = lax.broadcasted_iota(jnp.int32, (WINDOW, WINDOW), 0)
    qi = lax.broadcasted_iota(jnp.int32, (WINDOW, WINDOW), 1)
    ok_prev = jnp.logical_and(jnp.logical_and(is_lat, bi >= 1), kj >= qi)
    ok_cur = jnp.logical_and(is_lat, kj >= 0)
    ok_next = jnp.logical_and(jnp.logical_and(is_lat, bi <= W_NLAT - 2), kj <= qi)
    ok_ctx = lax.broadcasted_iota(jnp.int32, (CTX_LEN, WINDOW), 1) >= 0
    valid = jnp.concatenate([ok_prev, ok_cur, ok_next, ok_ctx], axis=0)
    valid4 = jnp.concatenate([valid] * W_PAIRS, axis=1)
    lane4 = lax.broadcasted_iota(jnp.int32, (1, W_PAIRS * WINDOW), 1)

    kk = jnp.concatenate([_rope(kvp_ref[:, 0:LANES], tp_ref[0], tp_ref[1], tp_ref[2]),
                          _rope(kvc_ref[:, 0:LANES], *tq),
                          _rope(kvn_ref[:, 0:LANES], tn_ref[0], tn_ref[1], tn_ref[2]),
                          kvx_ref[:, 0:LANES]], axis=0)
    vv = jnp.concatenate([kvp_ref[:, LANES:2 * LANES], kvc_ref[:, LANES:2 * LANES],
                          kvn_ref[:, LANES:2 * LANES], kvx_ref[:, LANES:2 * LANES]], axis=0)
    kk_sw = pltpu.roll(kk, A_HD, 1)
    vv_sw = pltpu.roll(vv, A_HD, 1)
    low = lane_k < A_HD

    v_halves = []
    maxima = []
    for g in range(A_KV_HEADS):
        k_src = (kk, kk_sw) if g == 0 else (kk_sw, kk)
        v_src = (vv, vv_sw) if g == 0 else (vv_sw, vv)
        k_half = [jnp.where(low, k_src[0], 0.0).astype(BF16), jnp.where(low, 0.0, k_src[1]).astype(BF16)]
        v_halves.append([jnp.where(low, v_src[0], 0.0).astype(BF16), jnp.where(low, 0.0, v_src[1]).astype(BF16)])
        cols = [(g * W_PAIRS + p) * LANES for p in range(W_PAIRS)]
        qs = jnp.concatenate([(_rope(q_ref[:, c:c + LANES], *tq) * W_QSCALE).astype(BF16) for c in cols],
                             axis=0)
        for par in range(2):
            s = jnp.where(valid4, _dot_nt(k_half[par], qs), -jnp.inf)
            s_ref[2 * g + par] = s
            maxima.append(jnp.max(s, axis=0, keepdims=True))

    for g in range(A_KV_HEADS):
        cols = [(g * W_PAIRS + p) * LANES for p in range(W_PAIRS)]
        out_t = jnp.zeros((LANES, W_PAIRS * WINDOW), F32)
        for par in range(2):
            snk = jnp.full((1, W_PAIRS * WINDOW), sink_ref[g * 8 + par] * LOG2E, F32)
            for p in range(1, W_PAIRS):
                snk = jnp.where(lane4 >= p * WINDOW, sink_ref[g * 8 + 2 * p + par] * LOG2E, snk)
            m = jnp.maximum(maxima[2 * g + par], snk)
            e = jnp.exp2(s_ref[2 * g + par] - m)
            den = jnp.sum(e, axis=0, keepdims=True) + jnp.exp2(snk - m)
            out_t = out_t + _dot_tn(v_halves[g][par], e.astype(BF16)) * (1.0 / den)
        out = out_t.T
        for p, c in enumerate(cols):
            grp = c // LANES
            az = az_refs[grp // 2][:, (grp % 2) * LANES:(grp % 2 + 1) * LANES]
            o_ref[:, c:c + LANES] = (out[p * WINDOW:(p + 1) * WINDOW] * _silu(az)).astype(BF16)


def _win_attn(u, sink, tabs):
    kvb = OD_COL_KV // OD_KV_W
    cur = lambda b: jnp.minimum(b, W_NLAT - 1)
    prv = lambda b: jnp.clip(b - 1, 0, W_NLAT - 1)
    nxt = lambda b: jnp.clip(b + 1, 0, W_NLAT - 1)
    tab = lambda f: pl.BlockSpec((3, WINDOW, LANES), lambda b: (0, f(b), 0))
    az_spec = lambda n: pl.BlockSpec((WINDOW, 2 * LANES), lambda b: (b, OD_COL_AZ // (2 * LANES) + n))
    return pl.pallas_call(
        _win_kernel,
        grid=(W_NBLK,),
        in_specs=[pl.BlockSpec(memory_space=pltpu.SMEM),
                  pl.BlockSpec((WINDOW, A_W), lambda b: (b, OD_COL_AQ // A_W)),
                  az_spec(0), az_spec(1), az_spec(2), az_spec(3),
                  pl.BlockSpec((WINDOW, OD_KV_W), lambda b: (prv(b), kvb)),
                  pl.BlockSpec((WINDOW, OD_KV_W), lambda b: (cur(b), kvb)),
                  pl.BlockSpec((WINDOW, OD_KV_W), lambda b: (nxt(b), kvb)),
                  pl.BlockSpec((CTX_LEN, OD_KV_W), lambda b: (SEQ // CTX_LEN, kvb)),
                  tab(lambda b: b), tab(prv), tab(nxt)],
        out_specs=pl.BlockSpec((WINDOW, A_W), lambda b: (b, 0)),
        out_shape=jax.ShapeDtypeStruct((ROWS, A_W), BF16),
        scratch_shapes=[pltpu.VMEM((2 * A_KV_HEADS, W_KEYS, W_PAIRS * WINDOW), F32)],
        name="window_attention",
    )(sink, u, u, u, u, u, u, u, u, u, tabs, tabs, tabs)


def kernel(x, c, ctx, c_ctx, ada_w, ada_b, norm_w, ev_w_in, ev_q_norm, ev_kv_norm, ev_w_uq, ev_w_ukv,
           ev_w_pool, ev_pool_scale, ev_w_out, od_w_in, od_gate_b, od_head_norm, od_sink, od_w_out, final_norm):
    tabs = _rope_tables()
    mod = _ada(c, c_ctx, ada_w, ada_b)

    n_ev = ev_w_in.shape[0]
    ev_w = _ev_layout(ev_w_in)
    od_w = _od_layout(od_w_in)
    wq = ev_w_uq.astype(BF16).reshape(n_ev, Q_RANK, MLA_HEADS, MLA_QK)
    wq_nope = wq[..., :MLA_NOPE].reshape(n_ev, Q_RANK, MLA_HEADS * MLA_NOPE)
    wq_rope = wq[..., MLA_NOPE:].reshape(n_ev, Q_RANK, MLA_HEADS * ROPE_DIM)
    wkv = ev_w_ukv.astype(BF16).reshape(n_ev, KV_RANK, MLA_HEADS, MLA_NOPE + MLA_VDIM)
    wk_nope = wkv[..., :MLA_NOPE].reshape(n_ev, KV_RANK, MLA_HEADS * MLA_NOPE)
    wv = wkv[..., MLA_NOPE:].reshape(n_ev, KV_RANK, MLA_W)
    q_norm = ev_q_norm.reshape(n_ev, 1, Q_RANK)
    kv_norm = ev_kv_norm.reshape(n_ev, 1, KV_RANK)

    h = _prenorm(x[0], ctx[0], norm_w[0], mod[0])
    res_lat, res_ctx = x[0], ctx[0]
    out = None
    for i in range(DEPTH):
        j = i // 2
        if i % 2 == 0:
            u = _in_proj(h, ev_w, j, EV_TN)
            q = _q_proj(u, j, q_norm, wq_nope, wq_rope, tabs)
            k, v = _kv_proj(u, j, kv_norm, wk_nope, wv, tabs)
            a = _pool(u, ev_w_pool[j], ev_pool_scale[j])
            b = _flash(q, k, v, u)
            w_out = ev_w_out
        else:
            u = _in_proj(h, od_w, j, OD_TN)
            hf, hb = _mlstm(u, od_gate_b[j])
            a = _mlstm_post(hf, hb, u, od_head_norm[j])
            b = _win_attn(u, od_sink[j].astype(F32), tabs)
            w_out = od_w_out
        if i < DEPTH - 1:
            xs, h = _out_proj(a, b, w_out, j, res_lat, res_ctx, mod[i], norm_w[i + 1], mod[i + 1])
            res_lat = res_ctx = xs
        else:
            out = _out_proj(a, b, w_out, j, res_lat, res_ctx, mod[i], final_norm, None)
    return out[None]
```

```python
import functools
import math

import numpy as np
import jax
import jax.numpy as jnp
from jax import lax
from jax.experimental import pallas as pl
from jax.experimental.pallas import tpu as pltpu

F32 = jnp.float32
BF16 = jnp.bfloat16

D_MODEL = 2048
SEQ = 8192
CTX_LEN = 256
ROWS = SEQ + CTX_LEN
DEPTH = 4
GRID_W = 64
EPS = 1e-6
ROPE_THETA = 10000.0
ROPE_DIM = 64
LOG2E = math.log2(math.e)

LANES = 128
ROW_BLK = 256
N_LAT_BLK = SEQ // ROW_BLK
N_BLK = ROWS // ROW_BLK
MM_ROWS = 768
IN_ROWS = 384

POOL_WINDOWS = (2, 4, 8, 16)
POOL_W = 512
POOL_GC = 128
POOL_HALO = 16

MLA_HEADS = 12
MLA_NOPE = 128
MLA_VDIM = 128
MLA_QK = MLA_NOPE + ROPE_DIM
MLA_W = MLA_HEADS * MLA_VDIM
Q_RANK = 512
KV_RANK = 512
HEAD_PAD = 256

M_HEADS = 4
M_DQK = 128
M_DV = 256
M_QK_W = 512
M_V_W = 1024
M_CHUNK = 256

A_HEADS = 16
A_KV_HEADS = 2
A_HD = 64
A_W = 1024
WINDOW = 128

NEG_BIG = -1e30

EV_N = 3840
EV_COL_CQ = 1024
EV_COL_CKV = 1536
EV_COL_AGATE = 2048
EV_COL_KR = 3584
EV_TN = 3840

OD_N = 6656
OD_COL_MO = 2048
OD_COL_MZ = 3072
OD_COL_AQ = 4096
OD_COL_KV = 5120
OD_KV_W = 256
OD_COL_AZ = 5376
OD_COL_MG = 6400
OD_TN = 3328

PREP_W = 256


def _prep_kernel(a_ref, b_ref, o_ref, *, n_copy, n_shift, shift):
    t = pl.program_id(1)

    @pl.when(t < n_copy)
    def _():
        o_ref[...] = a_ref[...].astype(BF16).T

    @pl.when(jnp.logical_and(t >= n_copy, t < n_copy + n_shift))
    def _():
        o_ref[...] = jnp.concatenate([a_ref[shift:, :], b_ref[...]], axis=0).astype(BF16).T

    @pl.when(t == n_copy + n_shift)
    def _():
        pad = jnp.zeros((PREP_W - shift, b_ref.shape[1]), F32)
        o_ref[...] = jnp.concatenate([b_ref[...], pad], axis=0).astype(BF16).T


def _prep_weight(w, n_copy, n_shift, shift):
    layers, d, n = w.shape
    n_tiles = n_copy + n_shift + 1
    per = PREP_W // shift
    a_map = lambda l, t: (l, jnp.minimum(t, n_copy + n_shift - 1), 0)
    b_map = lambda l, t: (l, jnp.where(t == n_tiles - 1, n_copy * per, jnp.minimum((t + 1) * per, n // shift - 1)), 0)
    return pl.pallas_call(
        functools.partial(_prep_kernel, n_copy=n_copy, n_shift=n_shift, shift=shift),
        grid=(layers, n_tiles),
        in_specs=[pl.BlockSpec((None, PREP_W, d), a_map),
                  pl.BlockSpec((None, shift, d), b_map)],
        out_specs=pl.BlockSpec((None, d, PREP_W), lambda l, t: (l, 0, t)),
        out_shape=jax.ShapeDtypeStruct((layers, d, n_tiles * PREP_W), BF16),
        name="weight_layout",
    )(jnp.swapaxes(w, 1, 2), jnp.swapaxes(w, 1, 2))


def _ev_layout(w):
    return _prep_weight(w, n_copy=8, n_shift=6, shift=ROPE_DIM)


def _od_layout(w):
    return _prep_weight(w, n_copy=12, n_shift=13, shift=4 * M_HEADS)


def _rope_tables():
    t = jnp.arange(SEQ)
    row = (t // GRID_W).astype(F32)
    col = (t % GRID_W).astype(F32)
    half = ROPE_DIM // 2
    inv = ROPE_THETA ** (-jnp.arange(0, half, 2, dtype=F32) / half)
    ar = row[:, None] * inv
    ac = col[:, None] * inv
    zero = jnp.zeros_like(ar)
    cos64 = jnp.concatenate([jnp.cos(ar), jnp.cos(ar), jnp.cos(ac), jnp.cos(ac)], axis=-1)
    sa64 = jnp.concatenate([-jnp.sin(ar), zero, -jnp.sin(ac), zero], axis=-1)
    sb64 = jnp.concatenate([zero, jnp.sin(ar), zero, jnp.sin(ac)], axis=-1)
    ctx_zero = jnp.zeros((CTX_LEN, LANES), F32)
    cos = jnp.concatenate([jnp.tile(cos64, (1, 2)), jnp.ones((CTX_LEN, LANES), F32)], axis=0)
    sin_a = jnp.concatenate([jnp.tile(sa64, (1, 2)), ctx_zero], axis=0)
    sin_b = jnp.concatenate([jnp.tile(sb64, (1, 2)), ctx_zero], axis=0)
    return jnp.stack([cos, sin_a, sin_b])


def _rope(x, cos, sin_a, sin_b):
    up = pltpu.roll(x, LANES - ROPE_DIM // 4, 1)
    down = pltpu.roll(x, ROPE_DIM // 4, 1)
    return x * cos + up * sin_a + down * sin_b


def _dot(a, b):
    return jnp.dot(a, b, preferred_element_type=F32)


def _dot_nt(a, b):
    return lax.dot_general(a, b, (((1,), (1,)), ((), ())), preferred_element_type=F32)


def _dot_tn(a, b):
    return lax.dot_general(a, b, (((0,), (0,)), ((), ())), preferred_element_type=F32)


def _rms(x):
    return x * lax.rsqrt(jnp.mean(x * x, axis=-1, keepdims=True) + EPS)


def _modnorm(x, nw, mod):
    return (_rms(x) * nw) * (1.0 + mod[1:2, :]) + mod[0:1, :]


def _silu(x):
    return x * (1.0 / (1.0 + jnp.exp(-x)))


def _sigmoid(x):
    return 1.0 / (1.0 + jnp.exp(-x))


def _split3(x):
    hi = x.astype(BF16)
    r1 = x - hi.astype(F32)
    mid = r1.astype(BF16)
    lo = (r1 - mid.astype(F32)).astype(BF16)
    return hi, mid, lo


def _is_ctx_blk(i):
    return jnp.where(i < N_LAT_BLK, 0, 1)


def _ada_kernel(s_ref, w_ref, b_ref, o_ref):
    s = _silu(s_ref[...])
    hi = s.astype(BF16)
    lo = (s - hi.astype(F32)).astype(BF16)
    lhs = jnp.concatenate([hi, lo], axis=0)
    r = _dot(lhs, w_ref[0].astype(BF16))
    o_ref[0] = r[0:8] + r[8:16] + b_ref[0]


def _ada(c, c_ctx, ada_w, ada_b):
    tn = 1536
    s = jnp.concatenate([c.reshape(1, D_MODEL), c_ctx.reshape(1, D_MODEL), jnp.zeros((6, D_MODEL), F32)], axis=0)
    out = pl.pallas_call(
        _ada_kernel,
        grid=(DEPTH, 3 * D_MODEL // tn),
        in_specs=[pl.BlockSpec((8, D_MODEL), lambda l, j: (0, 0)),
                  pl.BlockSpec((1, D_MODEL, tn), lambda l, j: (l, 0, j)),
                  pl.BlockSpec((1, 1, tn), lambda l, j: (l, 0, j))],
        out_specs=pl.BlockSpec((1, 8, tn), lambda l, j: (l, 0, j)),
        out_shape=jax.ShapeDtypeStruct((DEPTH, 8, 3 * D_MODEL), F32),
        name="ada_mod",
    )(s, ada_w, ada_b.reshape(DEPTH, 1, 3 * D_MODEL))
    m = out[:, :2, :].reshape(DEPTH, 2, 3, D_MODEL)
    return jnp.concatenate([m, jnp.zeros((DEPTH, 2, 5, D_MODEL), F32)], axis=2)


def _split_rows_specs():
    return [pl.BlockSpec((ROW_BLK, D_MODEL), lambda i: (jnp.minimum(i, N_LAT_BLK - 1), 0)),
            pl.BlockSpec((CTX_LEN, D_MODEL), lambda i: (0, 0))]


def _split_rows(x_ref, c_ref, blk):
    return jnp.where(blk < N_LAT_BLK, x_ref[...], c_ref[...])


def _prenorm_kernel(x_ref, c_ref, nw_ref, mod_ref, h_ref):
    x = _split_rows(x_ref, c_ref, pl.program_id(0))
    h_ref[...] = _modnorm(x, nw_ref[...], mod_ref[0]).astype(BF16)


def _prenorm(x, ctx, nw, mod):
    return pl.pallas_call(
        _prenorm_kernel,
        grid=(N_BLK,),
        in_specs=_split_rows_specs() + [pl.BlockSpec((1, D_MODEL), lambda i: (0, 0)),
                                        pl.BlockSpec((1, 8, D_MODEL), lambda i: (_is_ctx_blk(i), 0, 0))],
        out_specs=pl.BlockSpec((ROW_BLK, D_MODEL), lambda i: (i, 0)),
        out_shape=jax.ShapeDtypeStruct((ROWS, D_MODEL), BF16),
        name="prenorm",
    )(x, ctx, nw.reshape(1, D_MODEL), mod)


def _mm_kernel(h_ref, w_ref, o_ref):
    o_ref[...] = _dot(h_ref[...], w_ref[...])


def _in_proj(h, w, layer, tn):
    n = w.shape[2]
    return pl.pallas_call(
        _mm_kernel,
        grid=(n // tn, ROWS // IN_ROWS),
        in_specs=[pl.BlockSpec((IN_ROWS, D_MODEL), lambda j, i: (i, 0)),
                  pl.BlockSpec((None, D_MODEL, tn), lambda j, i: (layer, 0, j))],
        out_specs=pl.BlockSpec((IN_ROWS, tn), lambda j, i: (i, j)),
        out_shape=jax.ShapeDtypeStruct((ROWS, n), F32),
        name="in_proj",
    )(h, w)


Q_SCALE = (MLA_QK ** -0.5) * LOG2E


def _qproj_kernel(cq_ref, nw_ref, wn_ref, wr_ref, tab_ref, qn_ref, qr_ref):
    xn = (_rms(cq_ref[...]) * nw_ref[...]).astype(BF16)
    qn_ref[...] = (_dot(xn, wn_ref[...]) * Q_SCALE).astype(BF16)
    qr = _dot(xn, wr_ref[...])
    for p in range(MLA_HEADS // 2):
        sl = slice(p * LANES, (p + 1) * LANES)
        qr_ref[:, sl] = (_rope(qr[:, sl], tab_ref[0], tab_ref[1], tab_ref[2]) * Q_SCALE).astype(BF16)


def _layer_spec(w, layer):
    return pl.BlockSpec((None,) + w.shape[1:], lambda i: (layer, 0, 0))


def _q_proj(u, layer, nw, wn, wr, tabs):
    t = MM_ROWS
    return pl.pallas_call(
        _qproj_kernel,
        grid=(ROWS // t,),
        in_specs=[pl.BlockSpec((t, Q_RANK), lambda i: (i, EV_COL_CQ // Q_RANK)),
                  _layer_spec(nw, layer), _layer_spec(wn, layer), _layer_spec(wr, layer),
                  pl.BlockSpec((3, t, LANES), lambda i: (0, i, 0))],
        out_specs=[pl.BlockSpec((t, MLA_HEADS * MLA_NOPE), lambda i: (i, 0)),
                   pl.BlockSpec((t, MLA_HEADS * ROPE_DIM), lambda i: (i, 0))],
        out_shape=[jax.ShapeDtypeStruct((ROWS, MLA_HEADS * MLA_NOPE), BF16),
                   jax.ShapeDtypeStruct((ROWS, MLA_HEADS * ROPE_DIM), BF16)],
        name="mla_q_proj",
    )(u, nw, wn, wr, tabs)


def _kvproj_kernel(ckv_ref, kr_ref, nw_ref, wk_ref, wv_ref, tab_ref, kn_ref, kro_ref, v_ref):
    xn = (_rms(ckv_ref[...]) * nw_ref[...]).astype(BF16)
    kn_ref[...] = _dot(xn, wk_ref[...]).astype(BF16)
    v_ref[...] = _dot(xn, wv_ref[...]).astype(BF16)
    kr2 = _rope(kr_ref[...], tab_ref[0], tab_ref[1], tab_ref[2])
    lane = lax.broadcasted_iota(jnp.int32, kr2.shape, 1)
    kr_lo = jnp.where(lane < ROPE_DIM, kr2, 0.0)
    kro_ref[:, 0:LANES] = kr_lo.astype(BF16)
    kro_ref[:, LANES:2 * LANES] = pltpu.roll(kr_lo, ROPE_DIM, 1).astype(BF16)


def _kv_proj(u, layer, nw, wk, wv, tabs):
    t = MM_ROWS
    return pl.pallas_call(
        _kvproj_kernel,
        grid=(ROWS // t,),
        in_specs=[pl.BlockSpec((t, KV_RANK), lambda i: (i, EV_COL_CKV // KV_RANK)),
                  pl.BlockSpec((t, LANES), lambda i: (i, EV_COL_KR // LANES)),
                  _layer_spec(nw, layer), _layer_spec(wk, layer), _layer_spec(wv, layer),
                  pl.BlockSpec((3, t, LANES), lambda i: (0, i, 0))],
        out_specs=[pl.BlockSpec((t, MLA_HEADS * MLA_NOPE), lambda i: (i, 0)),
                   pl.BlockSpec((t, 2 * LANES), lambda i: (i, 0)),
                   pl.BlockSpec((t, MLA_W), lambda i: (i, 0))],
        out_shape=[jax.ShapeDtypeStruct((ROWS, MLA_HEADS * MLA_NOPE), BF16),
                   jax.ShapeDtypeStruct((ROWS, 2 * LANES), BF16),
                   jax.ShapeDtypeStruct((ROWS, MLA_W), BF16)],
        name="mla_kv_proj",
    )(u, u, nw, wk, wv, tabs)


FL_TQ = 768
FL_TK = 768
FL_LAST = ROWS // FL_TQ - 1
FL_LAT_IN_LAST = SEQ - FL_LAST * FL_TQ


def _key_chunks():
    return [(i * FL_TK, FL_TK) for i in range(ROWS // FL_TK)]


def _attend(q, kn_ref, kr_ref, v_ref, sa_ref, sb_ref, chunks):
    rows = q.shape[0]
    bufs = (sa_ref, sb_ref)

    def scores_into(j):
        off, size = chunks[j]
        k = jnp.concatenate([kn_ref[off:off + size, :], kr_ref[off:off + size, :]], axis=1)
        s = _dot_nt(q, k)
        bufs[j % 2][0:rows, 0:size] = s
        return jnp.max(s, axis=1, keepdims=True)

    def ones_block(size):
        lane = lax.broadcasted_iota(jnp.int32, (size, HEAD_PAD - MLA_VDIM), 1)
        return jnp.where(lane == 0, 1.0, 0.0).astype(BF16)

    def update(j, m_chunk, m_prev, acc):
        off, size = chunks[j]
        m_new = jnp.maximum(m_prev, m_chunk)
        alpha = jnp.exp2(m_prev - m_new)
        p = jnp.exp2(bufs[j % 2][0:rows, 0:size] - m_new).astype(BF16)
        v_ext = jnp.concatenate([v_ref[off:off + size, :], ones_block(size)], axis=1)
        return m_new, alpha * acc + _dot(p, v_ext)

    m = jnp.full((rows, 1), NEG_BIG, F32)
    acc = jnp.zeros((rows, HEAD_PAD), F32)
    m_next = scores_into(0)
    for j in range(len(chunks)):
        m_chunk = m_next
        if j + 1 < len(chunks):
            m_next = scores_into(j + 1)
        m, acc = update(j, m_chunk, m, acc)
    return acc[:, :MLA_VDIM] / acc[:, MLA_VDIM:MLA_VDIM + 1]


def _flash_kernel(qn_ref, qr_ref, kn_ref, kr_ref, v_ref, g_ref, o_ref, sa_ref, sb_ref):
    i = pl.program_id(1)
    chunks = _key_chunks()
    refs = (kn_ref, kr_ref, v_ref, sa_ref, sb_ref)

    def queries(lo, hi):
        return jnp.concatenate([qn_ref[lo:hi, :], qr_ref[lo:hi, :]], axis=1)

    @pl.when(i < FL_LAST)
    def _():
        o = _attend(queries(0, FL_TQ), *refs, chunks)
        o_ref[...] = (o * _silu(g_ref[...])).astype(BF16)

    @pl.when(i == FL_LAST)
    def _():
        nl = FL_LAT_IN_LAST
        o = _attend(queries(0, nl), *refs, chunks)
        o_ref[0:nl, :] = (o * _silu(g_ref[0:nl, :])).astype(BF16)
        oc = _attend(queries(nl, FL_TQ), *refs, [(SEQ, CTX_LEN)])
        o_ref[nl:, :] = (oc * _silu(g_ref[nl:, :])).astype(BF16)


def _flash(qn, qr, kn, kr, v, u):
    return pl.pallas_call(
        _flash_kernel,
        grid=(MLA_HEADS, ROWS // FL_TQ),
        in_specs=[pl.BlockSpec((FL_TQ, MLA_NOPE), lambda h, i: (i, h)),
                  pl.BlockSpec((FL_TQ, LANES), lambda h, i: (i, h // 2)),
                  pl.BlockSpec((ROWS, MLA_NOPE), lambda h, i: (0, h)),
                  pl.BlockSpec((ROWS, LANES), lambda h, i: (0, h % 2)),
                  pl.BlockSpec((ROWS, MLA_VDIM), lambda h, i: (0, h)),
                  pl.BlockSpec((FL_TQ, MLA_VDIM), lambda h, i: (i, EV_COL_AGATE // MLA_VDIM + h))],
        out_specs=pl.BlockSpec((FL_TQ, MLA_VDIM), lambda h, i: (i, h)),
        out_shape=jax.ShapeDtypeStruct((ROWS, MLA_W), BF16),
        scratch_shapes=[pltpu.VMEM((FL_TQ, FL_TK), F32), pltpu.VMEM((FL_TQ, FL_TK), F32)],
        name="mla_attention",
    )(qn, qr, kn, kr, v, u)


def _pool_kernel(prev_ref, cur_ref, next_ref, gate_ref, w_ref, sc_ref, o_ref):
    i = pl.program_id(0)
    first = jnp.logical_or(i == 0, i == N_LAT_BLK)
    last = i >= N_LAT_BLK - 1
    lo_lim = jnp.where(first, 0, -POOL_HALO)
    hi_lim = jnp.where(last, ROW_BLK - 1, ROW_BLK - 1 + POOL_HALO)
    src_rows = ROW_BLK + 2 * POOL_HALO
    t = lax.broadcasted_iota(jnp.int32, (ROW_BLK, src_rows), 0)
    j = lax.broadcasted_iota(jnp.int32, (ROW_BLK, src_rows), 1) - POOL_HALO
    t1 = lax.broadcasted_iota(jnp.int32, (ROW_BLK, 1), 0)
    for g, w in enumerate(POOL_WINDOWS):
        sl = slice(g * POOL_GC, (g + 1) * POOL_GC)
        cur = cur_ref[:, sl]
        src = jnp.concatenate([prev_ref[ROW_BLK - POOL_HALO:, sl], cur, next_ref[:POOL_HALO, sl]], axis=0)
        lo = jnp.maximum(t - w // 2, lo_lim)
        hi = jnp.minimum(t + w // 2 - 1, hi_lim)
        band = jnp.where(jnp.logical_and(j >= lo, j <= hi), 1.0, 0.0).astype(BF16)
        cnt = (jnp.minimum(t1 + w // 2 - 1, hi_lim) - jnp.maximum(t1 - w // 2, lo_lim) + 1).astype(F32)
        s_hi, s_mid, s_lo = _split3(src)
        wsum = _dot(band, s_hi) + _dot(band, s_mid) + _dot(band, s_lo)
        z = (wsum / cnt - cur).astype(BF16)
        y = _dot(z, w_ref[g]) * sc_ref[:, sl]
        o_ref[:, sl] = (y * _silu(gate_ref[:, sl])).astype(BF16)


def _pool(u, w_pool, pool_scale):
    def prev_map(i):
        return (jnp.where(i < N_LAT_BLK, jnp.maximum(i - 1, 0), N_LAT_BLK), 0)

    def next_map(i):
        return (jnp.where(i < N_LAT_BLK, jnp.minimum(i + 1, N_LAT_BLK - 1), N_LAT_BLK), 0)

    return pl.pallas_call(
        _pool_kernel,
        grid=(N_BLK,),
        in_specs=[pl.BlockSpec((ROW_BLK, POOL_W), prev_map),
                  pl.BlockSpec((ROW_BLK, POOL_W), lambda i: (i, 0)),
                  pl.BlockSpec((ROW_BLK, POOL_W), next_map),
                  pl.BlockSpec((ROW_BLK, POOL_W), lambda i: (i, 1)),
                  pl.BlockSpec((4, POOL_GC, POOL_GC), lambda i: (0, 0, 0)),
                  pl.BlockSpec((1, POOL_W), lambda i: (0, 0))],
        out_specs=pl.BlockSpec((ROW_BLK, POOL_W), lambda i: (i, 0)),
        out_shape=jax.ShapeDtypeStruct((ROWS, POOL_W), BF16),
        name="pool_mix",
    )(u, u, u, u, w_pool.astype(BF16), pool_scale.reshape(1, POOL_W))


def _outproj_residual(a_ref, b_ref, w_ref, x_ref, c_ref, mod_ref, wbf_ref):
    i = pl.program_id(0)

    @pl.when(i == 0)
    def _():
        wbf_ref[...] = w_ref[...].astype(BF16)

    k1 = a_ref.shape[1]
    y = _dot(a_ref[...], wbf_ref[0:k1, :]) + _dot(b_ref[...], wbf_ref[k1:, :])
    return _split_rows(x_ref, c_ref, i) + mod_ref[0, 2:3, :] * y


def _outproj_kernel(a_ref, b_ref, w_ref, x_ref, c_ref, mod_ref, nw_ref, modn_ref, xo_ref, h_ref, wbf_ref):
    xn = _outproj_residual(a_ref, b_ref, w_ref, x_ref, c_ref, mod_ref, wbf_ref)
    xo_ref[...] = xn
    h_ref[...] = _modnorm(xn, nw_ref[...], modn_ref[0]).astype(BF16)


def _outproj_final_kernel(a_ref, b_ref, w_ref, x_ref, c_ref, mod_ref, nw_ref, o_ref, wbf_ref):
    xn = _outproj_residual(a_ref, b_ref, w_ref, x_ref, c_ref, mod_ref, wbf_ref)
    o_ref[...] = _rms(xn) * nw_ref[...]


def _out_proj(a, b, w, layer, x, x_ctx, mod, nw_next, mod_next):
    k1, k2 = a.shape[1], b.shape[1]
    n_blk = N_LAT_BLK if mod_next is None else N_BLK
    row = lambda i: (i, 0)
    full = lambda i: (0, 0)
    stream = lambda i: (_is_ctx_blk(i), 0, 0)
    ctx_blk = x_ctx.shape[0] // CTX_LEN - 1
    common = [pl.BlockSpec((ROW_BLK, k1), row), pl.BlockSpec((ROW_BLK, k2), row),
              pl.BlockSpec((None, D_MODEL, D_MODEL), lambda i: (layer, 0, 0), pipeline_mode=pl.Buffered(1)),
              pl.BlockSpec((ROW_BLK, D_MODEL), lambda i: (jnp.minimum(i, N_LAT_BLK - 1), 0)),
              pl.BlockSpec((CTX_LEN, D_MODEL), lambda i: (ctx_blk, 0)),
              pl.BlockSpec((1, 8, D_MODEL), stream),
              pl.BlockSpec((1, D_MODEL), full)]
    scratch = [pltpu.VMEM((D_MODEL, D_MODEL), BF16)]
    params = pltpu.CompilerParams(dimension_semantics=("arbitrary",))
    if mod_next is None:
        return pl.pallas_call(
            _outproj_final_kernel,
            grid=(n_blk,),
            in_specs=common,
            out_specs=pl.BlockSpec((ROW_BLK, D_MODEL), row),
            out_shape=jax.ShapeDtypeStruct((SEQ, D_MODEL), F32),
            scratch_shapes=scratch,
            compiler_params=params,
            name="out_proj_final",
        )(a, b, w, x, x_ctx, mod, nw_next.reshape(1, D_MODEL))
    return pl.pallas_call(
        _outproj_kernel,
        grid=(n_blk,),
        in_specs=common + [pl.BlockSpec((1, 8, D_MODEL), stream)],
        out_specs=[pl.BlockSpec((ROW_BLK, D_MODEL), row), pl.BlockSpec((ROW_BLK, D_MODEL), row)],
        out_shape=[jax.ShapeDtypeStruct((ROWS, D_MODEL), F32), jax.ShapeDtypeStruct((ROWS, D_MODEL), BF16)],
        scratch_shapes=scratch,
        compiler_params=params,
        name="out_proj",
    )(a, b, w, x, x_ctx, mod, nw_next.reshape(1, D_MODEL), mod_next)


M_QSCALE = M_DQK ** -0.5


def _log_sigmoid(x):
    return jnp.minimum(x, 0.0) - jnp.log(1.0 + jnp.exp(-jnp.abs(x)))


def _mlstm_products(q_ref, k_ref, c_ref, n_ref, qk_ref, cq_ref, nq_ref, reverse):
    d = 1 if reverse else 0
    for h in range(M_HEADS):
        st = M_HEADS * d + h
        q = (q_ref[:, h * M_DQK:(h + 1) * M_DQK] * M_QSCALE).astype(BF16)
        qk_ref[st] = _dot_nt(k_ref[:, h * M_DQK:(h + 1) * M_DQK].astype(BF16), q)
        cq_ref[st] = _dot_nt(c_ref[st].astype(BF16), q)
        nq_ref[st] = _dot_nt(jnp.broadcast_to(n_ref[st:st + 1, :], (8, M_DQK)).astype(BF16), q)


def _mlstm_dir(k_ref, v_ref, g_ref, gb_ref, ht_ref, c_ref, n_ref, m_ref, qk_ref, cq_ref, nq_ref, reverse):
    T = M_CHUNK
    d = 1 if reverse else 0
    base = 2 * M_HEADS * d
    g = g_ref[...] + gb_ref[...]
    glf = _log_sigmoid(g)
    row = lax.broadcasted_iota(jnp.int32, (T, T), 0)
    col = lax.broadcasted_iota(jnp.int32, (T, T), 1)
    tri = jnp.where((col >= row) if reverse else (col <= row), 1.0, 0.0).astype(BF16)
    reach = (row >= col) if reverse else (row <= col)
    p_hi, p_mid, p_lo = _split3(glf)
    bcum = _dot(tri, p_hi) + _dot(tri, p_mid) + _dot(tri, p_lo)
    g_sh = pltpu.roll(g, M_HEADS, 1)
    r_all = g_sh - bcum
    end = 0 if reverse else T - 1
    b_last = bcum[end:end + 1, :]
    a_all = b_last - bcum + g_sh
    m_prev_row = m_ref[d:d + 1, :]
    m_new_row = jnp.maximum(b_last + m_prev_row, jnp.max(a_all, axis=0, keepdims=True))
    decay_row = jnp.exp(b_last + m_prev_row - m_new_row)
    w_all = jnp.exp(a_all - m_new_row)
    m_ref[d:d + 1, :] = m_new_row
    b_t = bcum.T
    for h in range(M_HEADS):
        cf = base + M_HEADS + h
        st = M_HEADS * d + h
        m_prev = m_prev_row[:, cf:cf + 1]
        decay = decay_row[:, cf:cf + 1]
        r_mat = jnp.where(reach, r_all[:, cf:cf + 1], -jnp.inf)
        mx = jnp.maximum(m_prev, jnp.max(r_mat, axis=0, keepdims=True))
        iw = jnp.exp(m_prev - mx)
        dm = jnp.exp(r_mat - mx)

        v = v_ref[:, h * M_DV:(h + 1) * M_DV].astype(BF16)
        s = qk_ref[st] * dm
        den = iw * nq_ref[st, 0:1, :] + jnp.sum(s, axis=0, keepdims=True)
        inv = 1.0 / jnp.maximum(jnp.abs(den), jnp.exp(-(b_t[cf:cf + 1, :] + mx)))
        ht_ref[h * M_DV:(h + 1) * M_DV, :] = _dot_tn(v, (s * inv).astype(BF16)) + cq_ref[st] * (iw * inv)

        wk = w_all[:, cf:cf + 1] * k_ref[:, h * M_DQK:(h + 1) * M_DQK]
        c_ref[st] = decay * c_ref[st] + _dot_tn(v, wk.astype(BF16))
        n_ref[st:st + 1, :] = decay * n_ref[st:st + 1, :] + jnp.sum(wk, axis=0, keepdims=True)


def _mlstm_kernel(qf_ref, kf_ref, vf_ref, gf_ref, qb_ref, kb_ref, vb_ref, gb_ref, bias_ref,
                  hf_ref, hb_ref, c_ref, n_ref, m_ref, qk_ref, cq_ref, nq_ref):
    @pl.when(pl.program_id(0) == 0)
    def _():
        c_ref[...] = jnp.zeros_like(c_ref)
        n_ref[...] = jnp.zeros_like(n_ref)
        m_ref[...] = jnp.zeros_like(m_ref)

    products = (qk_ref, cq_ref, nq_ref)
    _mlstm_products(qf_ref, kf_ref, c_ref, n_ref, *products, reverse=False)
    _mlstm_products(qb_ref, kb_ref, c_ref, n_ref, *products, reverse=True)
    _mlstm_dir(kf_ref, vf_ref, gf_ref, bias_ref, hf_ref, c_ref, n_ref, m_ref, *products, reverse=False)
    _mlstm_dir(kb_ref, vb_ref, gb_ref, bias_ref, hb_ref, c_ref, n_ref, m_ref, *products, reverse=True)


def _mlstm(u, gate_b):
    fwd = lambda k: jnp.where(k == 0, N_LAT_BLK, k - 1)
    bwd = lambda k: jnp.where(k == 0, N_LAT_BLK, N_LAT_BLK - k)
    mg_blk = OD_COL_MG // LANES

    def specs(rmap):
        return [pl.BlockSpec((M_CHUNK, M_QK_W), lambda k: (rmap(k), 0)),
                pl.BlockSpec((M_CHUNK, M_QK_W), lambda k: (rmap(k), 1)),
                pl.BlockSpec((M_CHUNK, M_V_W), lambda k: (rmap(k), 1)),
                pl.BlockSpec((M_CHUNK, LANES), lambda k: (rmap(k), mg_blk))]

    bias = jnp.concatenate([gate_b.astype(F32), jnp.zeros((LANES - 4 * M_HEADS,), F32)]).reshape(1, LANES)
    return pl.pallas_call(
        _mlstm_kernel,
        grid=(N_BLK,),
        in_specs=specs(fwd) + specs(bwd) + [pl.BlockSpec((1, LANES), lambda k: (0, 0))],
        out_specs=[pl.BlockSpec((M_V_W, M_CHUNK), lambda k: (0, fwd(k))),
                   pl.BlockSpec((M_V_W, M_CHUNK), lambda k: (0, bwd(k)))],
        out_shape=[jax.ShapeDtypeStruct((M_V_W, ROWS), F32), jax.ShapeDtypeStruct((M_V_W, ROWS), F32)],
        scratch_shapes=[pltpu.VMEM((2 * M_HEADS, M_DV, M_DQK), F32),
                        pltpu.VMEM((2 * M_HEADS, M_DQK), F32),
                        pltpu.VMEM((8, LANES), F32),
                        pltpu.VMEM((2 * M_HEADS, M_CHUNK, M_CHUNK), F32),
                        pltpu.VMEM((2 * M_HEADS, M_DV, M_CHUNK), F32),
                        pltpu.VMEM((2 * M_HEADS, 8, M_CHUNK), F32)],
        compiler_params=pltpu.CompilerParams(dimension_semantics=("arbitrary",)),
        name="mlstm_scan",
    )(u, u, u, u, u, u, u, u, bias)


def _mlstm_post_kernel(hf_ref, hb_ref, o_ref, z_ref, nw_ref, y_ref):
    for h in range(M_HEADS):
        sl = slice(h * M_DV, (h + 1) * M_DV)
        ht = hf_ref[sl, :] + hb_ref[sl, :]
        hn = (ht * lax.rsqrt(jnp.mean(ht * ht, axis=0, keepdims=True) + EPS)).T * nw_ref[:, sl]
        y_ref[:, sl] = (hn * _sigmoid(o_ref[:, sl]) * _silu(z_ref[:, sl])).astype(BF16)


def _mlstm_post(hf, hb, u, head_norm):
    row = lambda i: (i, 0)
    col = lambda i: (0, i)
    t = MM_ROWS
    return pl.pallas_call(
        _mlstm_post_kernel,
        grid=(ROWS // t,),
        in_specs=[pl.BlockSpec((M_V_W, t), col), pl.BlockSpec((M_V_W, t), col),
                  pl.BlockSpec((t, M_V_W), lambda i: (i, OD_COL_MO // M_V_W)),
                  pl.BlockSpec((t, M_V_W), lambda i: (i, OD_COL_MZ // M_V_W)),
                  pl.BlockSpec((1, M_V_W), lambda i: (0, 0))],
        out_specs=pl.BlockSpec((t, M_V_W), row),
        out_shape=jax.ShapeDtypeStruct((ROWS, M_V_W), BF16),
        name="mlstm_post",
    )(hf, hb, u, u, head_norm.reshape(1, M_V_W))


W_NLAT = SEQ // WINDOW
W_NBLK = ROWS // WINDOW
W_QSCALE = (A_HD ** -0.5) * LOG2E
W_KEYS = 3 * WINDOW + CTX_LEN
W_PAIRS = A_HEADS // A_KV_HEADS // 2


def _win_kernel(sink_ref, q_ref, az0_ref, az1_ref, az2_ref, az3_ref, kvp_ref, kvc_ref, kvn_ref, kvx_ref,
                tq_ref, tp_ref, tn_ref, o_ref, s_ref):
    az_refs = (az0_ref, az1_ref, az2_ref, az3_ref)
    bi = pl.program_id(0)
    is_lat = bi < W_NLAT
    tq = (tq_ref[0], tq_ref[1], tq_ref[2])
    lane_k = lax.broadcasted_iota(jnp.int32, (W_KEYS, LANES), 1)

    kj = lax.broadcasted_iota(jnp.int32, (WINDOW, WINDOW), 0)
    qi = lax.broadcasted_iota(jnp.int32, (WINDOW, WINDOW), 1)
    ok_prev = jnp.logical_and(jnp.logical_and(is_lat, bi >= 1), kj >= qi)
    ok_cur = jnp.logical_and(is_lat, kj >= 0)
    ok_next = jnp.logical_and(jnp.logical_and(is_lat, bi <= W_NLAT - 2), kj <= qi)
    ok_ctx = lax.broadcasted_iota(jnp.int32, (CTX_LEN, WINDOW), 1) >= 0
    valid = jnp.concatenate([ok_prev, ok_cur, ok_next, ok_ctx], axis=0)
    valid4 = jnp.concatenate([valid] * W_PAIRS, axis=1)
    lane4 = lax.broadcasted_iota(jnp.int32, (1, W_PAIRS * WINDOW), 1)

    kk = jnp.concatenate([_rope(kvp_ref[:, 0:LANES], tp_ref[0], tp_ref[1], tp_ref[2]),
                          _rope(kvc_ref[:, 0:LANES], *tq),
                          _rope(kvn_ref[:, 0:LANES], tn_ref[0], tn_ref[1], tn_ref[2]),
                          kvx_ref[:, 0:LANES]], axis=0)
    vv = jnp.concatenate([kvp_ref[:, LANES:2 * LANES], kvc_ref[:, LANES:2 * LANES],
                          kvn_ref[:, LANES:2 * LANES], kvx_ref[:, LANES:2 * LANES]], axis=0)
    kk_sw = pltpu.roll(kk, A_HD, 1)
    vv_sw = pltpu.roll(vv, A_HD, 1)
    low = lane_k < A_HD

    v_halves = []
    maxima = []
    for g in range(A_KV_HEADS):
        k_src = (kk, kk_sw) if g == 0 else (kk_sw, kk)
        v_src = (vv, vv_sw) if g == 0 else (vv_sw, vv)
        k_half = [jnp.where(low, k_src[0], 0.0).astype(BF16), jnp.where(low, 0.0, k_src[1]).astype(BF16)]
        v_halves.append([jnp.where(low, v_src[0], 0.0).astype(BF16), jnp.where(low, 0.0, v_src[1]).astype(BF16)])
        cols = [(g * W_PAIRS + p) * LANES for p in range(W_PAIRS)]
        qs = jnp.concatenate([(_rope(q_ref[:, c:c + LANES], *tq) * W_QSCALE).astype(BF16) for c in cols],
                             axis=0)
        for par in range(2):
            s = jnp.where(valid4, _dot_nt(k_half[par], qs), -jnp.inf)
            s_ref[2 * g + par] = s
            maxima.append(jnp.max(s, axis=0, keepdims=True))

    for g in range(A_KV_HEADS):
        cols = [(g * W_PAIRS + p) * LANES for p in range(W_PAIRS)]
        out_t = jnp.zeros((LANES, W_PAIRS * WINDOW), F32)
        for par in range(2):
            snk = jnp.full((1, W_PAIRS * WINDOW), sink_ref[g * 8 + par] * LOG2E, F32)
            for p in range(1, W_PAIRS):
                snk = jnp.where(lane4 >= p * WINDOW, sink_ref[g * 8 + 2 * p + par] * LOG2E, snk)
            m = jnp.maximum(maxima[2 * g + par], snk)
            e = jnp.exp2(s_ref[2 * g + par] - m)
            den = jnp.sum(e, axis=0, keepdims=True) + jnp.exp2(snk - m)
            out_t = out_t + _dot_tn(v_halves[g][par], e.astype(BF16)) * (1.0 / den)
        out = out_t.T
        for p, c in enumerate(cols):
            grp = c // LANES
            az = az_refs[grp // 2][:, (grp % 2) * LANES:(grp % 2 + 1) * LANES]
            o_ref[:, c:c + LANES] = (out[p * WINDOW:(p + 1) * WINDOW] * _silu(az)).astype(BF16)


def _win_attn(u, sink, tabs):
    kvb = OD_COL_KV // OD_KV_W
    cur = lambda b: jnp.minimum(b, W_NLAT - 1)
    prv = lambda b: jnp.clip(b - 1, 0, W_NLAT - 1)
    nxt = lambda b: jnp.clip(b + 1, 0, W_NLAT - 1)
    tab = lambda f: pl.BlockSpec((3, WINDOW, LANES), lambda b: (0, f(b), 0))
    az_spec = lambda n: pl.BlockSpec((WINDOW, 2 * LANES), lambda b: (b, OD_COL_AZ // (2 * LANES) + n))
    return pl.pallas_call(
        _win_kernel,
        grid=(W_NBLK,),
        in_specs=[pl.BlockSpec(memory_space=pltpu.SMEM),
                  pl.BlockSpec((WINDOW, A_W), lambda b: (b, OD_COL_AQ // A_W)),
                  az_spec(0), az_spec(1), az_spec(2), az_spec(3),
                  pl.BlockSpec((WINDOW, OD_KV_W), lambda b: (prv(b), kvb)),
                  pl.BlockSpec((WINDOW, OD_KV_W), lambda b: (cur(b), kvb)),
                  pl.BlockSpec((WINDOW, OD_KV_W), lambda b: (nxt(b), kvb)),
                  pl.BlockSpec((CTX_LEN, OD_KV_W), lambda b: (SEQ // CTX_LEN, kvb)),
                  tab(lambda b: b), tab(prv), tab(nxt)],
        out_specs=pl.BlockSpec((WINDOW, A_W), lambda b: (b, 0)),
        out_shape=jax.ShapeDtypeStruct((ROWS, A_W), BF16),
        scratch_shapes=[pltpu.VMEM((2 * A_KV_HEADS, W_KEYS, W_PAIRS * WINDOW), F32)],
        name="window_attention",
    )(sink, u, u, u, u, u, u, u, u, u, tabs, tabs, tabs)


def kernel(x, c, ctx, c_ctx, ada_w, ada_b, norm_w, ev_w_in, ev_q_norm, ev_kv_norm, ev_w_uq, ev_w_ukv,
           ev_w_pool, ev_pool_scale, ev_w_out, od_w_in, od_gate_b, od_head_norm, od_sink, od_w_out, final_norm):
    tabs = _rope_tables()
    mod = _ada(c, c_ctx, ada_w, ada_b)

    n_ev = ev_w_in.shape[0]
    ev_w = _ev_layout(ev_w_in)
    od_w = _od_layout(od_w_in)
    wq = ev_w_uq.astype(BF16).reshape(n_ev, Q_RANK, MLA_HEADS, MLA_QK)
    wq_nope = wq[..., :MLA_NOPE].reshape(n_ev, Q_RANK, MLA_HEADS * MLA_NOPE)
    wq_rope = wq[..., MLA_NOPE:].reshape(n_ev, Q_RANK, MLA_HEADS * ROPE_DIM)
    wkv = ev_w_ukv.astype(BF16).reshape(n_ev, KV_RANK, MLA_HEADS, MLA_NOPE + MLA_VDIM)
    wk_nope = wkv[..., :MLA_NOPE].reshape(n_ev, KV_RANK, MLA_HEADS * MLA_NOPE)
    wv = wkv[..., MLA_NOPE:].reshape(n_ev, KV_RANK, MLA_W)
    q_norm = ev_q_norm.reshape(n_ev, 1, Q_RANK)
    kv_norm = ev_kv_norm.reshape(n_ev, 1, KV_RANK)

    h = _prenorm(x[0], ctx[0], norm_w[0], mod[0])
    res_lat, res_ctx = x[0], ctx[0]
    out = None
    for i in range(DEPTH):
        j = i // 2
        if i % 2 == 0:
            u = _in_proj(h, ev_w, j, EV_TN)
            qn, qr = _q_proj(u, j, q_norm, wq_nope, wq_rope, tabs)
            kn, kr, v = _kv_proj(u, j, kv_norm, wk_nope, wv, tabs)
            a = _pool(u, ev_w_pool[j], ev_pool_scale[j])
            b = _flash(qn, qr, kn, kr, v, u)
            w_out = ev_w_out
        else:
            u = _in_proj(h, od_w, j, OD_TN)
            hf, hb = _mlstm(u, od_gate_b[j])
            a = _mlstm_post(hf, hb, u, od_head_norm[j])
            b = _win_attn(u, od_sink[j].astype(F32), tabs)
            w_out = od_w_out
        if i < DEPTH - 1:
            xs, h = _out_proj(a, b, w_out, j, res_lat, res_ctx, mod[i], norm_w[i + 1], mod[i + 1])
            res_lat = res_ctx = xs
        else:
            out = _out_proj(a, b, w_out, j, res_lat, res_ctx, mod[i], final_norm, None)
    return out[None]
```

```python
import functools
import math

import jax
import jax.numpy as jnp
from jax import lax
from jax.experimental import pallas as pl
from jax.experimental.pallas import tpu as pltpu

F32 = jnp.float32
BF16 = jnp.bfloat16

D_MODEL = 2048
SEQ = 8192
CTX_LEN = 256
ROWS = SEQ + CTX_LEN
DEPTH = 4
GRID_W = 64
EPS = 1e-6
ROPE_THETA = 10000.0
ROPE_DIM = 64
LOG2E = math.log2(math.e)

LANES = 128
ROW_BLK = 256
N_LAT_BLK = SEQ // ROW_BLK
N_BLK = ROWS // ROW_BLK
MM_ROWS = 768
IN_ROWS = 384

POOL_WINDOWS = (2, 4, 8, 16)
POOL_W = 512
POOL_GC = 128
POOL_HALO = 16

MLA_HEADS = 12
MLA_NOPE = 128
MLA_VDIM = 128
MLA_QK = MLA_NOPE + ROPE_DIM
MLA_W = MLA_HEADS * MLA_VDIM
Q_RANK = 512
KV_RANK = 512
HEAD_PAD = 256

M_HEADS = 4
M_DQK = 128
M_DV = 256
M_QK_W = 512
M_V_W = 1024
M_CHUNK = 256

A_HEADS = 16
A_KV_HEADS = 2
A_HD = 64
A_W = 1024
WINDOW = 128

NEG_BIG = -1e30

EV_N = 3840
EV_COL_CQ = 1024
EV_COL_CKV = 1536
EV_COL_AGATE = 2048
EV_COL_KR = 3584
EV_TN = 3840

OD_N = 6656
OD_COL_MO = 2048
OD_COL_MZ = 3072
OD_COL_AQ = 4096
OD_COL_KV = 5120
OD_KV_W = 256
OD_COL_AZ = 5376
OD_COL_MG = 6400
OD_TN = 3328

PREP_W = 256


def _prep_kernel(a_ref, b_ref, o_ref, *, n_copy, n_shift, shift):
    t = pl.program_id(1)

    @pl.when(t < n_copy)
    def _():
        o_ref[...] = a_ref[...].astype(BF16).T

    @pl.when(jnp.logical_and(t >= n_copy, t < n_copy + n_shift))
    def _():
        o_ref[...] = jnp.concatenate([a_ref[shift:, :], b_ref[...]], axis=0).astype(BF16).T

    @pl.when(t == n_copy + n_shift)
    def _():
        pad = jnp.zeros((PREP_W - shift, b_ref.shape[1]), F32)
        o_ref[...] = jnp.concatenate([b_ref[...], pad], axis=0).astype(BF16).T


def _prep_weight(w, n_copy, n_shift, shift):
    layers, d, n = w.shape
    n_tiles = n_copy + n_shift + 1
    per = PREP_W // shift
    a_map = lambda l, t: (l, jnp.minimum(t, n_copy + n_shift - 1), 0)
    b_map = lambda l, t: (l, jnp.where(t == n_tiles - 1, n_copy * per, jnp.minimum((t + 1) * per, n // shift - 1)), 0)
    return pl.pallas_call(
        functools.partial(_prep_kernel, n_copy=n_copy, n_shift=n_shift, shift=shift),
        grid=(layers, n_tiles),
        in_specs=[pl.BlockSpec((None, PREP_W, d), a_map),
                  pl.BlockSpec((None, shift, d), b_map)],
        out_specs=pl.BlockSpec((None, d, PREP_W), lambda l, t: (l, 0, t)),
        out_shape=jax.ShapeDtypeStruct((layers, d, n_tiles * PREP_W), BF16),
        name="weight_layout",
    )(jnp.swapaxes(w, 1, 2), jnp.swapaxes(w, 1, 2))


def _ev_layout(w):
    return _prep_weight(w, n_copy=8, n_shift=6, shift=ROPE_DIM)


def _od_layout(w):
    return _prep_weight(w, n_copy=12, n_shift=13, shift=4 * M_HEADS)


def _rope_tables():
    t = jnp.arange(SEQ)
    row = (t // GRID_W).astype(F32)
    col = (t % GRID_W).astype(F32)
    half = ROPE_DIM // 2
    inv = ROPE_THETA ** (-jnp.arange(0, half, 2, dtype=F32) / half)
    ar = row[:, None] * inv
    ac = col[:, None] * inv
    zero = jnp.zeros_like(ar)
    cos64 = jnp.concatenate([jnp.cos(ar), jnp.cos(ar), jnp.cos(ac), jnp.cos(ac)], axis=-1)
    sa64 = jnp.concatenate([-jnp.sin(ar), zero, -jnp.sin(ac), zero], axis=-1)
    sb64 = jnp.concatenate([zero, jnp.sin(ar), zero, jnp.sin(ac)], axis=-1)
    ctx_zero = jnp.zeros((CTX_LEN, LANES), F32)
    cos = jnp.concatenate([jnp.tile(cos64, (1, 2)), jnp.ones((CTX_LEN, LANES), F32)], axis=0)
    sin_a = jnp.concatenate([jnp.tile(sa64, (1, 2)), ctx_zero], axis=0)
    sin_b = jnp.concatenate([jnp.tile(sb64, (1, 2)), ctx_zero], axis=0)
    return jnp.stack([cos, sin_a, sin_b])


def _rope(x, cos, sin_a, sin_b):
    up = pltpu.roll(x, LANES - ROPE_DIM // 4, 1)
    down = pltpu.roll(x, ROPE_DIM // 4, 1)
    return x * cos + up * sin_a + down * sin_b


def _dot(a, b):
    return jnp.dot(a, b, preferred_element_type=F32)


def _dot_nt(a, b):
    return lax.dot_general(a, b, (((1,), (1,)), ((), ())), preferred_element_type=F32)


def _dot_tn(a, b):
    return lax.dot_general(a, b, (((0,), (0,)), ((), ())), preferred_element_type=F32)


def _rms(x):
    return x * lax.rsqrt(jnp.mean(x * x, axis=-1, keepdims=True) + EPS)


def _modnorm(x, nw, mod):
    return (_rms(x) * nw) * (1.0 + mod[1:2, :]) + mod[0:1, :]


def _silu(x):
    return x * (1.0 / (1.0 + jnp.exp(-x)))


def _sigmoid(x):
    return 1.0 / (1.0 + jnp.exp(-x))


def _split3(x):
    hi = x.astype(BF16)
    r1 = x - hi.astype(F32)
    mid = r1.astype(BF16)
    lo = (r1 - mid.astype(F32)).astype(BF16)
    return hi, mid, lo


def _is_ctx_blk(i):
    return jnp.where(i < N_LAT_BLK, 0, 1)


def _ada_kernel(s_ref, w_ref, b_ref, o_ref):
    s = _silu(s_ref[...])
    hi = s.astype(BF16)
    lo = (s - hi.astype(F32)).astype(BF16)
    lhs = jnp.concatenate([hi, lo], axis=0)
    r = _dot(lhs, w_ref[0].astype(BF16))
    o_ref[0] = r[0:8] + r[8:16] + b_ref[0]


def _ada(c, c_ctx, ada_w, ada_b):
    tn = 1536
    s = jnp.concatenate([c.reshape(1, D_MODEL), c_ctx.reshape(1, D_MODEL), jnp.zeros((6, D_MODEL), F32)], axis=0)
    out = pl.pallas_call(
        _ada_kernel,
        grid=(DEPTH, 3 * D_MODEL // tn),
        in_specs=[pl.BlockSpec((8, D_MODEL), lambda l, j: (0, 0)),
                  pl.BlockSpec((1, D_MODEL, tn), lambda l, j: (l, 0, j)),
                  pl.BlockSpec((1, 1, tn), lambda l, j: (l, 0, j))],
        out_specs=pl.BlockSpec((1, 8, tn), lambda l, j: (l, 0, j)),
        out_shape=jax.ShapeDtypeStruct((DEPTH, 8, 3 * D_MODEL), F32),
        name="ada_mod",
    )(s, ada_w, ada_b.reshape(DEPTH, 1, 3 * D_MODEL))
    m = out[:, :2, :].reshape(DEPTH, 2, 3, D_MODEL)
    return jnp.concatenate([m, jnp.zeros((DEPTH, 2, 5, D_MODEL), F32)], axis=2)


def _split_rows_specs():
    return [pl.BlockSpec((ROW_BLK, D_MODEL), lambda i: (jnp.minimum(i, N_LAT_BLK - 1), 0)),
            pl.BlockSpec((CTX_LEN, D_MODEL), lambda i: (0, 0))]


def _split_rows(x_ref, c_ref, blk):
    return jnp.where(blk < N_LAT_BLK, x_ref[...], c_ref[...])


def _prenorm_kernel(x_ref, c_ref, nw_ref, mod_ref, h_ref):
    x = _split_rows(x_ref, c_ref, pl.program_id(0))
    h_ref[...] = _modnorm(x, nw_ref[...], mod_ref[0]).astype(BF16)


def _prenorm(x, ctx, nw, mod):
    return pl.pallas_call(
        _prenorm_kernel,
        grid=(N_BLK,),
        in_specs=_split_rows_specs() + [pl.BlockSpec((1, D_MODEL), lambda i: (0, 0)),
                                        pl.BlockSpec((1, 8, D_MODEL), lambda i: (_is_ctx_blk(i), 0, 0))],
        out_specs=pl.BlockSpec((ROW_BLK, D_MODEL), lambda i: (i, 0)),
        out_shape=jax.ShapeDtypeStruct((ROWS, D_MODEL), BF16),
        name="prenorm",
    )(x, ctx, nw.reshape(1, D_MODEL), mod)


def _mm_kernel(h_ref, w_ref, o_ref):
    o_ref[...] = _dot(h_ref[...], w_ref[...])


def _in_proj(h, w, layer, tn):
    n = w.shape[2]
    return pl.pallas_call(
        _mm_kernel,
        grid=(n // tn, ROWS // IN_ROWS),
        in_specs=[pl.BlockSpec((IN_ROWS, D_MODEL), lambda j, i: (i, 0)),
                  pl.BlockSpec((None, D_MODEL, tn), lambda j, i: (layer, 0, j))],
        out_specs=pl.BlockSpec((IN_ROWS, tn), lambda j, i: (i, j)),
        out_shape=jax.ShapeDtypeStruct((ROWS, n), F32),
        name="in_proj",
    )(h, w)


Q_SCALE = (MLA_QK ** -0.5) * LOG2E


def _qproj_kernel(cq_ref, nw_ref, wn_ref, wr_ref, tab_ref, qn_ref, qr_ref):
    xn = (_rms(cq_ref[...]) * nw_ref[...]).astype(BF16)
    qn_ref[...] = (_dot(xn, wn_ref[...]) * Q_SCALE).astype(BF16)
    qr = _dot(xn, wr_ref[...])
    for p in range(MLA_HEADS // 2):
        sl = slice(p * LANES, (p + 1) * LANES)
        qr_ref[:, sl] = (_rope(qr[:, sl], tab_ref[0], tab_ref[1], tab_ref[2]) * Q_SCALE).astype(BF16)


def _layer_spec(w, layer):
    return pl.BlockSpec((None,) + w.shape[1:], lambda i: (layer, 0, 0))


def _q_proj(u, layer, nw, wn, wr, tabs):
    t = MM_ROWS
    return pl.pallas_call(
        _qproj_kernel,
        grid=(ROWS // t,),
        in_specs=[pl.BlockSpec((t, Q_RANK), lambda i: (i, EV_COL_CQ // Q_RANK)),
                  _layer_spec(nw, layer), _layer_spec(wn, layer), _layer_spec(wr, layer),
                  pl.BlockSpec((3, t, LANES), lambda i: (0, i, 0))],
        out_specs=[pl.BlockSpec((t, MLA_HEADS * MLA_NOPE), lambda i: (i, 0)),
                   pl.BlockSpec((t, MLA_HEADS * ROPE_DIM), lambda i: (i, 0))],
        out_shape=[jax.ShapeDtypeStruct((ROWS, MLA_HEADS * MLA_NOPE), BF16),
                   jax.ShapeDtypeStruct((ROWS, MLA_HEADS * ROPE_DIM), BF16)],
        name="mla_q_proj",
    )(u, nw, wn, wr, tabs)


def _kvproj_kernel(ckv_ref, kr_ref, nw_ref, wk_ref, wv_ref, tab_ref, kn_ref, kro_ref, v_ref):
    xn = (_rms(ckv_ref[...]) * nw_ref[...]).astype(BF16)
    kn_ref[...] = _dot(xn, wk_ref[...]).astype(BF16)
    v_ref[...] = _dot(xn, wv_ref[...]).astype(BF16)
    kr2 = _rope(kr_ref[...], tab_ref[0], tab_ref[1], tab_ref[2])
    lane = lax.broadcasted_iota(jnp.int32, kr2.shape, 1)
    kr_lo = jnp.where(lane < ROPE_DIM, kr2, 0.0)
    kro_ref[:, 0:LANES] = kr_lo.astype(BF16)
    kro_ref[:, LANES:2 * LANES] = pltpu.roll(kr_lo, ROPE_DIM, 1).astype(BF16)


def _kv_proj(u, layer, nw, wk, wv, tabs):
    t = MM_ROWS
    return pl.pallas_call(
        _kvproj_kernel,
        grid=(ROWS // t,),
        in_specs=[pl.BlockSpec((t, KV_RANK), lambda i: (i, EV_COL_CKV // KV_RANK)),
                  pl.BlockSpec((t, LANES), lambda i: (i, EV_COL_KR // LANES)),
                  _layer_spec(nw, layer), _layer_spec(wk, layer), _layer_spec(wv, layer),
                  pl.BlockSpec((3, t, LANES), lambda i: (0, i, 0))],
        out_specs=[pl.BlockSpec((t, MLA_HEADS * MLA_NOPE), lambda i: (i, 0)),
                   pl.BlockSpec((t, 2 * LANES), lambda i: (i, 0)),
                   pl.BlockSpec((t, MLA_W), lambda i: (i, 0))],
        out_shape=[jax.ShapeDtypeStruct((ROWS, MLA_HEADS * MLA_NOPE), BF16),
                   jax.ShapeDtypeStruct((ROWS, 2 * LANES), BF16),
                   jax.ShapeDtypeStruct((ROWS, MLA_W), BF16)],
        name="mla_kv_proj",
    )(u, u, nw, wk, wv, tabs)


FL_TQ = 768
FL_TK = 768
FL_LAST = ROWS // FL_TQ - 1
FL_LAT_IN_LAST = SEQ - FL_LAST * FL_TQ


def _key_chunks():
    return [(i * FL_TK, FL_TK) for i in range(ROWS // FL_TK)]


def _attend(q, kn_ref, kr_ref, v_ref, sa_ref, sb_ref, chunks):
    rows = q.shape[0]
    bufs = (sa_ref, sb_ref)

    def scores_into(j):
        off, size = chunks[j]
        k = jnp.concatenate([kn_ref[off:off + size, :], kr_ref[off:off + size, :]], axis=1)
        s = _dot_nt(q, k)
        bufs[j % 2][0:rows, 0:size] = s
        return jnp.max(s, axis=1, keepdims=True)

    def ones_block(size):
        lane = lax.broadcasted_iota(jnp.int32, (size, HEAD_PAD - MLA_VDIM), 1)
        return jnp.where(lane == 0, 1.0, 0.0).astype(BF16)

    def update(j, m_chunk, m_prev, acc):
        off, size = chunks[j]
        m_new = jnp.maximum(m_prev, m_chunk)
        alpha = jnp.exp2(m_prev - m_new)
        p = jnp.exp2(bufs[j % 2][0:rows, 0:size] - m_new).astype(BF16)
        v_ext = jnp.concatenate([v_ref[off:off + size, :], ones_block(size)], axis=1)
        return m_new, alpha * acc + _dot(p, v_ext)

    m = jnp.full((rows, 1), NEG_BIG, F32)
    acc = jnp.zeros((rows, HEAD_PAD), F32)
    m_next = scores_into(0)
    for j in range(len(chunks)):
        m_chunk = m_next
        if j + 1 < len(chunks):
            m_next = scores_into(j + 1)
        m, acc = update(j, m_chunk, m, acc)
    return acc[:, :MLA_VDIM] / acc[:, MLA_VDIM:MLA_VDIM + 1]


def _flash_kernel(qn_ref, qr_ref, kn_ref, kr_ref, v_ref, g_ref, o_ref, sa_ref, sb_ref):
    i = pl.program_id(1)
    chunks = _key_chunks()
    refs = (kn_ref, kr_ref, v_ref, sa_ref, sb_ref)

    def queries(lo, hi):
        return jnp.concatenate([qn_ref[lo:hi, :], qr_ref[lo:hi, :]], axis=1)

    @pl.when(i < FL_LAST)
    def _():
        o = _attend(queries(0, FL_TQ), *refs, chunks)
        o_ref[...] = (o * _silu(g_ref[...])).astype(BF16)

    @pl.when(i == FL_LAST)
    def _():
        nl = FL_LAT_IN_LAST
        o = _attend(queries(0, nl), *refs, chunks)
        o_ref[0:nl, :] = (o * _silu(g_ref[0:nl, :])).astype(BF16)
        oc = _attend(queries(nl, FL_TQ), *refs, [(SEQ, CTX_LEN)])
        o_ref[nl:, :] = (oc * _silu(g_ref[nl:, :])).astype(BF16)


def _flash(qn, qr, kn, kr, v, u):
    return pl.pallas_call(
        _flash_kernel,
        grid=(MLA_HEADS, ROWS // FL_TQ),
        in_specs=[pl.BlockSpec((FL_TQ, MLA_NOPE), lambda h, i: (i, h)),
                  pl.BlockSpec((FL_TQ, LANES), lambda h, i: (i, h // 2)),
                  pl.BlockSpec((ROWS, MLA_NOPE), lambda h, i: (0, h)),
                  pl.BlockSpec((ROWS, LANES), lambda h, i: (0, h % 2)),
                  pl.BlockSpec((ROWS, MLA_VDIM), lambda h, i: (0, h)),
                  pl.BlockSpec((FL_TQ, MLA_VDIM), lambda h, i: (i, EV_COL_AGATE // MLA_VDIM + h))],
        out_specs=pl.BlockSpec((FL_TQ, MLA_VDIM), lambda h, i: (i, h)),
        out_shape=jax.ShapeDtypeStruct((ROWS, MLA_W), BF16),
        scratch_shapes=[pltpu.VMEM((FL_TQ, FL_TK), F32), pltpu.VMEM((FL_TQ, FL_TK), F32)],
        name="mla_attention",
    )(qn, qr, kn, kr, v, u)


def _pool_kernel(prev_ref, cur_ref, next_ref, gate_ref, w_ref, sc_ref, o_ref):
    i = pl.program_id(0)
    first = jnp.logical_or(i == 0, i == N_LAT_BLK)
    last = i >= N_LAT_BLK - 1
    lo_lim = jnp.where(first, 0, -POOL_HALO)
    hi_lim = jnp.where(last, ROW_BLK - 1, ROW_BLK - 1 + POOL_HALO)
    src_rows = ROW_BLK + 2 * POOL_HALO
    t = lax.broadcasted_iota(jnp.int32, (ROW_BLK, src_rows), 0)
    j = lax.broadcasted_iota(jnp.int32, (ROW_BLK, src_rows), 1) - POOL_HALO
    t1 = lax.broadcasted_iota(jnp.int32, (ROW_BLK, 1), 0)
    for g, w in enumerate(POOL_WINDOWS):
        sl = slice(g * POOL_GC, (g + 1) * POOL_GC)
        cur = cur_ref[:, sl]
        src = jnp.concatenate([prev_ref[ROW_BLK - POOL_HALO:, sl], cur, next_ref[:POOL_HALO, sl]], axis=0)
        lo = jnp.maximum(t - w // 2, lo_lim)
        hi = jnp.minimum(t + w // 2 - 1, hi_lim)
        band = jnp.where(jnp.logical_and(j >= lo, j <= hi), 1.0, 0.0).astype(BF16)
        cnt = (jnp.minimum(t1 + w // 2 - 1, hi_lim) - jnp.maximum(t1 - w // 2, lo_lim) + 1).astype(F32)
        s_hi, s_mid, s_lo = _split3(src)
        wsum = _dot(band, s_hi) + _dot(band, s_mid) + _dot(band, s_lo)
        z = (wsum / cnt - cur).astype(BF16)
        y = _dot(z, w_ref[g]) * sc_ref[:, sl]
        o_ref[:, sl] = (y * _silu(gate_ref[:, sl])).astype(BF16)


def _pool(u, w_pool, pool_scale):
    def prev_map(i):
        return (jnp.where(i < N_LAT_BLK, jnp.maximum(i - 1, 0), N_LAT_BLK), 0)

    def next_map(i):
        return (jnp.where(i < N_LAT_BLK, jnp.minimum(i + 1, N_LAT_BLK - 1), N_LAT_BLK), 0)

    return pl.pallas_call(
        _pool_kernel,
        grid=(N_BLK,),
        in_specs=[pl.BlockSpec((ROW_BLK, POOL_W), prev_map),
                  pl.BlockSpec((ROW_BLK, POOL_W), lambda i: (i, 0)),
                  pl.BlockSpec((ROW_BLK, POOL_W), next_map),
                  pl.BlockSpec((ROW_BLK, POOL_W), lambda i: (i, 1)),
                  pl.BlockSpec((4, POOL_GC, POOL_GC), lambda i: (0, 0, 0)),
                  pl.BlockSpec((1, POOL_W), lambda i: (0, 0))],
        out_specs=pl.BlockSpec((ROW_BLK, POOL_W), lambda i: (i, 0)),
        out_shape=jax.ShapeDtypeStruct((ROWS, POOL_W), BF16),
        name="pool_mix",
    )(u, u, u, u, w_pool.astype(BF16), pool_scale.reshape(1, POOL_W))


def _outproj_residual(a_ref, b_ref, w_ref, x_ref, c_ref, mod_ref, wbf_ref):
    i = pl.program_id(0)

    @pl.when(i == 0)
    def _():
        wbf_ref[...] = w_ref[...].astype(BF16)

    k1 = a_ref.shape[1]
    y = _dot(a_ref[...], wbf_ref[0:k1, :]) + _dot(b_ref[...], wbf_ref[k1:, :])
    return _split_rows(x_ref, c_ref, i) + mod_ref[0, 2:3, :] * y


def _outproj_kernel(a_ref, b_ref, w_ref, x_ref, c_ref, mod_ref, nw_ref, modn_ref, xo_ref, h_ref, wbf_ref):
    xn = _outproj_residual(a_ref, b_ref, w_ref, x_ref, c_ref, mod_ref, wbf_ref)
    xo_ref[...] = xn
    h_ref[...] = _modnorm(xn, nw_ref[...], modn_ref[0]).astype(BF16)


def _outproj_final_kernel(a_ref, b_ref, w_ref, x_ref, c_ref, mod_ref, nw_ref, o_ref, wbf_ref):
    xn = _outproj_residual(a_ref, b_ref, w_ref, x_ref, c_ref, mod_ref, wbf_ref)
    o_ref[...] = _rms(xn) * nw_ref[...]


def _out_proj(a, b, w, layer, x, x_ctx, mod, nw_next, mod_next):
    k1, k2 = a.shape[1], b.shape[1]
    n_blk = N_LAT_BLK if mod_next is None else N_BLK
    row = lambda i: (i, 0)
    full = lambda i: (0, 0)
    stream = lambda i: (_is_ctx_blk(i), 0, 0)
    ctx_blk = x_ctx.shape[0] // CTX_LEN - 1
    common = [pl.BlockSpec((ROW_BLK, k1), row), pl.BlockSpec((ROW_BLK, k2), row),
              pl.BlockSpec((None, D_MODEL, D_MODEL), lambda i: (layer, 0, 0), pipeline_mode=pl.Buffered(1)),
              pl.BlockSpec((ROW_BLK, D_MODEL), lambda i: (jnp.minimum(i, N_LAT_BLK - 1), 0)),
              pl.BlockSpec((CTX_LEN, D_MODEL), lambda i: (ctx_blk, 0)),
              pl.BlockSpec((1, 8, D_MODEL), stream),
              pl.BlockSpec((1, D_MODEL), full)]
    scratch = [pltpu.VMEM((D_MODEL, D_MODEL), BF16)]
    params = pltpu.CompilerParams(dimension_semantics=("arbitrary",))
    if mod_next is None:
        return pl.pallas_call(
            _outproj_final_kernel,
            grid=(n_blk,),
            in_specs=common,
            out_specs=pl.BlockSpec((ROW_BLK, D_MODEL), row),
            out_shape=jax.ShapeDtypeStruct((SEQ, D_MODEL), F32),
            scratch_shapes=scratch,
            compiler_params=params,
            name="out_proj_final",
        )(a, b, w, x, x_ctx, mod, nw_next.reshape(1, D_MODEL))
    return pl.pallas_call(
        _outproj_kernel,
        grid=(n_blk,),
        in_specs=common + [pl.BlockSpec((1, 8, D_MODEL), stream)],
        out_specs=[pl.BlockSpec((ROW_BLK, D_MODEL), row), pl.BlockSpec((ROW_BLK, D_MODEL), row)],
        out_shape=[jax.ShapeDtypeStruct((ROWS, D_MODEL), F32), jax.ShapeDtypeStruct((ROWS, D_MODEL), BF16)],
        scratch_shapes=scratch,
        compiler_params=params,
        name="out_proj",
    )(a, b, w, x, x_ctx, mod, nw_next.reshape(1, D_MODEL), mod_next)


M_QSCALE = M_DQK ** -0.5


def _log_sigmoid(x):
    return jnp.minimum(x, 0.0) - jnp.log(1.0 + jnp.exp(-jnp.abs(x)))


def _mlstm_products(q_ref, k_ref, c_ref, n_ref, qk_ref, cq_ref, nq_ref, reverse):
    d = 1 if reverse else 0
    for h in range(M_HEADS):
        st = M_HEADS * d + h
        q = (q_ref[:, h * M_DQK:(h + 1) * M_DQK] * M_QSCALE).astype(BF16)
        qk_ref[st] = _dot_nt(k_ref[:, h * M_DQK:(h + 1) * M_DQK].astype(BF16), q)
        cq_ref[st] = _dot_nt(c_ref[st].astype(BF16), q)
        nq_ref[st] = _dot_nt(jnp.broadcast_to(n_ref[st:st + 1, :], (8, M_DQK)).astype(BF16), q)


def _mlstm_dir(k_ref, v_ref, g_ref, gb_ref, ht_ref, c_ref, n_ref, m_ref, qk_ref, cq_ref, nq_ref, reverse):
    T = M_CHUNK
    d = 1 if reverse else 0
    base = 2 * M_HEADS * d
    g = g_ref[...] + gb_ref[...]
    glf = _log_sigmoid(g)
    row = lax.broadcasted_iota(jnp.int32, (T, T), 0)
    col = lax.broadcasted_iota(jnp.int32, (T, T), 1)
    tri = jnp.where((col >= row) if reverse else (col <= row), 1.0, 0.0).astype(BF16)
    reach = (row >= col) if reverse else (row <= col)
    p_hi, p_mid, p_lo = _split3(glf)
    bcum = _dot(tri, p_hi) + _dot(tri, p_mid) + _dot(tri, p_lo)
    g_sh = pltpu.roll(g, M_HEADS, 1)
    r_all = g_sh - bcum
    end = 0 if reverse else T - 1
    b_last = bcum[end:end + 1, :]
    a_all = b_last - bcum + g_sh
    m_prev_row = m_ref[d:d + 1, :]
    m_new_row = jnp.maximum(b_last + m_prev_row, jnp.max(a_all, axis=0, keepdims=True))
    decay_row = jnp.exp(b_last + m_prev_row - m_new_row)
    w_all = jnp.exp(a_all - m_new_row)
    m_ref[d:d + 1, :] = m_new_row
    b_t = bcum.T
    for h in range(M_HEADS):
        cf = base + M_HEADS + h
        st = M_HEADS * d + h
        m_prev = m_prev_row[:, cf:cf + 1]
        decay = decay_row[:, cf:cf + 1]
        r_mat = jnp.where(reach, r_all[:, cf:cf + 1], -jnp.inf)
        mx = jnp.maximum(m_prev, jnp.max(r_mat, axis=0, keepdims=True))
        iw = jnp.exp(m_prev - mx)
        dm = jnp.exp(r_mat - mx)

        v = v_ref[:, h * M_DV:(h + 1) * M_DV].astype(BF16)
        s = qk_ref[st] * dm
        den = iw * nq_ref[st, 0:1, :] + jnp.sum(s, axis=0, keepdims=True)
        inv = 1.0 / jnp.maximum(jnp.abs(den), jnp.exp(-(b_t[cf:cf + 1, :] + mx)))
        ht_ref[h * M_DV:(h + 1) * M_DV, :] = _dot_tn(v, (s * inv).astype(BF16)) + cq_ref[st] * (iw * inv)

        wk = w_all[:, cf:cf + 1] * k_ref[:, h * M_DQK:(h + 1) * M_DQK]
        c_ref[st] = decay * c_ref[st] + _dot_tn(v, wk.astype(BF16))
        n_ref[st:st + 1, :] = decay * n_ref[st:st + 1, :] + jnp.sum(wk, axis=0, keepdims=True)


def _mlstm_kernel(qf_ref, kf_ref, vf_ref, gf_ref, qb_ref, kb_ref, vb_ref, gb_ref, bias_ref,
                  hf_ref, hb_ref, c_ref, n_ref, m_ref, qk_ref, cq_ref, nq_ref):
    @pl.when(pl.program_id(0) == 0)
    def _():
        c_ref[...] = jnp.zeros_like(c_ref)
        n_ref[...] = jnp.zeros_like(n_ref)
        m_ref[...] = jnp.zeros_like(m_ref)

    products = (qk_ref, cq_ref, nq_ref)
    _mlstm_products(qf_ref, kf_ref, c_ref, n_ref, *products, reverse=False)
    _mlstm_products(qb_ref, kb_ref, c_ref, n_ref, *products, reverse=True)
    _mlstm_dir(kf_ref, vf_ref, gf_ref, bias_ref, hf_ref, c_ref, n_ref, m_ref, *products, reverse=False)
    _mlstm_dir(kb_ref, vb_ref, gb_ref, bias_ref, hb_ref, c_ref, n_ref, m_ref, *products, reverse=True)


def _mlstm(u, gate_b):
    fwd = lambda k: jnp.where(k == 0, N_LAT_BLK, k - 1)
    bwd = lambda k: jnp.where(k == 0, N_LAT_BLK, N_LAT_BLK - k)
    mg_blk = OD_COL_MG // LANES

    def specs(rmap):
        return [pl.BlockSpec((M_CHUNK, M_QK_W), lambda k: (rmap(k), 0)),
                pl.BlockSpec((M_CHUNK, M_QK_W), lambda k: (rmap(k), 1)),
                pl.BlockSpec((M_CHUNK, M_V_W), lambda k: (rmap(k), 1)),
                pl.BlockSpec((M_CHUNK, LANES), lambda k: (rmap(k), mg_blk))]

    bias = jnp.concatenate([gate_b.astype(F32), jnp.zeros((LANES - 4 * M_HEADS,), F32)]).reshape(1, LANES)
    return pl.pallas_call(
        _mlstm_kernel,
        grid=(N_BLK,),
        in_specs=specs(fwd) + specs(bwd) + [pl.BlockSpec((1, LANES), lambda k: (0, 0))],
        out_specs=[pl.BlockSpec((M_V_W, M_CHUNK), lambda k: (0, fwd(k))),
                   pl.BlockSpec((M_V_W, M_CHUNK), lambda k: (0, bwd(k)))],
        out_shape=[jax.ShapeDtypeStruct((M_V_W, ROWS), F32), jax.ShapeDtypeStruct((M_V_W, ROWS), F32)],
        scratch_shapes=[pltpu.VMEM((2 * M_HEADS, M_DV, M_DQK), F32),
                        pltpu.VMEM((2 * M_HEADS, M_DQK), F32),
                        pltpu.VMEM((8, LANES), F32),
                        pltpu.VMEM((2 * M_HEADS, M_CHUNK, M_CHUNK), F32),
                        pltpu.VMEM((2 * M_HEADS, M_DV, M_CHUNK), F32),
                        pltpu.VMEM((2 * M_HEADS, 8, M_CHUNK), F32)],
        compiler_params=pltpu.CompilerParams(dimension_semantics=("arbitrary",)),
        name="mlstm_scan",
    )(u, u, u, u, u, u, u, u, bias)


def _mlstm_post_kernel(hf_ref, hb_ref, o_ref, z_ref, nw_ref, y_ref):
    for h in range(M_HEADS):
        sl = slice(h * M_DV, (h + 1) * M_DV)
        ht = hf_ref[sl, :] + hb_ref[sl, :]
        hn = (ht * lax.rsqrt(jnp.mean(ht * ht, axis=0, keepdims=True) + EPS)).T * nw_ref[:, sl]
        y_ref[:, sl] = (hn * _sigmoid(o_ref[:, sl]) * _silu(z_ref[:, sl])).astype(BF16)


def _mlstm_post(hf, hb, u, head_norm):
    row = lambda i: (i, 0)
    col = lambda i: (0, i)
    t = MM_ROWS
    return pl.pallas_call(
        _mlstm_post_kernel,
        grid=(ROWS // t,),
        in_specs=[pl.BlockSpec((M_V_W, t), col), pl.BlockSpec((M_V_W, t), col),
                  pl.BlockSpec((t, M_V_W), lambda i: (i, OD_COL_MO // M_V_W)),
                  pl.BlockSpec((t, M_V_W), lambda i: (i, OD_COL_MZ // M_V_W)),
                  pl.BlockSpec((1, M_V_W), lambda i: (0, 0))],
        out_specs=pl.BlockSpec((t, M_V_W), row),
        out_shape=jax.ShapeDtypeStruct((ROWS, M_V_W), BF16),
        name="mlstm_post",
    )(hf, hb, u, u, head_norm.reshape(1, M_V_W))


W_NLAT = SEQ // WINDOW
W_NBLK = ROWS // WINDOW
W_QSCALE = (A_HD ** -0.5) * LOG2E
W_KEYS = 3 * WINDOW + CTX_LEN
W_PAIRS = A_HEADS // A_KV_HEADS // 2


def _win_kernel(sink_ref, q_ref, az0_ref, az1_ref, az2_ref, az3_ref, kvp_ref, kvc_ref, kvn_ref, kvx_ref,
                tq_ref, tp_ref, tn_ref, o_ref, s_ref):
    az_refs = (az0_ref, az1_ref, az2_ref, az3_ref)
    bi = pl.program_id(0)
    is_lat = bi < W_NLAT
    tq = (tq_ref[0], tq_ref[1], tq_ref[2])
    lane_k = lax.broadcasted_iota(jnp.int32, (W_KEYS, LANES), 1)

    kj = lax.broadcasted_iota(jnp.int32, (WINDOW, WINDOW), 0)
    qi = lax.broadcasted_iota(jnp.int32, (WINDOW, WINDOW), 1)
    ok_prev = jnp.logical_and(jnp.logical_and(is_lat, bi >= 1), kj >= qi)
    ok_cur = jnp.logical_and(is_lat, kj >= 0)
    ok_next = jnp.logical_and(jnp.logical_and(is_lat, bi <= W_NLAT - 2), kj <= qi)
    ok_ctx = lax.broadcasted_iota(jnp.int32, (CTX_LEN, WINDOW), 1) >= 0
    valid = jnp.concatenate([ok_prev, ok_cur, ok_next, ok_ctx], axis=0)
    valid4 = jnp.concatenate([valid] * W_PAIRS, axis=1)
    lane4 = lax.broadcasted_iota(jnp.int32, (1, W_PAIRS * WINDOW), 1)

    kk = jnp.concatenate([_rope(kvp_ref[:, 0:LANES], tp_ref[0], tp_ref[1], tp_ref[2]),
                          _rope(kvc_ref[:, 0:LANES], *tq),
                          _rope(kvn_ref[:, 0:LANES], tn_ref[0], tn_ref[1], tn_ref[2]),
                          kvx_ref[:, 0:LANES]], axis=0)
    vv = jnp.concatenate([kvp_ref[:, LANES:2 * LANES], kvc_ref[:, LANES:2 * LANES],
                          kvn_ref[:, LANES:2 * LANES], kvx_ref[:, LANES:2 * LANES]], axis=0)
    kk_sw = pltpu.roll(kk, A_HD, 1)
    vv_sw = pltpu.roll(vv, A_HD, 1)
    low = lane_k < A_HD

    v_halves = []
    maxima = []
    for g in range(A_KV_HEADS):
        k_src = (kk, kk_sw) if g == 0 else (kk_sw, kk)
        v_src = (vv, vv_sw) if g == 0 else (vv_sw, vv)
        k_half = [jnp.where(low, k_src[0], 0.0).astype(BF16), jnp.where(low, 0.0, k_src[1]).astype(BF16)]
        v_halves.append([jnp.where(low, v_src[0], 0.0).astype(BF16), jnp.where(low, 0.0, v_src[1]).astype(BF16)])
        cols = [(g * W_PAIRS + p) * LANES for p in range(W_PAIRS)]
        qs = jnp.concatenate([(_rope(q_ref[:, c:c + LANES], *tq) * W_QSCALE).astype(BF16) for c in cols],
                             axis=0)
        for par in range(2):
            s = jnp.where(valid4, _dot_nt(k_half[par], qs), -jnp.inf)
            s_ref[2 * g + par] = s
            maxima.append(jnp.max(s, axis=0, keepdims=True))

    for g in range(A_KV_HEADS):
        cols = [(g * W_PAIRS + p) * LANES for p in range(W_PAIRS)]
        out_t = jnp.zeros((LANES, W_PAIRS * WINDOW), F32)
        for par in range(2):
            snk = jnp.full((1, W_PAIRS * WINDOW), sink_ref[g * 8 + par] * LOG2E, F32)
            for p in range(1, W_PAIRS):
                snk = jnp.where(lane4 >= p * WINDOW, sink_ref[g * 8 + 2 * p + par] * LOG2E, snk)
            m = jnp.maximum(maxima[2 * g + par], snk)
            e = jnp.exp2(s_ref[2 * g + par] - m)
            den = jnp.sum(e, axis=0, keepdims=True) + jnp.exp2(snk - m)
            out_t = out_t + _dot_tn(v_halves[g][par], e.astype(BF16)) * (1.0 / den)
        out = out_t.T
        for p, c in enumerate(cols):
            grp = c // LANES
            az = az_refs[grp // 2][:, (grp % 2) * LANES:(grp % 2 + 1) * LANES]
            o_ref[:, c:c + LANES] = (out[p * WINDOW:(p + 1) * WINDOW] * _silu(az)).astype(BF16)


def _win_attn(u, sink, tabs):
    kvb = OD_COL_KV // OD_KV_W
    cur = lambda b: jnp.minimum(b, W_NLAT - 1)
    prv = lambda b: jnp.clip(b - 1, 0, W_NLAT - 1)
    nxt = lambda b: jnp.clip(b + 1, 0, W_NLAT - 1)
    tab = lambda f: pl.BlockSpec((3, WINDOW, LANES), lambda b: (0, f(b), 0))
    az_spec = lambda n: pl.BlockSpec((WINDOW, 2 * LANES), lambda b: (b, OD_COL_AZ // (2 * LANES) + n))
    return pl.pallas_call(
        _win_kernel,
        grid=(W_NBLK,),
        in_specs=[pl.BlockSpec(memory_space=pltpu.SMEM),
                  pl.BlockSpec((WINDOW, A_W), lambda b: (b, OD_COL_AQ // A_W)),
                  az_spec(0), az_spec(1), az_spec(2), az_spec(3),
                  pl.BlockSpec((WINDOW, OD_KV_W), lambda b: (prv(b), kvb)),
                  pl.BlockSpec((WINDOW, OD_KV_W), lambda b: (cur(b), kvb)),
                  pl.BlockSpec((WINDOW, OD_KV_W), lambda b: (nxt(b), kvb)),
                  pl.BlockSpec((CTX_LEN, OD_KV_W), lambda b: (SEQ // CTX_LEN, kvb)),
                  tab(lambda b: b), tab(prv), tab(nxt)],
        out_specs=pl.BlockSpec((WINDOW, A_W), lambda b: (b, 0)),
        out_shape=jax.ShapeDtypeStruct((ROWS, A_W), BF16),
        scratch_shapes=[pltpu.VMEM((2 * A_KV_HEADS, W_KEYS, W_PAIRS * WINDOW), F32)],
        name="window_attention",
    )(sink, u, u, u, u, u, u, u, u, u, tabs, tabs, tabs)


def kernel(x, c, ctx, c_ctx, ada_w, ada_b, norm_w, ev_w_in, ev_q_norm, ev_kv_norm, ev_w_uq, ev_w_ukv,
           ev_w_pool, ev_pool_scale, ev_w_out, od_w_in, od_gate_b, od_head_norm, od_sink, od_w_out, final_norm):
    tabs = _rope_tables()
    mod = _ada(c, c_ctx, ada_w, ada_b)

    n_ev = ev_w_in.shape[0]
    ev_w = _ev_layout(ev_w_in)
    od_w = _od_layout(od_w_in)
    wq = ev_w_uq.astype(BF16).reshape(n_ev, Q_RANK, MLA_HEADS, MLA_QK)
    wq_nope = wq[..., :MLA_NOPE].reshape(n_ev, Q_RANK, MLA_HEADS * MLA_NOPE)
    wq_rope = wq[..., MLA_NOPE:].reshape(n_ev, Q_RANK, MLA_HEADS * ROPE_DIM)
    wkv = ev_w_ukv.astype(BF16).reshape(n_ev, KV_RANK, MLA_HEADS, MLA_NOPE + MLA_VDIM)
    wk_nope = wkv[..., :MLA_NOPE].reshape(n_ev, KV_RANK, MLA_HEADS * MLA_NOPE)
    wv = wkv[..., MLA_NOPE:].reshape(n_ev, KV_RANK, MLA_W)
    q_norm = ev_q_norm.reshape(n_ev, 1, Q_RANK)
    kv_norm = ev_kv_norm.reshape(n_ev, 1, KV_RANK)

    h = _prenorm(x[0], ctx[0], norm_w[0], mod[0])
    res_lat, res_ctx = x[0], ctx[0]
    out = None
    for i in range(DEPTH):
        j = i // 2
        if i % 2 == 0:
            u = _in_proj(h, ev_w, j, EV_TN)
            qn, qr = _q_proj(u, j, q_norm, wq_nope, wq_rope, tabs)
            kn, kr, v = _kv_proj(u, j, kv_norm, wk_nope, wv, tabs)
            a = _pool(u, ev_w_pool[j], ev_pool_scale[j])
            b = _flash(qn, qr, kn, kr, v, u)
            w_out = ev_w_out
        else:
            u = _in_proj(h, od_w, j, OD_TN)
            hf, hb = _mlstm(u, od_gate_b[j])
            a = _mlstm_post(hf, hb, u, od_head_norm[j])
            b = _win_attn(u, od_sink[j].astype(F32), tabs)
            w_out = od_w_out
        if i < DEPTH - 1:
            xs, h = _out_proj(a, b, w_out, j, res_lat, res_ctx, mod[i], norm_w[i + 1], mod[i + 1])
            res_lat = res_ctx = xs
        else:
            out = _out_proj(a, b, w_out, j, res_lat, res_ctx, mod[i], final_norm, None)
    return out[None]
```

```python
import functools
import math

import jax
import jax.numpy as jnp
from jax import lax
from jax.experimental import pallas as pl
from jax.experimental.pallas import tpu as pltpu

F32 = jnp.float32
BF16 = jnp.bfloat16

D_MODEL = 2048
SEQ = 8192
CTX_LEN = 256
ROWS = SEQ + CTX_LEN
DEPTH = 4
GRID_W = 64
EPS = 1e-6
ROPE_THETA = 10000.0
ROPE_DIM = 64
LOG2E = math.log2(math.e)

LANES = 128
ROW_BLK = 256
N_LAT_BLK = SEQ // ROW_BLK
N_BLK = ROWS // ROW_BLK
MM_ROWS = 768
IN_ROWS = 384

POOL_WINDOWS = (2, 4, 8, 16)
POOL_W = 512
POOL_GC = 128
POOL_HALO = 16

MLA_HEADS = 12
MLA_NOPE = 128
MLA_VDIM = 128
MLA_QK = MLA_NOPE + ROPE_DIM
MLA_W = MLA_HEADS * MLA_VDIM
Q_RANK = 512
KV_RANK = 512
HEAD_PAD = 256

M_HEADS = 4
M_DQK = 128
M_DV = 256
M_QK_W = 512
M_V_W = 1024
M_CHUNK = 256

A_HEADS = 16
A_KV_HEADS = 2
A_HD = 64
A_W = 1024
WINDOW = 128

NEG_BIG = -1e30

EV_N = 3840
EV_COL_CQ = 1024
EV_COL_CKV = 1536
EV_COL_AGATE = 2048
EV_COL_KR = 3584
EV_TN = 3840

OD_N = 6656
OD_COL_MO = 2048
OD_COL_MZ = 3072
OD_COL_AQ = 4096
OD_COL_KV = 5120
OD_KV_W = 256
OD_COL_AZ = 5376
OD_COL_MG = 6400
OD_TN = 3328

PREP_W = 256


def _prep_kernel(a_ref, b_ref, o_ref, *, n_copy, n_shift, shift):
    t = pl.program_id(1)

    @pl.when(t < n_copy)
    def _():
        o_ref[...] = a_ref[...].astype(BF16).T

    @pl.when(jnp.logical_and(t >= n_copy, t < n_copy + n_shift))
    def _():
        o_ref[...] = jnp.concatenate([a_ref[shift:, :], b_ref[...]], axis=0).astype(BF16).T

    @pl.when(t == n_copy + n_shift)
    def _():
        pad = jnp.zeros((PREP_W - shift, b_ref.shape[1]), F32)
        o_ref[...] = jnp.concatenate([b_ref[...], pad], axis=0).astype(BF16).T


def _prep_weight(w, n_copy, n_shift, shift):
    layers, d, n = w.shape
    n_tiles = n_copy + n_shift + 1
    per = PREP_W // shift
    a_map = lambda l, t: (l, jnp.minimum(t, n_copy + n_shift - 1), 0)
    b_map = lambda l, t: (l, jnp.where(t == n_tiles - 1, n_copy * per, jnp.minimum((t + 1) * per, n // shift - 1)), 0)
    return pl.pallas_call(
        functools.partial(_prep_kernel, n_copy=n_copy, n_shift=n_shift, shift=shift),
        grid=(layers, n_tiles),
        in_specs=[pl.BlockSpec((None, PREP_W, d), a_map),
                  pl.BlockSpec((None, shift, d), b_map)],
        out_specs=pl.BlockSpec((None, d, PREP_W), lambda l, t: (l, 0, t)),
        out_shape=jax.ShapeDtypeStruct((layers, d, n_tiles * PREP_W), BF16),
        name="weight_layout",
    )(jnp.swapaxes(w, 1, 2), jnp.swapaxes(w, 1, 2))


def _ev_layout(w):
    return _prep_weight(w, n_copy=8, n_shift=6, shift=ROPE_DIM)


def _od_layout(w):
    return _prep_weight(w, n_copy=12, n_shift=13, shift=4 * M_HEADS)


def _rope_tables():
    t = jnp.arange(SEQ)
    row = (t // GRID_W).astype(F32)
    col = (t % GRID_W).astype(F32)
    half = ROPE_DIM // 2
    inv = ROPE_THETA ** (-jnp.arange(0, half, 2, dtype=F32) / half)
    ar = row[:, None] * inv
    ac = col[:, None] * inv
    zero = jnp.zeros_like(ar)
    cos64 = jnp.concatenate([jnp.cos(ar), jnp.cos(ar), jnp.cos(ac), jnp.cos(ac)], axis=-1)
    sa64 = jnp.concatenate([-jnp.sin(ar), zero, -jnp.sin(ac), zero], axis=-1)
    sb64 = jnp.concatenate([zero, jnp.sin(ar), zero, jnp.sin(ac)], axis=-1)
    ctx_zero = jnp.zeros((CTX_LEN, LANES), F32)
    cos = jnp.concatenate([jnp.tile(cos64, (1, 2)), jnp.ones((CTX_LEN, LANES), F32)], axis=0)
    sin_a = jnp.concatenate([jnp.tile(sa64, (1, 2)), ctx_zero], axis=0)
    sin_b = jnp.concatenate([jnp.tile(sb64, (1, 2)), ctx_zero], axis=0)
    return jnp.stack([cos, sin_a, sin_b])


def _rope(x, cos, sin_a, sin_b):
    up = pltpu.roll(x, LANES - ROPE_DIM // 4, 1)
    down = pltpu.roll(x, ROPE_DIM // 4, 1)
    return x * cos + up * sin_a + down * sin_b


def _dot(a, b):
    return jnp.dot(a, b, preferred_element_type=F32)


def _dot_nt(a, b):
    return lax.dot_general(a, b, (((1,), (1,)), ((), ())), preferred_element_type=F32)


def _dot_tn(a, b):
    return lax.dot_general(a, b, (((0,), (0,)), ((), ())), preferred_element_type=F32)


def _rms(x):
    return x * lax.rsqrt(jnp.mean(x * x, axis=-1, keepdims=True) + EPS)


def _modnorm(x, nw, mod):
    return (_rms(x) * nw) * (1.0 + mod[1:2, :]) + mod[0:1, :]


def _silu(x):
    return x * (1.0 / (1.0 + jnp.exp(-x)))


def _sigmoid(x):
    return 1.0 / (1.0 + jnp.exp(-x))


def _split3(x):
    hi = x.astype(BF16)
    r1 = x - hi.astype(F32)
    mid = r1.astype(BF16)
    lo = (r1 - mid.astype(F32)).astype(BF16)
    return hi, mid, lo


def _is_ctx_blk(i):
    return jnp.where(i < N_LAT_BLK, 0, 1)


def _ada_kernel(s_ref, w_ref, b_ref, o_ref):
    s = _silu(s_ref[...])
    hi = s.astype(BF16)
    lo = (s - hi.astype(F32)).astype(BF16)
    lhs = jnp.concatenate([hi, lo], axis=0)
    r = _dot(lhs, w_ref[0].astype(BF16))
    o_ref[0] = r[0:8] + r[8:16] + b_ref[0]


def _ada(c, c_ctx, ada_w, ada_b):
    tn = 1536
    s = jnp.concatenate([c.reshape(1, D_MODEL), c_ctx.reshape(1, D_MODEL), jnp.zeros((6, D_MODEL), F32)], axis=0)
    out = pl.pallas_call(
        _ada_kernel,
        grid=(DEPTH, 3 * D_MODEL // tn),
        in_specs=[pl.BlockSpec((8, D_MODEL), lambda l, j: (0, 0)),
                  pl.BlockSpec((1, D_MODEL, tn), lambda l, j: (l, 0, j)),
                  pl.BlockSpec((1, 1, tn), lambda l, j: (l, 0, j))],
        out_specs=pl.BlockSpec((1, 8, tn), lambda l, j: (l, 0, j)),
        out_shape=jax.ShapeDtypeStruct((DEPTH, 8, 3 * D_MODEL), F32),
        name="ada_mod",
    )(s, ada_w, ada_b.reshape(DEPTH, 1, 3 * D_MODEL))
    m = out[:, :2, :].reshape(DEPTH, 2, 3, D_MODEL)
    return jnp.concatenate([m, jnp.zeros((DEPTH, 2, 5, D_MODEL), F32)], axis=2)


def _split_rows_specs():
    return [pl.BlockSpec((ROW_BLK, D_MODEL), lambda i: (jnp.minimum(i, N_LAT_BLK - 1), 0)),
            pl.BlockSpec((CTX_LEN, D_MODEL), lambda i: (0, 0))]


def _split_rows(x_ref, c_ref, blk):
    return jnp.where(blk < N_LAT_BLK, x_ref[...], c_ref[...])


def _mm_kernel(h_ref, w_ref, o_ref):
    o_ref[...] = _dot(h_ref[...], w_ref[...])


def _in_proj(h, w, layer, tn):
    n = w.shape[2]
    return pl.pallas_call(
        _mm_kernel,
        grid=(n // tn, ROWS // IN_ROWS),
        in_specs=[pl.BlockSpec((IN_ROWS, D_MODEL), lambda j, i: (i, 0)),
                  pl.BlockSpec((None, D_MODEL, tn), lambda j, i: (layer, 0, j))],
        out_specs=pl.BlockSpec((IN_ROWS, tn), lambda j, i: (i, j)),
        out_shape=jax.ShapeDtypeStruct((ROWS, n), F32),
        name="in_proj",
    )(h, w)


def _first_proj_kernel(x_ref, c_ref, nw_ref, mod_ref, w_ref, o_ref):
    x = _split_rows(x_ref, c_ref, pl.program_id(0))
    o_ref[...] = _dot(_modnorm(x, nw_ref[...], mod_ref[0]).astype(BF16), w_ref[...])


def _first_proj(x, ctx, nw, mod, w, layer):
    n = w.shape[2]
    return pl.pallas_call(
        _first_proj_kernel,
        grid=(N_BLK,),
        in_specs=_split_rows_specs() + [pl.BlockSpec((1, D_MODEL), lambda i: (0, 0)),
                                        pl.BlockSpec((1, 8, D_MODEL), lambda i: (_is_ctx_blk(i), 0, 0)),
                                        pl.BlockSpec((None, D_MODEL, n), lambda i: (layer, 0, 0),
                                                     pipeline_mode=pl.Buffered(1))],
        out_specs=pl.BlockSpec((ROW_BLK, n), lambda i: (i, 0)),
        out_shape=jax.ShapeDtypeStruct((ROWS, n), F32),
        name="first_proj",
    )(x, ctx, nw.reshape(1, D_MODEL), mod, w)


Q_SCALE = (MLA_QK ** -0.5) * LOG2E


def _qproj_kernel(cq_ref, nw_ref, wn_ref, wr_ref, tab_ref, qn_ref, qr_ref):
    xn = (_rms(cq_ref[...]) * nw_ref[...]).astype(BF16)
    qn_ref[...] = (_dot(xn, wn_ref[...]) * Q_SCALE).astype(BF16)
    qr = _dot(xn, wr_ref[...])
    for p in range(MLA_HEADS // 2):
        sl = slice(p * LANES, (p + 1) * LANES)
        qr_ref[:, sl] = (_rope(qr[:, sl], tab_ref[0], tab_ref[1], tab_ref[2]) * Q_SCALE).astype(BF16)


def _layer_spec(w, layer):
    return pl.BlockSpec((None,) + w.shape[1:], lambda i: (layer, 0, 0))


def _q_proj(u, layer, nw, wn, wr, tabs):
    t = MM_ROWS
    return pl.pallas_call(
        _qproj_kernel,
        grid=(ROWS // t,),
        in_specs=[pl.BlockSpec((t, Q_RANK), lambda i: (i, EV_COL_CQ // Q_RANK)),
                  _layer_spec(nw, layer), _layer_spec(wn, layer), _layer_spec(wr, layer),
                  pl.BlockSpec((3, t, LANES), lambda i: (0, i, 0))],
        out_specs=[pl.BlockSpec((t, MLA_HEADS * MLA_NOPE), lambda i: (i, 0)),
                   pl.BlockSpec((t, MLA_HEADS * ROPE_DIM), lambda i: (i, 0))],
        out_shape=[jax.ShapeDtypeStruct((ROWS, MLA_HEADS * MLA_NOPE), BF16),
                   jax.ShapeDtypeStruct((ROWS, MLA_HEADS * ROPE_DIM), BF16)],
        name="mla_q_proj",
    )(u, nw, wn, wr, tabs)


def _kvproj_kernel(ckv_ref, kr_ref, nw_ref, wk_ref, wv_ref, tab_ref, kn_ref, kro_ref, v_ref):
    xn = (_rms(ckv_ref[...]) * nw_ref[...]).astype(BF16)
    kn_ref[...] = _dot(xn, wk_ref[...]).astype(BF16)
    v_ref[...] = _dot(xn, wv_ref[...]).astype(BF16)
    kr2 = _rope(kr_ref[...], tab_ref[0], tab_ref[1], tab_ref[2])
    lane = lax.broadcasted_iota(jnp.int32, kr2.shape, 1)
    kr_lo = jnp.where(lane < ROPE_DIM, kr2, 0.0)
    kro_ref[:, 0:LANES] = kr_lo.astype(BF16)
    kro_ref[:, LANES:2 * LANES] = pltpu.roll(kr_lo, ROPE_DIM, 1).astype(BF16)


def _kv_proj(u, layer, nw, wk, wv, tabs):
    t = MM_ROWS
    return pl.pallas_call(
        _kvproj_kernel,
        grid=(ROWS // t,),
        in_specs=[pl.BlockSpec((t, KV_RANK), lambda i: (i, EV_COL_CKV // KV_RANK)),
                  pl.BlockSpec((t, LANES), lambda i: (i, EV_COL_KR // LANES)),
                  _layer_spec(nw, layer), _layer_spec(wk, layer), _layer_spec(wv, layer),
                  pl.BlockSpec((3, t, LANES), lambda i: (0, i, 0))],
        out_specs=[pl.BlockSpec((t, MLA_HEADS * MLA_NOPE), lambda i: (i, 0)),
                   pl.BlockSpec((t, 2 * LANES), lambda i: (i, 0)),
                   pl.BlockSpec((t, MLA_W), lambda i: (i, 0))],
        out_shape=[jax.ShapeDtypeStruct((ROWS, MLA_HEADS * MLA_NOPE), BF16),
                   jax.ShapeDtypeStruct((ROWS, 2 * LANES), BF16),
                   jax.ShapeDtypeStruct((ROWS, MLA_W), BF16)],
        name="mla_kv_proj",
    )(u, u, nw, wk, wv, tabs)


FL_TQ = 768
FL_TK = 768
FL_LAST = ROWS // FL_TQ - 1
FL_LAT_IN_LAST = SEQ - FL_LAST * FL_TQ


def _key_chunks():
    return [(i * FL_TK, FL_TK) for i in range(ROWS // FL_TK)]


def _attend(q, kn_ref, kr_ref, v_ref, sa_ref, sb_ref, chunks):
    rows = q.shape[0]
    bufs = (sa_ref, sb_ref)

    def scores_into(j):
        off, size = chunks[j]
        k = jnp.concatenate([kn_ref[off:off + size, :], kr_ref[off:off + size, :]], axis=1)
        s = _dot_nt(q, k)
        bufs[j % 2][0:rows, 0:size] = s
        return jnp.max(s, axis=1, keepdims=True)

    def ones_block(size):
        lane = lax.broadcasted_iota(jnp.int32, (size, HEAD_PAD - MLA_VDIM), 1)
        return jnp.where(lane == 0, 1.0, 0.0).astype(BF16)

    def update(j, m_chunk, m_prev, acc):
        off, size = chunks[j]
        m_new = jnp.maximum(m_prev, m_chunk)
        alpha = jnp.exp2(m_prev - m_new)
        p = jnp.exp2(bufs[j % 2][0:rows, 0:size] - m_new).astype(BF16)
        v_ext = jnp.concatenate([v_ref[off:off + size, :], ones_block(size)], axis=1)
        return m_new, alpha * acc + _dot(p, v_ext)

    m = jnp.full((rows, 1), NEG_BIG, F32)
    acc = jnp.zeros((rows, HEAD_PAD), F32)
    m_next = scores_into(0)
    for j in range(len(chunks)):
        m_chunk = m_next
        if j + 1 < len(chunks):
            m_next = scores_into(j + 1)
        m, acc = update(j, m_chunk, m, acc)
    return acc[:, :MLA_VDIM] / acc[:, MLA_VDIM:MLA_VDIM + 1]


def _flash_kernel(qn_ref, qr_ref, kn_ref, kr_ref, v_ref, g_ref, o_ref, sa_ref, sb_ref):
    i = pl.program_id(1)
    chunks = _key_chunks()
    refs = (kn_ref, kr_ref, v_ref, sa_ref, sb_ref)

    def queries(lo, hi):
        return jnp.concatenate([qn_ref[lo:hi, :], qr_ref[lo:hi, :]], axis=1)

    @pl.when(i < FL_LAST)
    def _():
        o = _attend(queries(0, FL_TQ), *refs, chunks)
        o_ref[...] = (o * _silu(g_ref[...])).astype(BF16)

    @pl.when(i == FL_LAST)
    def _():
        nl = FL_LAT_IN_LAST
        o = _attend(queries(0, nl), *refs, chunks)
        o_ref[0:nl, :] = (o * _silu(g_ref[0:nl, :])).astype(BF16)
        oc = _attend(queries(nl, FL_TQ), *refs, [(SEQ, CTX_LEN)])
        o_ref[nl:, :] = (oc * _silu(g_ref[nl:, :])).astype(BF16)


def _flash(qn, qr, kn, kr, v, u):
    return pl.pallas_call(
        _flash_kernel,
        grid=(MLA_HEADS, ROWS // FL_TQ),
        in_specs=[pl.BlockSpec((FL_TQ, MLA_NOPE), lambda h, i: (i, h)),
                  pl.BlockSpec((FL_TQ, LANES), lambda h, i: (i, h // 2)),
                  pl.BlockSpec((ROWS, MLA_NOPE), lambda h, i: (0, h)),
                  pl.BlockSpec((ROWS, LANES), lambda h, i: (0, h % 2)),
                  pl.BlockSpec((ROWS, MLA_VDIM), lambda h, i: (0, h)),
                  pl.BlockSpec((FL_TQ, MLA_VDIM), lambda h, i: (i, EV_COL_AGATE // MLA_VDIM + h))],
        out_specs=pl.BlockSpec((FL_TQ, MLA_VDIM), lambda h, i: (i, h)),
        out_shape=jax.ShapeDtypeStruct((ROWS, MLA_W), BF16),
        scratch_shapes=[pltpu.VMEM((FL_TQ, FL_TK), F32), pltpu.VMEM((FL_TQ, FL_TK), F32)],
        name="mla_attention",
    )(qn, qr, kn, kr, v, u)


def _pool_kernel(prev_ref, cur_ref, next_ref, gate_ref, w_ref, sc_ref, o_ref):
    i = pl.program_id(0)
    first = jnp.logical_or(i == 0, i == N_LAT_BLK)
    last = i >= N_LAT_BLK - 1
    lo_lim = jnp.where(first, 0, -POOL_HALO)
    hi_lim = jnp.where(last, ROW_BLK - 1, ROW_BLK - 1 + POOL_HALO)
    src_rows = ROW_BLK + 2 * POOL_HALO
    t = lax.broadcasted_iota(jnp.int32, (ROW_BLK, src_rows), 0)
    j = lax.broadcasted_iota(jnp.int32, (ROW_BLK, src_rows), 1) - POOL_HALO
    t1 = lax.broadcasted_iota(jnp.int32, (ROW_BLK, 1), 0)
    for g, w in enumerate(POOL_WINDOWS):
        sl = slice(g * POOL_GC, (g + 1) * POOL_GC)
        cur = cur_ref[:, sl]
        src = jnp.concatenate([prev_ref[ROW_BLK - POOL_HALO:, sl], cur, next_ref[:POOL_HALO, sl]], axis=0)
        lo = jnp.maximum(t - w // 2, lo_lim)
        hi = jnp.minimum(t + w // 2 - 1, hi_lim)
        band = jnp.where(jnp.logical_and(j >= lo, j <= hi), 1.0, 0.0).astype(BF16)
        cnt = (jnp.minimum(t1 + w // 2 - 1, hi_lim) - jnp.maximum(t1 - w // 2, lo_lim) + 1).astype(F32)
        s_hi, s_mid, s_lo = _split3(src)
        wsum = _dot(band, s_hi) + _dot(band, s_mid) + _dot(band, s_lo)
        z = (wsum / cnt - cur).astype(BF16)
        y = _dot(z, w_ref[g]) * sc_ref[:, sl]
        o_ref[:, sl] = (y * _silu(gate_ref[:, sl])).astype(BF16)


def _pool(u, w_pool, pool_scale):
    def prev_map(i):
        return (jnp.where(i < N_LAT_BLK, jnp.maximum(i - 1, 0), N_LAT_BLK), 0)

    def next_map(i):
        return (jnp.where(i < N_LAT_BLK, jnp.minimum(i + 1, N_LAT_BLK - 1), N_LAT_BLK), 0)

    return pl.pallas_call(
        _pool_kernel,
        grid=(N_BLK,),
        in_specs=[pl.BlockSpec((ROW_BLK, POOL_W), prev_map),
                  pl.BlockSpec((ROW_BLK, POOL_W), lambda i: (i, 0)),
                  pl.BlockSpec((ROW_BLK, POOL_W), next_map),
                  pl.BlockSpec((ROW_BLK, POOL_W), lambda i: (i, 1)),
                  pl.BlockSpec((4, POOL_GC, POOL_GC), lambda i: (0, 0, 0)),
                  pl.BlockSpec((1, POOL_W), lambda i: (0, 0))],
        out_specs=pl.BlockSpec((ROW_BLK, POOL_W), lambda i: (i, 0)),
        out_shape=jax.ShapeDtypeStruct((ROWS, POOL_W), BF16),
        name="pool_mix",
    )(u, u, u, u, w_pool.astype(BF16), pool_scale.reshape(1, POOL_W))


def _outproj_residual(a_ref, b_ref, w_ref, x_ref, c_ref, mod_ref, wbf_ref):
    i = pl.program_id(0)

    @pl.when(i == 0)
    def _():
        wbf_ref[...] = w_ref[...].astype(BF16)

    k1 = a_ref.shape[1]
    y = _dot(a_ref[...], wbf_ref[0:k1, :]) + _dot(b_ref[...], wbf_ref[k1:, :])
    return _split_rows(x_ref, c_ref, i) + mod_ref[0, 2:3, :] * y


def _outproj_kernel(a_ref, b_ref, w_ref, x_ref, c_ref, mod_ref, nw_ref, modn_ref, xo_ref, h_ref, wbf_ref):
    xn = _outproj_residual(a_ref, b_ref, w_ref, x_ref, c_ref, mod_ref, wbf_ref)
    xo_ref[...] = xn
    h_ref[...] = _modnorm(xn, nw_ref[...], modn_ref[0]).astype(BF16)


def _outproj_final_kernel(a_ref, b_ref, w_ref, x_ref, c_ref, mod_ref, nw_ref, o_ref, wbf_ref):
    xn = _outproj_residual(a_ref, b_ref, w_ref, x_ref, c_ref, mod_ref, wbf_ref)
    o_ref[...] = _rms(xn) * nw_ref[...]


def _out_proj(a, b, w, layer, x, x_ctx, mod, nw_next, mod_next):
    k1, k2 = a.shape[1], b.shape[1]
    n_blk = N_LAT_BLK if mod_next is None else N_BLK
    row = lambda i: (i, 0)
    full = lambda i: (0, 0)
    stream = lambda i: (_is_ctx_blk(i), 0, 0)
    ctx_blk = x_ctx.shape[0] // CTX_LEN - 1
    common = [pl.BlockSpec((ROW_BLK, k1), row), pl.BlockSpec((ROW_BLK, k2), row),
              pl.BlockSpec((None, D_MODEL, D_MODEL), lambda i: (layer, 0, 0), pipeline_mode=pl.Buffered(1)),
              pl.BlockSpec((ROW_BLK, D_MODEL), lambda i: (jnp.minimum(i, N_LAT_BLK - 1), 0)),
              pl.BlockSpec((CTX_LEN, D_MODEL), lambda i: (ctx_blk, 0)),
              pl.BlockSpec((1, 8, D_MODEL), stream),
              pl.BlockSpec((1, D_MODEL), full)]
    scratch = [pltpu.VMEM((D_MODEL, D_MODEL), BF16)]
    params = pltpu.CompilerParams(dimension_semantics=("arbitrary",))
    if mod_next is None:
        return pl.pallas_call(
            _outproj_final_kernel,
            grid=(n_blk,),
            in_specs=common,
            out_specs=pl.BlockSpec((ROW_BLK, D_MODEL), row),
            out_shape=jax.ShapeDtypeStruct((SEQ, D_MODEL), F32),
            scratch_shapes=scratch,
            compiler_params=params,
            name="out_proj_final",
        )(a, b, w, x, x_ctx, mod, nw_next.reshape(1, D_MODEL))
    return pl.pallas_call(
        _outproj_kernel,
        grid=(n_blk,),
        in_specs=common + [pl.BlockSpec((1, 8, D_MODEL), stream)],
        out_specs=[pl.BlockSpec((ROW_BLK, D_MODEL), row), pl.BlockSpec((ROW_BLK, D_MODEL), row)],
        out_shape=[jax.ShapeDtypeStruct((ROWS, D_MODEL), F32), jax.ShapeDtypeStruct((ROWS, D_MODEL), BF16)],
        scratch_shapes=scratch,
        compiler_params=params,
        name="out_proj",
    )(a, b, w, x, x_ctx, mod, nw_next.reshape(1, D_MODEL), mod_next)


M_QSCALE = M_DQK ** -0.5


def _log_sigmoid(x):
    return jnp.minimum(x, 0.0) - jnp.log(1.0 + jnp.exp(-jnp.abs(x)))


def _mlstm_products(q_ref, k_ref, c_ref, n_ref, qk_ref, cq_ref, nq_ref, reverse):
    d = 1 if reverse else 0
    for h in range(M_HEADS):
        st = M_HEADS * d + h
        q = (q_ref[:, h * M_DQK:(h + 1) * M_DQK] * M_QSCALE).astype(BF16)
        qk_ref[st] = _dot_nt(k_ref[:, h * M_DQK:(h + 1) * M_DQK].astype(BF16), q)
        cq_ref[st] = _dot_nt(c_ref[st].astype(BF16), q)
        nq_ref[st] = _dot_nt(jnp.broadcast_to(n_ref[st:st + 1, :], (8, M_DQK)).astype(BF16), q)


def _mlstm_dir(k_ref, v_ref, g_ref, gb_ref, ht_ref, c_ref, n_ref, m_ref, qk_ref, cq_ref, nq_ref, reverse):
    T = M_CHUNK
    d = 1 if reverse else 0
    base = 2 * M_HEADS * d
    g = g_ref[...] + gb_ref[...]
    glf = _log_sigmoid(g)
    row = lax.broadcasted_iota(jnp.int32, (T, T), 0)
    col = lax.broadcasted_iota(jnp.int32, (T, T), 1)
    tri = jnp.where((col >= row) if reverse else (col <= row), 1.0, 0.0).astype(BF16)
    reach = (row >= col) if reverse else (row <= col)
    p_hi, p_mid, p_lo = _split3(glf)
    bcum = _dot(tri, p_hi) + _dot(tri, p_mid) + _dot(tri, p_lo)
    g_sh = pltpu.roll(g, M_HEADS, 1)
    r_all = g_sh - bcum
    end = 0 if reverse else T - 1
    b_last = bcum[end:end + 1, :]
    a_all = b_last - bcum + g_sh
    m_prev_row = m_ref[d:d + 1, :]
    m_new_row = jnp.maximum(b_last + m_prev_row, jnp.max(a_all, axis=0, keepdims=True))
    decay_row = jnp.exp(b_last + m_prev_row - m_new_row)
    w_all = jnp.exp(a_all - m_new_row)
    m_ref[d:d + 1, :] = m_new_row
    b_t = bcum.T
    for h in range(M_HEADS):
        cf = base + M_HEADS + h
        st = M_HEADS * d + h
        m_prev = m_prev_row[:, cf:cf + 1]
        decay = decay_row[:, cf:cf + 1]
        r_mat = jnp.where(reach, r_all[:, cf:cf + 1], -jnp.inf)
        mx = jnp.maximum(m_prev, jnp.max(r_mat, axis=0, keepdims=True))
        iw = jnp.exp(m_prev - mx)
        dm = jnp.exp(r_mat - mx)

        v = v_ref[:, h * M_DV:(h + 1) * M_DV].astype(BF16)
        s = qk_ref[st] * dm
        den = iw * nq_ref[st, 0:1, :] + jnp.sum(s, axis=0, keepdims=True)
        inv = 1.0 / jnp.maximum(jnp.abs(den), jnp.exp(-(b_t[cf:cf + 1, :] + mx)))
        ht_ref[h * M_DV:(h + 1) * M_DV, :] = _dot_tn(v, (s * inv).astype(BF16)) + cq_ref[st] * (iw * inv)

        wk = w_all[:, cf:cf + 1] * k_ref[:, h * M_DQK:(h + 1) * M_DQK]
        c_ref[st] = decay * c_ref[st] + _dot_tn(v, wk.astype(BF16))
        n_ref[st:st + 1, :] = decay * n_ref[st:st + 1, :] + jnp.sum(wk, axis=0, keepdims=True)


def _mlstm_kernel(qf_ref, kf_ref, vf_ref, gf_ref, qb_ref, kb_ref, vb_ref, gb_ref, bias_ref,
                  hf_ref, hb_ref, c_ref, n_ref, m_ref, qk_ref, cq_ref, nq_ref):
    @pl.when(pl.program_id(0) == 0)
    def _():
        c_ref[...] = jnp.zeros_like(c_ref)
        n_ref[...] = jnp.zeros_like(n_ref)
        m_ref[...] = jnp.zeros_like(m_ref)

    products = (qk_ref, cq_ref, nq_ref)
    _mlstm_products(qf_ref, kf_ref, c_ref, n_ref, *products, reverse=False)
    _mlstm_products(qb_ref, kb_ref, c_ref, n_ref, *products, reverse=True)
    _mlstm_dir(kf_ref, vf_ref, gf_ref, bias_ref, hf_ref, c_ref, n_ref, m_ref, *products, reverse=False)
    _mlstm_dir(kb_ref, vb_ref, gb_ref, bias_ref, hb_ref, c_ref, n_ref, m_ref, *products, reverse=True)


def _mlstm(u, gate_b):
    fwd = lambda k: jnp.where(k == 0, N_LAT_BLK, k - 1)
    bwd = lambda k: jnp.where(k == 0, N_LAT_BLK, N_LAT_BLK - k)
    mg_blk = OD_COL_MG // LANES

    def specs(rmap):
        return [pl.BlockSpec((M_CHUNK, M_QK_W), lambda k: (rmap(k), 0)),
                pl.BlockSpec((M_CHUNK, M_QK_W), lambda k: (rmap(k), 1)),
                pl.BlockSpec((M_CHUNK, M_V_W), lambda k: (rmap(k), 1)),
                pl.BlockSpec((M_CHUNK, LANES), lambda k: (rmap(k), mg_blk))]

    bias = jnp.concatenate([gate_b.astype(F32), jnp.zeros((LANES - 4 * M_HEADS,), F32)]).reshape(1, LANES)
    return pl.pallas_call(
        _mlstm_kernel,
        grid=(N_BLK,),
        in_specs=specs(fwd) + specs(bwd) + [pl.BlockSpec((1, LANES), lambda k: (0, 0))],
        out_specs=[pl.BlockSpec((M_V_W, M_CHUNK), lambda k: (0, fwd(k))),
                   pl.BlockSpec((M_V_W, M_CHUNK), lambda k: (0, bwd(k)))],
        out_shape=[jax.ShapeDtypeStruct((M_V_W, ROWS), F32), jax.ShapeDtypeStruct((M_V_W, ROWS), F32)],
        scratch_shapes=[pltpu.VMEM((2 * M_HEADS, M_DV, M_DQK), F32),
                        pltpu.VMEM((2 * M_HEADS, M_DQK), F32),
                        pltpu.VMEM((8, LANES), F32),
                        pltpu.VMEM((2 * M_HEADS, M_CHUNK, M_CHUNK), F32),
                        pltpu.VMEM((2 * M_HEADS, M_DV, M_CHUNK), F32),
                        pltpu.VMEM((2 * M_HEADS, 8, M_CHUNK), F32)],
        compiler_params=pltpu.CompilerParams(dimension_semantics=("arbitrary",)),
        name="mlstm_scan",
    )(u, u, u, u, u, u, u, u, bias)


def _mlstm_post_kernel(hf_ref, hb_ref, o_ref, z_ref, nw_ref, y_ref):
    for h in range(M_HEADS):
        sl = slice(h * M_DV, (h + 1) * M_DV)
        ht = hf_ref[sl, :] + hb_ref[sl, :]
        hn = (ht * lax.rsqrt(jnp.mean(ht * ht, axis=0, keepdims=True) + EPS)).T * nw_ref[:, sl]
        y_ref[:, sl] = (hn * _sigmoid(o_ref[:, sl]) * _silu(z_ref[:, sl])).astype(BF16)


def _mlstm_post(hf, hb, u, head_norm):
    row = lambda i: (i, 0)
    col = lambda i: (0, i)
    t = MM_ROWS
    return pl.pallas_call(
        _mlstm_post_kernel,
        grid=(ROWS // t,),
        in_specs=[pl.BlockSpec((M_V_W, t), col), pl.BlockSpec((M_V_W, t), col),
                  pl.BlockSpec((t, M_V_W), lambda i: (i, OD_COL_MO // M_V_W)),
                  pl.BlockSpec((t, M_V_W), lambda i: (i, OD_COL_MZ // M_V_W)),
                  pl.BlockSpec((1, M_V_W), lambda i: (0, 0))],
        out_specs=pl.BlockSpec((t, M_V_W), row),
        out_shape=jax.ShapeDtypeStruct((ROWS, M_V_W), BF16),
        name="mlstm_post",
    )(hf, hb, u, u, head_norm.reshape(1, M_V_W))


W_NLAT = SEQ // WINDOW
W_NBLK = ROWS // WINDOW
W_QSCALE = (A_HD ** -0.5) * LOG2E
W_KEYS = 3 * WINDOW + CTX_LEN
W_PAIRS = A_HEADS // A_KV_HEADS // 2


def _win_kernel(sink_ref, q_ref, az0_ref, az1_ref, az2_ref, az3_ref, kvp_ref, kvc_ref, kvn_ref, kvx_ref,
                tq_ref, tp_ref, tn_ref, o_ref, s_ref):
    az_refs = (az0_ref, az1_ref, az2_ref, az3_ref)
    bi = pl.program_id(0)
    is_lat = bi < W_NLAT
    tq = (tq_ref[0], tq_ref[1], tq_ref[2])
    lane_k = lax.broadcasted_iota(jnp.int32, (W_KEYS, LANES), 1)

    kj = lax.broadcasted_iota(jnp.int32, (WINDOW, WINDOW), 0)
    qi = lax.broadcasted_iota(jnp.int32, (WINDOW, WINDOW), 1)
    ok_prev = jnp.logical_and(jnp.logical_and(is_lat, bi >= 1), kj >= qi)
    ok_cur = jnp.logical_and(is_lat, kj >= 0)
    ok_next = jnp.logical_and(jnp.logical_and(is_lat, bi <= W_NLAT - 2), kj <= qi)
    ok_ctx = lax.broadcasted_iota(jnp.int32, (CTX_LEN, WINDOW), 1) >= 0
    valid = jnp.concatenate([ok_prev, ok_cur, ok_next, ok_ctx], axis=0)
    valid4 = jnp.concatenate([valid] * W_PAIRS, axis=1)
    lane4 = lax.broadcasted_iota(jnp.int32, (1, W_PAIRS * WINDOW), 1)

    kk = jnp.concatenate([_rope(kvp_ref[:, 0:LANES], tp_ref[0], tp_ref[1], tp_ref[2]),
                          _rope(kvc_ref[:, 0:LANES], *tq),
                          _rope(kvn_ref[:, 0:LANES], tn_ref[0], tn_ref[1], tn_ref[2]),
                          kvx_ref[:, 0:LANES]], axis=0)
    vv = jnp.concatenate([kvp_ref[:, LANES:2 * LANES], kvc_ref[:, LANES:2 * LANES],
                          kvn_ref[:, LANES:2 * LANES], kvx_ref[:, LANES:2 * LANES]], axis=0)
    kk_sw = pltpu.roll(kk, A_HD, 1)
    vv_sw = pltpu.roll(vv, A_HD, 1)
    low = lane_k < A_HD

    v_halves = []
    maxima = []
    for g in range(A_KV_HEADS):
        k_src = (kk, kk_sw) if g == 0 else (kk_sw, kk)
        v_src = (vv, vv_sw) if g == 0 else (vv_sw, vv)
        k_half = [jnp.where(low, k_src[0], 0.0).astype(BF16), jnp.where(low, 0.0, k_src[1]).astype(BF16)]
        v_halves.append([jnp.where(low, v_src[0], 0.0).astype(BF16), jnp.where(low, 0.0, v_src[1]).astype(BF16)])
        cols = [(g * W_PAIRS + p) * LANES for p in range(W_PAIRS)]
        qs = jnp.concatenate([(_rope(q_ref[:, c:c + LANES], *tq) * W_QSCALE).astype(BF16) for c in cols],
                             axis=0)
        for par in range(2):
            s = jnp.where(valid4, _dot_nt(k_half[par], qs), -jnp.inf)
            s_ref[2 * g + par] = s
            maxima.append(jnp.max(s, axis=0, keepdims=True))

    for g in range(A_KV_HEADS):
        cols = [(g * W_PAIRS + p) * LANES for p in range(W_PAIRS)]
        out_t = jnp.zeros((LANES, W_PAIRS * WINDOW), F32)
        for par in range(2):
            snk = jnp.full((1, W_PAIRS * WINDOW), sink_ref[g * 8 + par] * LOG2E, F32)
            for p in range(1, W_PAIRS):
                snk = jnp.where(lane4 >= p * WINDOW, sink_ref[g * 8 + 2 * p + par] * LOG2E, snk)
            m = jnp.maximum(maxima[2 * g + par], snk)
            e = jnp.exp2(s_ref[2 * g + par] - m)
            den = jnp.sum(e, axis=0, keepdims=True) + jnp.exp2(snk - m)
            out_t = out_t + _dot_tn(v_halves[g][par], e.astype(BF16)) * (1.0 / den)
        out = out_t.T
        for p, c in enumerate(cols):
            grp = c // LANES
            az = az_refs[grp // 2][:, (grp % 2) * LANES:(grp % 2 + 1) * LANES]
            o_ref[:, c:c + LANES] = (out[p * WINDOW:(p + 1) * WINDOW] * _silu(az)).astype(BF16)


def _win_attn(u, sink, tabs):
    kvb = OD_COL_KV // OD_KV_W
    cur = lambda b: jnp.minimum(b, W_NLAT - 1)
    prv = lambda b: jnp.clip(b - 1, 0, W_NLAT - 1)
    nxt = lambda b: jnp.clip(b + 1, 0, W_NLAT - 1)
    tab = lambda f: pl.BlockSpec((3, WINDOW, LANES), lambda b: (0, f(b), 0))
    az_spec = lambda n: pl.BlockSpec((WINDOW, 2 * LANES), lambda b: (b, OD_COL_AZ // (2 * LANES) + n))
    return pl.pallas_call(
        _win_kernel,
        grid=(W_NBLK,),
        in_specs=[pl.BlockSpec(memory_space=pltpu.SMEM),
                  pl.BlockSpec((WINDOW, A_W), lambda b: (b, OD_COL_AQ // A_W)),
                  az_spec(0), az_spec(1), az_spec(2), az_spec(3),
                  pl.BlockSpec((WINDOW, OD_KV_W), lambda b: (prv(b), kvb)),
                  pl.BlockSpec((WINDOW, OD_KV_W), lambda b: (cur(b), kvb)),
                  pl.BlockSpec((WINDOW, OD_KV_W), lambda b: (nxt(b), kvb)),
                  pl.BlockSpec((CTX_LEN, OD_KV_W), lambda b: (SEQ // CTX_LEN, kvb)),
                  tab(lambda b: b), tab(prv), tab(nxt)],
        out_specs=pl.BlockSpec((WINDOW, A_W), lambda b: (b, 0)),
        out_shape=jax.ShapeDtypeStruct((ROWS, A_W), BF16),
        scratch_shapes=[pltpu.VMEM((2 * A_KV_HEADS, W_KEYS, W_PAIRS * WINDOW), F32)],
        name="window_attention",
    )(sink, u, u, u, u, u, u, u, u, u, tabs, tabs, tabs)


def kernel(x, c, ctx, c_ctx, ada_w, ada_b, norm_w, ev_w_in, ev_q_norm, ev_kv_norm, ev_w_uq, ev_w_ukv,
           ev_w_pool, ev_pool_scale, ev_w_out, od_w_in, od_gate_b, od_head_norm, od_sink, od_w_out, final_norm):
    tabs = _rope_tables()
    mod = _ada(c, c_ctx, ada_w, ada_b)

    n_ev = ev_w_in.shape[0]
    ev_w = _ev_layout(ev_w_in)
    od_w = _od_layout(od_w_in)
    wq = ev_w_uq.astype(BF16).reshape(n_ev, Q_RANK, MLA_HEADS, MLA_QK)
    wq_nope = wq[..., :MLA_NOPE].reshape(n_ev, Q_RANK, MLA_HEADS * MLA_NOPE)
    wq_rope = wq[..., MLA_NOPE:].reshape(n_ev, Q_RANK, MLA_HEADS * ROPE_DIM)
    wkv = ev_w_ukv.astype(BF16).reshape(n_ev, KV_RANK, MLA_HEADS, MLA_NOPE + MLA_VDIM)
    wk_nope = wkv[..., :MLA_NOPE].reshape(n_ev, KV_RANK, MLA_HEADS * MLA_NOPE)
    wv = wkv[..., MLA_NOPE:].reshape(n_ev, KV_RANK, MLA_W)
    q_norm = ev_q_norm.reshape(n_ev, 1, Q_RANK)
    kv_norm = ev_kv_norm.reshape(n_ev, 1, KV_RANK)

    h = None
    res_lat, res_ctx = x[0], ctx[0]
    out = None
    for i in range(DEPTH):
        j = i // 2
        if i % 2 == 0:
            if i == 0:
                u = _first_proj(x[0], ctx[0], norm_w[0], mod[0], ev_w, j)
            else:
                u = _in_proj(h, ev_w, j, EV_TN)
            qn, qr = _q_proj(u, j, q_norm, wq_nope, wq_rope, tabs)
            kn, kr, v = _kv_proj(u, j, kv_norm, wk_nope, wv, tabs)
            a = _pool(u, ev_w_pool[j], ev_pool_scale[j])
            b = _flash(qn, qr, kn, kr, v, u)
            w_out = ev_w_out
        else:
            u = _in_proj(h, od_w, j, OD_TN)
            hf, hb = _mlstm(u, od_gate_b[j])
            a = _mlstm_post(hf, hb, u, od_head_norm[j])
            b = _win_attn(u, od_sink[j].astype(F32), tabs)
            w_out = od_w_out
        if i < DEPTH - 1:
            xs, h = _out_proj(a, b, w_out, j, res_lat, res_ctx, mod[i], norm_w[i + 1], mod[i + 1])
            res_lat = res_ctx = xs
        else:
            out = _out_proj(a, b, w_out, j, res_lat, res_ctx, mod[i], final_norm, None)
    return out[None]
```

```python
import functools
import math

import jax
import jax.numpy as jnp
from jax import lax
from jax.experimental import pallas as pl
from jax.experimental.pallas import tpu as pltpu

F32 = jnp.float32
BF16 = jnp.bfloat16

D_MODEL = 2048
SEQ = 8192
CTX_LEN = 256
ROWS = SEQ + CTX_LEN
DEPTH = 4
GRID_W = 64
EPS = 1e-6
ROPE_THETA = 10000.0
ROPE_DIM = 64
LOG2E = math.log2(math.e)

LANES = 128
ROW_BLK = 256
N_LAT_BLK = SEQ // ROW_BLK
N_BLK = ROWS // ROW_BLK
MM_ROWS = 768
IN_ROWS = 384

POOL_WINDOWS = (2, 4, 8, 16)
POOL_W = 512
POOL_GC = 128
POOL_HALO = 16

MLA_HEADS = 12
MLA_NOPE = 128
MLA_VDIM = 128
MLA_QK = MLA_NOPE + ROPE_DIM
MLA_W = MLA_HEADS * MLA_VDIM
Q_RANK = 512
KV_RANK = 512
HEAD_PAD = 256

M_HEADS = 4
M_DQK = 128
M_DV = 256
M_QK_W = 512
M_V_W = 1024
M_CHUNK = 256

A_HEADS = 16
A_KV_HEADS = 2
A_HD = 64
A_W = 1024
WINDOW = 128

NEG_BIG = -1e30

EV_N = 3840
EV_COL_CQ = 1024
EV_COL_CKV = 1536
EV_COL_AGATE = 2048
EV_COL_KR = 3584
EV_TN = 3840

OD_N = 6656
OD_COL_MO = 2048
OD_COL_MZ = 3072
OD_COL_AQ = 4096
OD_COL_KV = 5120
OD_KV_W = 256
OD_COL_AZ = 5376
OD_COL_MG = 6400
OD_TN = 3328

PREP_W = 256


def _prep_kernel(a_ref, b_ref, o_ref, *, n_copy, n_shift, shift):
    t = pl.program_id(1)

    @pl.when(t < n_copy)
    def _():
        o_ref[...] = a_ref[...].astype(BF16).T

    @pl.when(jnp.logical_and(t >= n_copy, t < n_copy + n_shift))
    def _():
        o_ref[...] = jnp.concatenate([a_ref[shift:, :], b_ref[...]], axis=0).astype(BF16).T

    @pl.when(t == n_copy + n_shift)
    def _():
        pad = jnp.zeros((PREP_W - shift, b_ref.shape[1]), F32)
        o_ref[...] = jnp.concatenate([b_ref[...], pad], axis=0).astype(BF16).T


def _prep_weight(w, n_copy, n_shift, shift):
    layers, d, n = w.shape
    n_tiles = n_copy + n_shift + 1
    per = PREP_W // shift
    a_map = lambda l, t: (l, jnp.minimum(t, n_copy + n_shift - 1), 0)
    b_map = lambda l, t: (l, jnp.where(t == n_tiles - 1, n_copy * per, jnp.minimum((t + 1) * per, n // shift - 1)), 0)
    return pl.pallas_call(
        functools.partial(_prep_kernel, n_copy=n_copy, n_shift=n_shift, shift=shift),
        grid=(layers, n_tiles),
        in_specs=[pl.BlockSpec((None, PREP_W, d), a_map),
                  pl.BlockSpec((None, shift, d), b_map)],
        out_specs=pl.BlockSpec((None, d, PREP_W), lambda l, t: (l, 0, t)),
        out_shape=jax.ShapeDtypeStruct((layers, d, n_tiles * PREP_W), BF16),
        name="weight_layout",
    )(jnp.swapaxes(w, 1, 2), jnp.swapaxes(w, 1, 2))


def _ev_layout(w):
    return _prep_weight(w, n_copy=8, n_shift=6, shift=ROPE_DIM)


def _od_layout(w):
    return _prep_weight(w, n_copy=12, n_shift=13, shift=4 * M_HEADS)


def _rope_tables():
    t = jnp.arange(SEQ)
    row = (t // GRID_W).astype(F32)
    col = (t % GRID_W).astype(F32)
    half = ROPE_DIM // 2
    inv = ROPE_THETA ** (-jnp.arange(0, half, 2, dtype=F32) / half)
    ar = row[:, None] * inv
    ac = col[:, None] * inv
    zero = jnp.zeros_like(ar)
    cos64 = jnp.concatenate([jnp.cos(ar), jnp.cos(ar), jnp.cos(ac), jnp.cos(ac)], axis=-1)
    sa64 = jnp.concatenate([-jnp.sin(ar), zero, -jnp.sin(ac), zero], axis=-1)
    sb64 = jnp.concatenate([zero, jnp.sin(ar), zero, jnp.sin(ac)], axis=-1)
    ctx_zero = jnp.zeros((CTX_LEN, LANES), F32)
    cos = jnp.concatenate([jnp.tile(cos64, (1, 2)), jnp.ones((CTX_LEN, LANES), F32)], axis=0)
    sin_a = jnp.concatenate([jnp.tile(sa64, (1, 2)), ctx_zero], axis=0)
    sin_b = jnp.concatenate([jnp.tile(sb64, (1, 2)), ctx_zero], axis=0)
    return jnp.stack([cos, sin_a, sin_b])


def _rope(x, cos, sin_a, sin_b):
    up = pltpu.roll(x, LANES - ROPE_DIM // 4, 1)
    down = pltpu.roll(x, ROPE_DIM // 4, 1)
    return x * cos + up * sin_a + down * sin_b


def _dot(a, b):
    return jnp.dot(a, b, preferred_element_type=F32)


def _dot_nt(a, b):
    return lax.dot_general(a, b, (((1,), (1,)), ((), ())), preferred_element_type=F32)


def _dot_tn(a, b):
    return lax.dot_general(a, b, (((0,), (0,)), ((), ())), preferred_element_type=F32)


def _rms(x):
    return x * lax.rsqrt(jnp.mean(x * x, axis=-1, keepdims=True) + EPS)


def _modnorm(x, nw, mod):
    return (_rms(x) * nw) * (1.0 + mod[1:2, :]) + mod[0:1, :]


def _silu(x):
    return x * (1.0 / (1.0 + jnp.exp(-x)))


def _sigmoid(x):
    return 1.0 / (1.0 + jnp.exp(-x))


def _split3(x):
    hi = x.astype(BF16)
    r1 = x - hi.astype(F32)
    mid = r1.astype(BF16)
    lo = (r1 - mid.astype(F32)).astype(BF16)
    return hi, mid, lo


def _is_ctx_blk(i):
    return jnp.where(i < N_LAT_BLK, 0, 1)


def _ada_kernel(s_ref, w_ref, b_ref, o_ref):
    s = _silu(s_ref[...])
    hi = s.astype(BF16)
    lo = (s - hi.astype(F32)).astype(BF16)
    lhs = jnp.concatenate([hi, lo], axis=0)
    r = _dot(lhs, w_ref[0].astype(BF16))
    o_ref[0] = r[0:8] + r[8:16] + b_ref[0]


def _ada(c, c_ctx, ada_w, ada_b):
    tn = 1536
    s = jnp.concatenate([c.reshape(1, D_MODEL), c_ctx.reshape(1, D_MODEL), jnp.zeros((6, D_MODEL), F32)], axis=0)
    out = pl.pallas_call(
        _ada_kernel,
        grid=(DEPTH, 3 * D_MODEL // tn),
        in_specs=[pl.BlockSpec((8, D_MODEL), lambda l, j: (0, 0)),
                  pl.BlockSpec((1, D_MODEL, tn), lambda l, j: (l, 0, j)),
                  pl.BlockSpec((1, 1, tn), lambda l, j: (l, 0, j))],
        out_specs=pl.BlockSpec((1, 8, tn), lambda l, j: (l, 0, j)),
        out_shape=jax.ShapeDtypeStruct((DEPTH, 8, 3 * D_MODEL), F32),
        name="ada_mod",
    )(s, ada_w, ada_b.reshape(DEPTH, 1, 3 * D_MODEL))
    m = out[:, :2, :].reshape(DEPTH, 2, 3, D_MODEL)
    return jnp.concatenate([m, jnp.zeros((DEPTH, 2, 5, D_MODEL), F32)], axis=2)


def _split_rows_specs():
    return [pl.BlockSpec((ROW_BLK, D_MODEL), lambda i: (jnp.minimum(i, N_LAT_BLK - 1), 0)),
            pl.BlockSpec((CTX_LEN, D_MODEL), lambda i: (0, 0))]


def _split_rows(x_ref, c_ref, blk):
    return jnp.where(blk < N_LAT_BLK, x_ref[...], c_ref[...])


def _mm_kernel(h_ref, w_ref, o_ref):
    o_ref[...] = _dot(h_ref[...], w_ref[...])


def _in_proj(h, w, layer, tn):
    n = w.shape[2]
    return pl.pallas_call(
        _mm_kernel,
        grid=(n // tn, ROWS // IN_ROWS),
        in_specs=[pl.BlockSpec((IN_ROWS, D_MODEL), lambda j, i: (i, 0)),
                  pl.BlockSpec((None, D_MODEL, tn), lambda j, i: (layer, 0, j))],
        out_specs=pl.BlockSpec((IN_ROWS, tn), lambda j, i: (i, j)),
        out_shape=jax.ShapeDtypeStruct((ROWS, n), F32),
        name="in_proj",
    )(h, w)


def _first_proj_kernel(x_ref, c_ref, nw_ref, mod_ref, w_ref, o_ref):
    x = _split_rows(x_ref, c_ref, pl.program_id(0))
    o_ref[...] = _dot(_modnorm(x, nw_ref[...], mod_ref[0]).astype(BF16), w_ref[...])


def _first_proj(x, ctx, nw, mod, w, layer):
    n = w.shape[2]
    return pl.pallas_call(
        _first_proj_kernel,
        grid=(N_BLK,),
        in_specs=_split_rows_specs() + [pl.BlockSpec((1, D_MODEL), lambda i: (0, 0)),
                                        pl.BlockSpec((1, 8, D_MODEL), lambda i: (_is_ctx_blk(i), 0, 0)),
                                        pl.BlockSpec((None, D_MODEL, n), lambda i: (layer, 0, 0),
                                                     pipeline_mode=pl.Buffered(1))],
        out_specs=pl.BlockSpec((ROW_BLK, n), lambda i: (i, 0)),
        out_shape=jax.ShapeDtypeStruct((ROWS, n), F32),
        name="first_proj",
    )(x, ctx, nw.reshape(1, D_MODEL), mod, w)


Q_SCALE = (MLA_QK ** -0.5) * LOG2E


def _qproj_kernel(cq_ref, nw_ref, wn_ref, wr_ref, tab_ref, qn_ref, qr_ref):
    xn = (_rms(cq_ref[...]) * nw_ref[...]).astype(BF16)
    qn_ref[...] = (_dot(xn, wn_ref[...]) * Q_SCALE).astype(BF16)
    qr = _dot(xn, wr_ref[...])
    for p in range(MLA_HEADS // 2):
        sl = slice(p * LANES, (p + 1) * LANES)
        qr_ref[:, sl] = (_rope(qr[:, sl], tab_ref[0], tab_ref[1], tab_ref[2]) * Q_SCALE).astype(BF16)


def _layer_spec(w, layer):
    return pl.BlockSpec((None,) + w.shape[1:], lambda i: (layer, 0, 0))


def _q_proj(u, layer, nw, wn, wr, tabs):
    t = MM_ROWS
    return pl.pallas_call(
        _qproj_kernel,
        grid=(ROWS // t,),
        in_specs=[pl.BlockSpec((t, Q_RANK), lambda i: (i, EV_COL_CQ // Q_RANK)),
                  _layer_spec(nw, layer), _layer_spec(wn, layer), _layer_spec(wr, layer),
                  pl.BlockSpec((3, t, LANES), lambda i: (0, i, 0))],
        out_specs=[pl.BlockSpec((t, MLA_HEADS * MLA_NOPE), lambda i: (i, 0)),
                   pl.BlockSpec((t, MLA_HEADS * ROPE_DIM), lambda i: (i, 0))],
        out_shape=[jax.ShapeDtypeStruct((ROWS, MLA_HEADS * MLA_NOPE), BF16),
                   jax.ShapeDtypeStruct((ROWS, MLA_HEADS * ROPE_DIM), BF16)],
        name="mla_q_proj",
    )(u, nw, wn, wr, tabs)


def _kvproj_kernel(ckv_ref, kr_ref, nw_ref, wk_ref, wv_ref, tab_ref, kn_ref, kro_ref, v_ref):
    xn = (_rms(ckv_ref[...]) * nw_ref[...]).astype(BF16)
    kn_ref[...] = _dot(xn, wk_ref[...]).astype(BF16)
    v_ref[...] = _dot(xn, wv_ref[...]).astype(BF16)
    kr2 = _rope(kr_ref[...], tab_ref[0], tab_ref[1], tab_ref[2])
    lane = lax.broadcasted_iota(jnp.int32, kr2.shape, 1)
    kr_lo = jnp.where(lane < ROPE_DIM, kr2, 0.0)
    kro_ref[:, 0:LANES] = kr_lo.astype(BF16)
    kro_ref[:, LANES:2 * LANES] = pltpu.roll(kr_lo, ROPE_DIM, 1).astype(BF16)


def _kv_proj(u, layer, nw, wk, wv, tabs):
    t = MM_ROWS
    return pl.pallas_call(
        _kvproj_kernel,
        grid=(ROWS // t,),
        in_specs=[pl.BlockSpec((t, KV_RANK), lambda i: (i, EV_COL_CKV // KV_RANK)),
                  pl.BlockSpec((t, LANES), lambda i: (i, EV_COL_KR // LANES)),
                  _layer_spec(nw, layer), _layer_spec(wk, layer), _layer_spec(wv, layer),
                  pl.BlockSpec((3, t, LANES), lambda i: (0, i, 0))],
        out_specs=[pl.BlockSpec((t, MLA_HEADS * MLA_NOPE), lambda i: (i, 0)),
                   pl.BlockSpec((t, 2 * LANES), lambda i: (i, 0)),
                   pl.BlockSpec((t, MLA_W), lambda i: (i, 0))],
        out_shape=[jax.ShapeDtypeStruct((ROWS, MLA_HEADS * MLA_NOPE), BF16),
                   jax.ShapeDtypeStruct((ROWS, 2 * LANES), BF16),
                   jax.ShapeDtypeStruct((ROWS, MLA_W), BF16)],
        name="mla_kv_proj",
    )(u, u, nw, wk, wv, tabs)


FL_TQ = 768
FL_TK = 768
FL_LAST = ROWS // FL_TQ - 1
FL_LAT_IN_LAST = SEQ - FL_LAST * FL_TQ


def _key_chunks():
    return [(i * FL_TK, FL_TK) for i in range(ROWS // FL_TK)]


def _attend(q, kn_ref, kr_ref, v_ref, sa_ref, sb_ref, chunks):
    rows = q.shape[0]
    bufs = (sa_ref, sb_ref)

    def scores_into(j):
        off, size = chunks[j]
        k = jnp.concatenate([kn_ref[off:off + size, :], kr_ref[off:off + size, :]], axis=1)
        s = _dot_nt(q, k)
        bufs[j % 2][0:rows, 0:size] = s
        return jnp.max(s, axis=1, keepdims=True)

    def ones_block(size):
        lane = lax.broadcasted_iota(jnp.int32, (size, HEAD_PAD - MLA_VDIM), 1)
        return jnp.where(lane == 0, 1.0, 0.0).astype(BF16)

    def update(j, m_chunk, m_prev, acc):
        off, size = chunks[j]
        m_new = jnp.maximum(m_prev, m_chunk)
        alpha = jnp.exp2(m_prev - m_new)
        p = jnp.exp2(bufs[j % 2][0:rows, 0:size] - m_new).astype(BF16)
        v_ext = jnp.concatenate([v_ref[off:off + size, :], ones_block(size)], axis=1)
        return m_new, alpha * acc + _dot(p, v_ext)

    m = jnp.full((rows, 1), NEG_BIG, F32)
    acc = jnp.zeros((rows, HEAD_PAD), F32)
    m_next = scores_into(0)
    for j in range(len(chunks)):
        m_chunk = m_next
        if j + 1 < len(chunks):
            m_next = scores_into(j + 1)
        m, acc = update(j, m_chunk, m, acc)
    return acc[:, :MLA_VDIM] / acc[:, MLA_VDIM:MLA_VDIM + 1]


def _flash_kernel(qn_ref, qr_ref, kn_ref, kr_ref, v_ref, g_ref, o_ref, sa_ref, sb_ref):
    i = pl.program_id(1)
    chunks = _key_chunks()
    refs = (kn_ref, kr_ref, v_ref, sa_ref, sb_ref)

    def queries(lo, hi):
        return jnp.concatenate([qn_ref[lo:hi, :], qr_ref[lo:hi, :]], axis=1)

    @pl.when(i < FL_LAST)
    def _():
        o = _attend(queries(0, FL_TQ), *refs, chunks)
        o_ref[...] = (o * _silu(g_ref[...])).astype(BF16)

    @pl.when(i == FL_LAST)
    def _():
        nl = FL_LAT_IN_LAST
        o = _attend(queries(0, nl), *refs, chunks)
        o_ref[0:nl, :] = (o * _silu(g_ref[0:nl, :])).astype(BF16)
        oc = _attend(queries(nl, FL_TQ), *refs, [(SEQ, CTX_LEN)])
        o_ref[nl:, :] = (oc * _silu(g_ref[nl:, :])).astype(BF16)


def _flash(qn, qr, kn, kr, v, u):
    return pl.pallas_call(
        _flash_kernel,
        grid=(MLA_HEADS, ROWS // FL_TQ),
        in_specs=[pl.BlockSpec((FL_TQ, MLA_NOPE), lambda h, i: (i, h)),
                  pl.BlockSpec((FL_TQ, LANES), lambda h, i: (i, h // 2)),
                  pl.BlockSpec((ROWS, MLA_NOPE), lambda h, i: (0, h)),
                  pl.BlockSpec((ROWS, LANES), lambda h, i: (0, h % 2)),
                  pl.BlockSpec((ROWS, MLA_VDIM), lambda h, i: (0, h)),
                  pl.BlockSpec((FL_TQ, MLA_VDIM), lambda h, i: (i, EV_COL_AGATE // MLA_VDIM + h))],
        out_specs=pl.BlockSpec((FL_TQ, MLA_VDIM), lambda h, i: (i, h)),
        out_shape=jax.ShapeDtypeStruct((ROWS, MLA_W), BF16),
        scratch_shapes=[pltpu.VMEM((FL_TQ, FL_TK), F32), pltpu.VMEM((FL_TQ, FL_TK), F32)],
        name="mla_attention",
    )(qn, qr, kn, kr, v, u)


def _pool_kernel(prev_ref, cur_ref, next_ref, gate_ref, w_ref, sc_ref, o_ref):
    i = pl.program_id(0)
    first = jnp.logical_or(i == 0, i == N_LAT_BLK)
    last = i >= N_LAT_BLK - 1
    lo_lim = jnp.where(first, 0, -POOL_HALO)
    hi_lim = jnp.where(last, ROW_BLK - 1, ROW_BLK - 1 + POOL_HALO)
    src_rows = ROW_BLK + 2 * POOL_HALO
    t = lax.broadcasted_iota(jnp.int32, (ROW_BLK, src_rows), 0)
    j = lax.broadcasted_iota(jnp.int32, (ROW_BLK, src_rows), 1) - POOL_HALO
    t1 = lax.broadcasted_iota(jnp.int32, (ROW_BLK, 1), 0)
    for g, w in enumerate(POOL_WINDOWS):
        sl = slice(g * POOL_GC, (g + 1) * POOL_GC)
        cur = cur_ref[:, sl]
        src = jnp.concatenate([prev_ref[ROW_BLK - POOL_HALO:, sl], cur, next_ref[:POOL_HALO, sl]], axis=0)
        lo = jnp.maximum(t - w // 2, lo_lim)
        hi = jnp.minimum(t + w // 2 - 1, hi_lim)
        band = jnp.where(jnp.logical_and(j >= lo, j <= hi), 1.0, 0.0).astype(BF16)
        cnt = (jnp.minimum(t1 + w // 2 - 1, hi_lim) - jnp.maximum(t1 - w // 2, lo_lim) + 1).astype(F32)
        s_hi, s_mid, s_lo = _split3(src)
        wsum = _dot(band, s_hi) + _dot(band, s_mid) + _dot(band, s_lo)
        z = (wsum / cnt - cur).astype(BF16)
        y = _dot(z, w_ref[g]) * sc_ref[:, sl]
        o_ref[:, sl] = (y * _silu(gate_ref[:, sl])).astype(BF16)


def _pool(u, w_pool, pool_scale):
    def prev_map(i):
        return (jnp.where(i < N_LAT_BLK, jnp.maximum(i - 1, 0), N_LAT_BLK), 0)

    def next_map(i):
        return (jnp.where(i < N_LAT_BLK, jnp.minimum(i + 1, N_LAT_BLK - 1), N_LAT_BLK), 0)

    return pl.pallas_call(
        _pool_kernel,
        grid=(N_BLK,),
        in_specs=[pl.BlockSpec((ROW_BLK, POOL_W), prev_map),
                  pl.BlockSpec((ROW_BLK, POOL_W), lambda i: (i, 0)),
                  pl.BlockSpec((ROW_BLK, POOL_W), next_map),
                  pl.BlockSpec((ROW_BLK, POOL_W), lambda i: (i, 1)),
                  pl.BlockSpec((4, POOL_GC, POOL_GC), lambda i: (0, 0, 0)),
                  pl.BlockSpec((1, POOL_W), lambda i: (0, 0))],
        out_specs=pl.BlockSpec((ROW_BLK, POOL_W), lambda i: (i, 0)),
        out_shape=jax.ShapeDtypeStruct((ROWS, POOL_W), BF16),
        name="pool_mix",
    )(u, u, u, u, w_pool.astype(BF16), pool_scale.reshape(1, POOL_W))


def _zero_from(v):
    bits = pltpu.bitcast(v, jnp.uint32)
    return pltpu.bitcast((bits >> 16) >> 16, F32)


def _lane_tile_sum(v):
    return sum(v[:, t * LANES:(t + 1) * LANES] for t in range(v.shape[1] // LANES))


def _outproj_step(a_ref, b_ref, w_ref, x_ref, c_ref, mod_ref, wbf_ref, y_ref, finish):
    i = pl.program_id(0)

    @pl.when(i == 0)
    def _():
        wbf_ref[...] = w_ref[...].astype(BF16)
        y_ref[...] = jnp.zeros_like(y_ref)

    done = finish(_split_rows(x_ref, c_ref, i - 1) + mod_ref[0, 2:3, :] * y_ref[...])
    k1 = a_ref.shape[1]
    y = _dot(a_ref[...], wbf_ref[0:k1, :]) + _dot(b_ref[...], wbf_ref[k1:, :])
    y_ref[:, :D_MODEL - LANES] = y[:, :D_MODEL - LANES]
    y_ref[:, D_MODEL - LANES:] = y[:, D_MODEL - LANES:] + _zero_from(_lane_tile_sum(done))


def _outproj_kernel(a_ref, b_ref, w_ref, x_ref, c_ref, mod_ref, nw_ref, modn_ref, xo_ref, h_ref, wbf_ref, y_ref):
    def finish(xn):
        xo_ref[...] = xn
        h = _modnorm(xn, nw_ref[...], modn_ref[0])
        h_ref[...] = h.astype(BF16)
        return h

    _outproj_step(a_ref, b_ref, w_ref, x_ref, c_ref, mod_ref, wbf_ref, y_ref, finish)


def _outproj_final_kernel(a_ref, b_ref, w_ref, x_ref, c_ref, mod_ref, nw_ref, o_ref, wbf_ref, y_ref):
    def finish(xn):
        o = _rms(xn) * nw_ref[...]
        o_ref[...] = o
        return o

    _outproj_step(a_ref, b_ref, w_ref, x_ref, c_ref, mod_ref, wbf_ref, y_ref, finish)


def _out_proj(a, b, w, layer, x, x_ctx, mod, nw_next, mod_next):
    k1, k2 = a.shape[1], b.shape[1]
    n_blk = N_LAT_BLK if mod_next is None else N_BLK
    cur = lambda i: (jnp.minimum(i, n_blk - 1), 0)
    row = lambda i: (jnp.maximum(i - 1, 0), 0)
    full = lambda i: (0, 0)
    stream = lambda i: (_is_ctx_blk(i - 1), 0, 0)
    ctx_blk = x_ctx.shape[0] // CTX_LEN - 1
    common = [pl.BlockSpec((ROW_BLK, k1), cur), pl.BlockSpec((ROW_BLK, k2), cur),
              pl.BlockSpec((None, D_MODEL, D_MODEL), lambda i: (layer, 0, 0), pipeline_mode=pl.Buffered(1)),
              pl.BlockSpec((ROW_BLK, D_MODEL), lambda i: (jnp.clip(i - 1, 0, N_LAT_BLK - 1), 0)),
              pl.BlockSpec((CTX_LEN, D_MODEL), lambda i: (ctx_blk, 0)),
              pl.BlockSpec((1, 8, D_MODEL), stream),
              pl.BlockSpec((1, D_MODEL), full)]
    scratch = [pltpu.VMEM((D_MODEL, D_MODEL), BF16), pltpu.VMEM((ROW_BLK, D_MODEL), F32)]
    params = pltpu.CompilerParams(dimension_semantics=("arbitrary",))
    if mod_next is None:
        return pl.pallas_call(
            _outproj_final_kernel,
            grid=(n_blk + 1,),
            in_specs=common,
            out_specs=pl.BlockSpec((ROW_BLK, D_MODEL), row),
            out_shape=jax.ShapeDtypeStruct((SEQ, D_MODEL), F32),
            scratch_shapes=scratch,
            compiler_params=params,
            name="out_proj_final",
        )(a, b, w, x, x_ctx, mod, nw_next.reshape(1, D_MODEL))
    return pl.pallas_call(
        _outproj_kernel,
        grid=(n_blk + 1,),
        in_specs=common + [pl.BlockSpec((1, 8, D_MODEL), stream)],
        out_specs=[pl.BlockSpec((ROW_BLK, D_MODEL), row), pl.BlockSpec((ROW_BLK, D_MODEL), row)],
        out_shape=[jax.ShapeDtypeStruct((ROWS, D_MODEL), F32), jax.ShapeDtypeStruct((ROWS, D_MODEL), BF16)],
        scratch_shapes=scratch,
        compiler_params=params,
        name="out_proj",
    )(a, b, w, x, x_ctx, mod, nw_next.reshape(1, D_MODEL), mod_next)


M_QSCALE = M_DQK ** -0.5


def _log_sigmoid(x):
    return jnp.minimum(x, 0.0) - jnp.log(1.0 + jnp.exp(-jnp.abs(x)))


def _mlstm_products(q_ref, k_ref, c_ref, n_ref, qk_ref, cq_ref, nq_ref, reverse):
    d = 1 if reverse else 0
    for h in range(M_HEADS):
        st = M_HEADS * d + h
        q = (q_ref[:, h * M_DQK:(h + 1) * M_DQK] * M_QSCALE).astype(BF16)
        qk_ref[st] = _dot_nt(k_ref[:, h * M_DQK:(h + 1) * M_DQK].astype(BF16), q)
        cq_ref[st] = _dot_nt(c_ref[st].astype(BF16), q)
        nq_ref[st] = _dot_nt(jnp.broadcast_to(n_ref[st:st + 1, :], (8, M_DQK)).astype(BF16), q)


def _mlstm_dir(k_ref, v_ref, g_ref, gb_ref, ht_ref, c_ref, n_ref, m_ref, qk_ref, cq_ref, nq_ref, reverse):
    T = M_CHUNK
    d = 1 if reverse else 0
    base = 2 * M_HEADS * d
    g = g_ref[...] + gb_ref[...]
    glf = _log_sigmoid(g)
    row = lax.broadcasted_iota(jnp.int32, (T, T), 0)
    col = lax.broadcasted_iota(jnp.int32, (T, T), 1)
    tri = jnp.where((col >= row) if reverse else (col <= row), 1.0, 0.0).astype(BF16)
    reach = (row >= col) if reverse else (row <= col)
    p_hi, p_mid, p_lo = _split3(glf)
    bcum = _dot(tri, p_hi) + _dot(tri, p_mid) + _dot(tri, p_lo)
    g_sh = pltpu.roll(g, M_HEADS, 1)
    r_all = g_sh - bcum
    end = 0 if reverse else T - 1
    b_last = bcum[end:end + 1, :]
    a_all = b_last - bcum + g_sh
    m_prev_row = m_ref[d:d + 1, :]
    m_new_row = jnp.maximum(b_last + m_prev_row, jnp.max(a_all, axis=0, keepdims=True))
    decay_row = jnp.exp(b_last + m_prev_row - m_new_row)
    w_all = jnp.exp(a_all - m_new_row)
    m_ref[d:d + 1, :] = m_new_row
    b_t = bcum.T
    for h in range(M_HEADS):
        cf = base + M_HEADS + h
        st = M_HEADS * d + h
        m_prev = m_prev_row[:, cf:cf + 1]
        decay = decay_row[:, cf:cf + 1]
        r_mat = jnp.where(reach, r_all[:, cf:cf + 1], -jnp.inf)
        mx = jnp.maximum(m_prev, jnp.max(r_mat, axis=0, keepdims=True))
        iw = jnp.exp(m_prev - mx)
        dm = jnp.exp(r_mat - mx)

        v = v_ref[:, h * M_DV:(h + 1) * M_DV].astype(BF16)
        s = qk_ref[st] * dm
        den = iw * nq_ref[st, 0:1, :] + jnp.sum(s, axis=0, keepdims=True)
        inv = 1.0 / jnp.maximum(jnp.abs(den), jnp.exp(-(b_t[cf:cf + 1, :] + mx)))
        ht_ref[h * M_DV:(h + 1) * M_DV, :] = _dot_tn(v, (s * inv).astype(BF16)) + cq_ref[st] * (iw * inv)

        wk = w_all[:, cf:cf + 1] * k_ref[:, h * M_DQK:(h + 1) * M_DQK]
        c_ref[st] = decay * c_ref[st] + _dot_tn(v, wk.astype(BF16))
        n_ref[st:st + 1, :] = decay * n_ref[st:st + 1, :] + jnp.sum(wk, axis=0, keepdims=True)


def _mlstm_kernel(qf_ref, kf_ref, vf_ref, gf_ref, qb_ref, kb_ref, vb_ref, gb_ref, bias_ref,
                  hf_ref, hb_ref, c_ref, n_ref, m_ref, qk_ref, cq_ref, nq_ref):
    @pl.when(pl.program_id(0) == 0)
    def _():
        c_ref[...] = jnp.zeros_like(c_ref)
        n_ref[...] = jnp.zeros_like(n_ref)
        m_ref[...] = jnp.zeros_like(m_ref)

    products = (qk_ref, cq_ref, nq_ref)
    _mlstm_products(qf_ref, kf_ref, c_ref, n_ref, *products, reverse=False)
    _mlstm_products(qb_ref, kb_ref, c_ref, n_ref, *products, reverse=True)
    _mlstm_dir(kf_ref, vf_ref, gf_ref, bias_ref, hf_ref, c_ref, n_ref, m_ref, *products, reverse=False)
    _mlstm_dir(kb_ref, vb_ref, gb_ref, bias_ref, hb_ref, c_ref, n_ref, m_ref, *products, reverse=True)


def _mlstm(u, gate_b):
    fwd = lambda k: jnp.where(k == 0, N_LAT_BLK, k - 1)
    bwd = lambda k: jnp.where(k == 0, N_LAT_BLK, N_LAT_BLK - k)
    mg_blk = OD_COL_MG // LANES

    def specs(rmap):
        return [pl.BlockSpec((M_CHUNK, M_QK_W), lambda k: (rmap(k), 0)),
                pl.BlockSpec((M_CHUNK, M_QK_W), lambda k: (rmap(k), 1)),
                pl.BlockSpec((M_CHUNK, M_V_W), lambda k: (rmap(k), 1)),
                pl.BlockSpec((M_CHUNK, LANES), lambda k: (rmap(k), mg_blk))]

    bias = jnp.concatenate([gate_b.astype(F32), jnp.zeros((LANES - 4 * M_HEADS,), F32)]).reshape(1, LANES)
    return pl.pallas_call(
        _mlstm_kernel,
        grid=(N_BLK,),
        in_specs=specs(fwd) + specs(bwd) + [pl.BlockSpec((1, LANES), lambda k: (0, 0))],
        out_specs=[pl.BlockSpec((M_V_W, M_CHUNK), lambda k: (0, fwd(k))),
                   pl.BlockSpec((M_V_W, M_CHUNK), lambda k: (0, bwd(k)))],
        out_shape=[jax.ShapeDtypeStruct((M_V_W, ROWS), F32), jax.ShapeDtypeStruct((M_V_W, ROWS), F32)],
        scratch_shapes=[pltpu.VMEM((2 * M_HEADS, M_DV, M_DQK), F32),
                        pltpu.VMEM((2 * M_HEADS, M_DQK), F32),
                        pltpu.VMEM((8, LANES), F32),
                        pltpu.VMEM((2 * M_HEADS, M_CHUNK, M_CHUNK), F32),
                        pltpu.VMEM((2 * M_HEADS, M_DV, M_CHUNK), F32),
                        pltpu.VMEM((2 * M_HEADS, 8, M_CHUNK), F32)],
        compiler_params=pltpu.CompilerParams(dimension_semantics=("arbitrary",)),
        name="mlstm_scan",
    )(u, u, u, u, u, u, u, u, bias)


def _mlstm_post_kernel(hf_ref, hb_ref, o_ref, z_ref, nw_ref, y_ref):
    for h in range(M_HEADS):
        sl = slice(h * M_DV, (h + 1) * M_DV)
        ht = hf_ref[sl, :] + hb_ref[sl, :]
        hn = (ht * lax.rsqrt(jnp.mean(ht * ht, axis=0, keepdims=True) + EPS)).T * nw_ref[:, sl]
        y_ref[:, sl] = (hn * _sigmoid(o_ref[:, sl]) * _silu(z_ref[:, sl])).astype(BF16)


def _mlstm_post(hf, hb, u, head_norm):
    row = lambda i: (i, 0)
    col = lambda i: (0, i)
    t = MM_ROWS
    return pl.pallas_call(
        _mlstm_post_kernel,
        grid=(ROWS // t,),
        in_specs=[pl.BlockSpec((M_V_W, t), col), pl.BlockSpec((M_V_W, t), col),
                  pl.BlockSpec((t, M_V_W), lambda i: (i, OD_COL_MO // M_V_W)),
                  pl.BlockSpec((t, M_V_W), lambda i: (i, OD_COL_MZ // M_V_W)),
                  pl.BlockSpec((1, M_V_W), lambda i: (0, 0))],
        out_specs=pl.BlockSpec((t, M_V_W), row),
        out_shape=jax.ShapeDtypeStruct((ROWS, M_V_W), BF16),
        name="mlstm_post",
    )(hf, hb, u, u, head_norm.reshape(1, M_V_W))


W_NLAT = SEQ // WINDOW
W_NBLK = ROWS // WINDOW
W_QSCALE = (A_HD ** -0.5) * LOG2E
W_KEYS = 3 * WINDOW + CTX_LEN
W_PAIRS = A_HEADS // A_KV_HEADS // 2


def _win_kernel(sink_ref, q_ref, az0_ref, az1_ref, az2_ref, az3_ref, kvp_ref, kvc_ref, kvn_ref, kvx_ref,
                tq_ref, tp_ref, tn_ref, o_ref, s_ref):
    az_refs = (az0_ref, az1_ref, az2_ref, az3_ref)
    bi = pl.program_id(0)
    is_lat = bi < W_NLAT
    tq = (tq_ref[0], tq_ref[1], tq_ref[2])
    lane_k = lax.broadcasted_iota(jnp.int32, (W_KEYS, LANES), 1)

    kj = lax.broadcasted_iota(jnp.int32, (WINDOW, WINDOW), 0)
    qi = lax.broadcasted_iota(jnp.int32, (WINDOW, WINDOW), 1)
    ok_prev = jnp.logical_and(jnp.logical_and(is_lat, bi >= 1), kj >= qi)
    ok_cur = jnp.logical_and(is_lat, kj >= 0)
    ok_next = jnp.logical_and(jnp.logical_and(is_lat, bi <= W_NLAT - 2), kj <= qi)
    ok_ctx = lax.broadcasted_iota(jnp.int32, (CTX_LEN, WINDOW), 1) >= 0
    valid = jnp.concatenate([ok_prev, ok_cur, ok_next, ok_ctx], axis=0)
    valid4 = jnp.concatenate([valid] * W_PAIRS, axis=1)
    lane4 = lax.broadcasted_iota(jnp.int32, (1, W_PAIRS * WINDOW), 1)

    kk = jnp.concatenate([_rope(kvp_ref[:, 0:LANES], tp_ref[0], tp_ref[1], tp_ref[2]),
                          _rope(kvc_ref[:, 0:LANES], *tq),
                          _rope(kvn_ref[:, 0:LANES], tn_ref[0], tn_ref[1], tn_ref[2]),
                          kvx_ref[:, 0:LANES]], axis=0)
    vv = jnp.concatenate([kvp_ref[:, LANES:2 * LANES], kvc_ref[:, LANES:2 * LANES],
                          kvn_ref[:, LANES:2 * LANES], kvx_ref[:, LANES:2 * LANES]], axis=0)
    kk_sw = pltpu.roll(kk, A_HD, 1)
    vv_sw = pltpu.roll(vv, A_HD, 1)
    low = lane_k < A_HD

    v_halves = []
    maxima = []
    for g in range(A_KV_HEADS):
        k_src = (kk, kk_sw) if g == 0 else (kk_sw, kk)
        v_src = (vv, vv_sw) if g == 0 else (vv_sw, vv)
        k_half = [jnp.where(low, k_src[0], 0.0).astype(BF16), jnp.where(low, 0.0, k_src[1]).astype(BF16)]
        v_halves.append([jnp.where(low, v_src[0], 0.0).astype(BF16), jnp.where(low, 0.0, v_src[1]).astype(BF16)])
        cols = [(g * W_PAIRS + p) * LANES for p in range(W_PAIRS)]
        qs = jnp.concatenate([(_rope(q_ref[:, c:c + LANES], *tq) * W_QSCALE).astype(BF16) for c in cols],
                             axis=0)
        for par in range(2):
            s = jnp.where(valid4, _dot_nt(k_half[par], qs), -jnp.inf)
            s_ref[2 * g + par] = s
            maxima.append(jnp.max(s, axis=0, keepdims=True))

    for g in range(A_KV_HEADS):
        cols = [(g * W_PAIRS + p) * LANES for p in range(W_PAIRS)]
        out_t = jnp.zeros((LANES, W_PAIRS * WINDOW), F32)
        for par in range(2):
            snk = jnp.full((1, W_PAIRS * WINDOW), sink_ref[g * 8 + par] * LOG2E, F32)
            for p in range(1, W_PAIRS):
                snk = jnp.where(lane4 >= p * WINDOW, sink_ref[g * 8 + 2 * p + par] * LOG2E, snk)
            m = jnp.maximum(maxima[2 * g + par], snk)
            e = jnp.exp2(s_ref[2 * g + par] - m)
            den = jnp.sum(e, axis=0, keepdims=True) + jnp.exp2(snk - m)
            out_t = out_t + _dot_tn(v_halves[g][par], e.astype(BF16)) * (1.0 / den)
        out = out_t.T
        for p, c in enumerate(cols):
            grp = c // LANES
            az = az_refs[grp // 2][:, (grp % 2) * LANES:(grp % 2 + 1) * LANES]
            o_ref[:, c:c + LANES] = (out[p * WINDOW:(p + 1) * WINDOW] * _silu(az)).astype(BF16)


def _win_attn(u, sink, tabs):
    kvb = OD_COL_KV // OD_KV_W
    cur = lambda b: jnp.minimum(b, W_NLAT - 1)
    prv = lambda b: jnp.clip(b - 1, 0, W_NLAT - 1)
    nxt = lambda b: jnp.clip(b + 1, 0, W_NLAT - 1)
    tab = lambda f: pl.BlockSpec((3, WINDOW, LANES), lambda b: (0, f(b), 0))
    az_spec = lambda n: pl.BlockSpec((WINDOW, 2 * LANES), lambda b: (b, OD_COL_AZ // (2 * LANES) + n))
    return pl.pallas_call(
        _win_kernel,
        grid=(W_NBLK,),
        in_specs=[pl.BlockSpec(memory_space=pltpu.SMEM),
                  pl.BlockSpec((WINDOW, A_W), lambda b: (b, OD_COL_AQ // A_W)),
                  az_spec(0), az_spec(1), az_spec(2), az_spec(3),
                  pl.BlockSpec((WINDOW, OD_KV_W), lambda b: (prv(b), kvb)),
                  pl.BlockSpec((WINDOW, OD_KV_W), lambda b: (cur(b), kvb)),
                  pl.BlockSpec((WINDOW, OD_KV_W), lambda b: (nxt(b), kvb)),
                  pl.BlockSpec((CTX_LEN, OD_KV_W), lambda b: (SEQ // CTX_LEN, kvb)),
                  tab(lambda b: b), tab(prv), tab(nxt)],
        out_specs=pl.BlockSpec((WINDOW, A_W), lambda b: (b, 0)),
        out_shape=jax.ShapeDtypeStruct((ROWS, A_W), BF16),
        scratch_shapes=[pltpu.VMEM((2 * A_KV_HEADS, W_KEYS, W_PAIRS * WINDOW), F32)],
        name="window_attention",
    )(sink, u, u, u, u, u, u, u, u, u, tabs, tabs, tabs)


def kernel(x, c, ctx, c_ctx, ada_w, ada_b, norm_w, ev_w_in, ev_q_norm, ev_kv_norm, ev_w_uq, ev_w_ukv,
           ev_w_pool, ev_pool_scale, ev_w_out, od_w_in, od_gate_b, od_head_norm, od_sink, od_w_out, final_norm):
    tabs = _rope_tables()
    mod = _ada(c, c_ctx, ada_w, ada_b)

    n_ev = ev_w_in.shape[0]
    ev_w = _ev_layout(ev_w_in)
    od_w = _od_layout(od_w_in)
    wq = ev_w_uq.astype(BF16).reshape(n_ev, Q_RANK, MLA_HEADS, MLA_QK)
    wq_nope = wq[..., :MLA_NOPE].reshape(n_ev, Q_RANK, MLA_HEADS * MLA_NOPE)
    wq_rope = wq[..., MLA_NOPE:].reshape(n_ev, Q_RANK, MLA_HEADS * ROPE_DIM)
    wkv = ev_w_ukv.astype(BF16).reshape(n_ev, KV_RANK, MLA_HEADS, MLA_NOPE + MLA_VDIM)
    wk_nope = wkv[..., :MLA_NOPE].reshape(n_ev, KV_RANK, MLA_HEADS * MLA_NOPE)
    wv = wkv[..., MLA_NOPE:].reshape(n_ev, KV_RANK, MLA_W)
    q_norm = ev_q_norm.reshape(n_ev, 1, Q_RANK)
    kv_norm = ev_kv_norm.reshape(n_ev, 1, KV_RANK)

    h = None
    res_lat, res_ctx = x[0], ctx[0]
    out = None
    for i in range(DEPTH):
        j = i // 2
        if i % 2 == 0:
            if i == 0:
                u = _first_proj(x[0], ctx[0], norm_w[0], mod[0], ev_w, j)
            else:
                u = _in_proj(h, ev_w, j, EV_TN)
            qn, qr = _q_proj(u, j, q_norm, wq_nope, wq_rope, tabs)
            kn, kr, v = _kv_proj(u, j, kv_norm, wk_nope, wv, tabs)
            a = _pool(u, ev_w_pool[j], ev_pool_scale[j])
            b = _flash(qn, qr, kn, kr, v, u)
            w_out = ev_w_out
        else:
            u = _in_proj(h, od_w, j, OD_TN)
            hf, hb = _mlstm(u, od_gate_b[j])
            a = _mlstm_post(hf, hb, u, od_head_norm[j])
            b = _win_attn(u, od_sink[j].astype(F32), tabs)
            w_out = od_w_out
        if i < DEPTH - 1:
            xs, h = _out_proj(a, b, w_out, j, res_lat, res_ctx, mod[i], norm_w[i + 1], mod[i + 1])
            res_lat = res_ctx = xs
        else:
            out = _out_proj(a, b, w_out, j, res_lat, res_ctx, mod[i], final_norm, None)
    return out[None]
```
